```python
import math
import jax, jax.numpy as jnp
from jax import lax
import numpy as np

D_MODEL = 1024
BATCH = 4
SEQ = 4096
DEPTH = 1
DEC_BATCH = 128
DEC_SEQ = 1
PAST_LEN = 2048
PAGE_SIZE = 128

MIX_WIDTH = D_MODEL
SB_WIDTH = MIX_WIDTH // 2
SSM_WIDTH = MIX_WIDTH - SB_WIDTH
SB_HEAD_DIM = 64
SB_HEADS = SB_WIDTH // SB_HEAD_DIM
SB_BIAS_LO = -7.0
SB_BIAS_HI = -4.0
SSM_GROUP = 16
SSM_GROUPS = SSM_WIDTH // SSM_GROUP
SSM_STATE = 64
IN_WIDTH = 3 * SB_WIDTH + SSM_WIDTH
MEM_TOKENS = 256
MEM_HEADS = 4
MEM_HEAD_DIM = D_MODEL // MEM_HEADS
D_FF = ((8 * D_MODEL // 3 + 127) // 128) * 128
CONV_W = 3
BLOCK_Q = 128
EPS = 1e-6
DT_MIN = 1e-3
DT_MAX = 1e-1

kernel_name = "hymba_s5_stickbreaking_memxattn_convffn_step"


def _rmsnorm(x, g):
    xf = x.astype(jnp.float32)
    y = xf * lax.rsqrt(jnp.mean(xf * xf, axis=-1, keepdims=True) + EPS)
    return (y * g.astype(jnp.float32)).astype(x.dtype)


def _sb_block(q_blk, q_pos, k, v, k_pos, sb_bias):
    z = jnp.einsum('bqhd,bkhd->bhqk', q_blk.astype(jnp.float32), k.astype(jnp.float32)) / math.sqrt(SB_HEAD_DIM)
    z = z + sb_bias.astype(jnp.float32)[None, :, None, None]
    causal = k_pos[None, :] < q_pos[:, None]
    log_not_beta = jnp.where(causal, -jax.nn.softplus(z), 0.0)
    after = lax.cumsum(log_not_beta, axis=3, reverse=True) - log_not_beta
    a = jnp.where(causal, jnp.exp(jax.nn.log_sigmoid(z) + after), 0.0)
    return jnp.einsum('bhqk,bkhd->bqhd', a, v.astype(jnp.float32)).astype(v.dtype)


def _stick_breaking(q, k, v, q_pos, k_pos, sb_bias):
    b, n_q, h, dh = q.shape
    if n_q > BLOCK_Q and n_q % BLOCK_Q == 0:
        n_blk = n_q // BLOCK_Q
        qb = q.reshape(b, n_blk, BLOCK_Q, h, dh).transpose(1, 0, 2, 3, 4)
        pb = q_pos.reshape(n_blk, BLOCK_Q)
        out = lax.map(lambda a: _sb_block(a[0], a[1], k, v, k_pos, sb_bias), (qb, pb))
        return out.transpose(1, 0, 2, 3, 4).reshape(b, n_q, h, dh)
    return _sb_block(q, q_pos, k, v, k_pos, sb_bias)


def _s5(u, h0, lam_re, lam_im, log_dt, b_re, b_im, c_re, c_im, d_skip, w_glu, b_glu):
    bsz, t = u.shape[:2]
    ug = u.astype(jnp.float32).reshape(bsz, t, SSM_GROUPS, SSM_GROUP)
    lam = lax.complex(lam_re.astype(jnp.float32), lam_im.astype(jnp.float32))
    dt = jnp.exp(log_dt.astype(jnp.float32))[:, None]
    lam_bar = jnp.exp(lam * dt)
    bmat = lax.complex(b_re.astype(jnp.float32), b_im.astype(jnp.float32))
    cmat = lax.complex(c_re.astype(jnp.float32), c_im.astype(jnp.float32))
    b_bar = ((lam_bar - 1.0) / lam)[..., None] * bmat
    bu = jnp.einsum('gnp,btgp->btgn', b_bar, ug)
    bu = bu.at[:, 0].add(lam_bar * h0)
    decay = jnp.broadcast_to(lam_bar, bu.shape)

    def combine(e1, e2):
        a1, x1 = e1
        a2, x2 = e2
        return a2 * a1, a2 * x1 + x2

    _, hs = lax.associative_scan(combine, (decay, bu), axis=1)
    y = jnp.einsum('gpn,btgn->btgp', cmat, hs).real + d_skip.astype(jnp.float32).reshape(SSM_GROUPS, SSM_GROUP) * ug
    y = jax.nn.gelu(y.reshape(bsz, t, SSM_WIDTH))
    out = y * jax.nn.sigmoid(y @ w_glu.astype(jnp.float32) + b_glu.astype(jnp.float32))
    return out.astype(u.dtype), hs[:, -1]


def _mem_kv(mem, g_mem_kv, w_mk, w_mv):
    b = mem.shape[0]
    m = _rmsnorm(mem, g_mem_kv)
    mk = (m @ w_mk).reshape(b, MEM_TOKENS, MEM_HEADS, MEM_HEAD_DIM)
    mv = (m @ w_mv).reshape(b, MEM_TOKENS, MEM_HEADS, MEM_HEAD_DIM)
    return mk, mv


def _mem_attend(h, mk, mv, w_mq, w_mo):
    b, t, _ = h.shape
    q = (h @ w_mq).reshape(b, t, MEM_HEADS, MEM_HEAD_DIM)
    s = jnp.einsum('bqhd,bkhd->bhqk', q.astype(jnp.float32), mk.astype(jnp.float32)) / math.sqrt(MEM_HEAD_DIM)
    p = jax.nn.softmax(s, axis=-1)
    o = jnp.einsum('bhqk,bkhd->bqhd', p, mv.astype(jnp.float32)).reshape(b, t, D_MODEL)
    return o.astype(h.dtype) @ w_mo


def _conv_ffn(h, prev, w_gate, w_up, conv_w, conv_b, w_down):
    t = h.shape[1]
    gate_in = h @ w_gate
    buf = jnp.concatenate([prev.astype(gate_in.dtype), gate_in], axis=1)
    conv = conv_b
    for j in range(CONV_W):
        conv = conv + conv_w[j] * buf[:, j:j + t]
    out = (jax.nn.gelu(conv) * (h @ w_up)) @ w_down
    return out, buf[:, -(CONV_W - 1):]


def _layer(x, k_past, v_past, past_len, h0, conv_prev, mem_k, mem_v, lp):
    b, t, _ = x.shape
    h = _rmsnorm(x, lp['g_mix'])
    proj = h @ lp['w_in']
    q, k, v, u = jnp.split(proj, [SB_WIDTH, 2 * SB_WIDTH, 3 * SB_WIDTH], axis=-1)
    q = q.reshape(b, t, SB_HEADS, SB_HEAD_DIM)
    k = k.reshape(b, t, SB_HEADS, SB_HEAD_DIM)
    v = v.reshape(b, t, SB_HEADS, SB_HEAD_DIM)
    q_pos = past_len + jnp.arange(t, dtype=jnp.int32)
    if k_past is None:
        k_all, v_all, k_pos = k, v, q_pos
    else:
        k_all = jnp.concatenate([k_past.astype(k.dtype), k], axis=1)
        v_all = jnp.concatenate([v_past.astype(v.dtype), v], axis=1)
        k_pos = jnp.arange(past_len + t, dtype=jnp.int32)
    o_sb = _stick_breaking(q, k_all, v_all, q_pos, k_pos, lp['sb_bias']).reshape(b, t, SB_WIDTH)
    o_ssm, h_last = _s5(u, h0, lp['lam_re'], lp['lam_im'], lp['log_dt'], lp['b_re'], lp['b_im'],
                        lp['c_re'], lp['c_im'], lp['d_skip'], lp['w_glu'], lp['b_glu'])
    mix = jnp.concatenate([_rmsnorm(o_sb, lp['g_sb_out']), _rmsnorm(o_ssm, lp['g_ssm_out'])], axis=-1)
    x = x + mix @ lp['w_out']
    x = x + _mem_attend(_rmsnorm(x, lp['g_mem_q']), mem_k, mem_v, lp['w_mq'], lp['w_mo'])
    f, conv_state = _conv_ffn(_rmsnorm(x, lp['g_ffn']), conv_prev, lp['w_gate'], lp['w_up'],
                              lp['conv_w'], lp['conv_b'], lp['w_down'])
    x = x + f
    return x, k, v, h_last, conv_state


def setup_inputs(seed: int = 0) -> dict:
    key = jax.random.key(seed)
    ks = iter(jax.random.split(key, 48))
    f32 = jnp.float32

    def nrm(shape, scale):
        return jax.random.normal(next(ks), shape, f32) * scale

    def gain(shape):
        return 1.0 + nrm(shape, 0.02)

    n_pages = PAST_LEN // PAGE_SIZE
    n_pool = (DEC_BATCH * n_pages * 5) // 4
    L = DEPTH
    inp = {}
    inp['x_prompt'] = nrm((BATCH, SEQ, D_MODEL), 1.0)
    inp['x_sample'] = nrm((DEC_BATCH, DEC_SEQ, D_MODEL), 1.0)
    inp['cache_sb_k'] = nrm((L, n_pool, PAGE_SIZE, SB_HEADS, SB_HEAD_DIM), 1.0)
    inp['cache_sb_v'] = nrm((L, n_pool, PAGE_SIZE, SB_HEADS, SB_HEAD_DIM), 1.0)
    perm = jax.random.permutation(next(ks), n_pool)[:DEC_BATCH * n_pages]
    inp['page_table'] = perm.reshape(DEC_BATCH, n_pages).astype(jnp.int32)
    inp['state_ssm_re'] = nrm((L, DEC_BATCH, SSM_GROUPS, SSM_STATE), 0.5)
    inp['state_ssm_im'] = nrm((L, DEC_BATCH, SSM_GROUPS, SSM_STATE), 0.5)
    inp['state_conv'] = nrm((L, DEC_BATCH, CONV_W - 1, D_FF), 1.0)
    inp['cache_mem_k'] = nrm((L, DEC_BATCH, MEM_TOKENS, MEM_HEADS, MEM_HEAD_DIM), 1.0)
    inp['cache_mem_v'] = nrm((L, DEC_BATCH, MEM_TOKENS, MEM_HEADS, MEM_HEAD_DIM), 1.0)
    inp['mem_prompt'] = nrm((BATCH, MEM_TOKENS, D_MODEL), 1.0)
    inp['g_mix'] = gain((L, D_MODEL))
    inp['w_in'] = nrm((L, D_MODEL, IN_WIDTH), D_MODEL ** -0.5)
    inp['sb_bias'] = jax.random.uniform(next(ks), (L, SB_HEADS), f32, SB_BIAS_LO, SB_BIAS_HI)
    inp['lam_re'] = -0.5 + nrm((L, SSM_GROUPS, SSM_STATE), 0.01)
    inp['lam_im'] = math.pi * jnp.arange(SSM_STATE, dtype=f32) + nrm((L, SSM_GROUPS, SSM_STATE), 0.01)
    inp['log_dt'] = jax.random.uniform(next(ks), (L, SSM_GROUPS), f32, math.log(DT_MIN), math.log(DT_MAX))
    inp['b_re'] = nrm((L, SSM_GROUPS, SSM_STATE, SSM_GROUP), (2 * SSM_GROUP) ** -0.5)
    inp['b_im'] = nrm((L, SSM_GROUPS, SSM_STATE, SSM_GROUP), (2 * SSM_GROUP) ** -0.5)
    inp['c_re'] = nrm((L, SSM_GROUPS, SSM_GROUP, SSM_STATE), (2 * SSM_STATE) ** -0.5)
    inp['c_im'] = nrm((L, SSM_GROUPS, SSM_GROUP, SSM_STATE), (2 * SSM_STATE) ** -0.5)
    inp['d_skip'] = nrm((L, SSM_WIDTH), 1.0)
    inp['w_glu'] = nrm((L, SSM_WIDTH, SSM_WIDTH), SSM_WIDTH ** -0.5)
    inp['b_glu'] = nrm((L, SSM_WIDTH), 0.01)
    inp['g_sb_out'] = gain((L, SB_WIDTH))
    inp['g_ssm_out'] = gain((L, SSM_WIDTH))
    inp['w_out'] = nrm((L, MIX_WIDTH, D_MODEL), MIX_WIDTH ** -0.5)
    inp['g_mem_q'] = gain((L, D_MODEL))
    inp['g_mem_kv'] = gain((L, D_MODEL))
    inp['w_mq'] = nrm((L, D_MODEL, D_MODEL), D_MODEL ** -0.5)
    inp['w_mk'] = nrm((L, D_MODEL, D_MODEL), D_MODEL ** -0.5)
    inp['w_mv'] = nrm((L, D_MODEL, D_MODEL), D_MODEL ** -0.5)
    inp['w_mo'] = nrm((L, D_MODEL, D_MODEL), D_MODEL ** -0.5)
    inp['g_ffn'] = gain((L, D_MODEL))
    inp['w_gate'] = nrm((L, D_MODEL, D_FF), D_MODEL ** -0.5)
    inp['w_up'] = nrm((L, D_MODEL, D_FF), D_MODEL ** -0.5)
    inp['conv_w'] = nrm((L, CONV_W, D_FF), CONV_W ** -0.5)
    inp['conv_b'] = nrm((L, D_FF), 0.01)
    inp['w_down'] = nrm((L, D_FF, D_MODEL), D_FF ** -0.5)
    inp['g_final'] = gain((D_MODEL,))
    return inp


def reference(x_prompt, x_sample, cache_sb_k, cache_sb_v, page_table, state_ssm_re, state_ssm_im,
              state_conv, cache_mem_k, cache_mem_v, mem_prompt,
              g_mix, w_in, sb_bias, lam_re, lam_im, log_dt, b_re, b_im, c_re, c_im, d_skip, w_glu, b_glu,
              g_sb_out, g_ssm_out, w_out, g_mem_q, g_mem_kv, w_mq, w_mk, w_mv, w_mo,
              g_ffn, w_gate, w_up, conv_w, conv_b, w_down, g_final):
    n_p = x_prompt.shape[0]
    n_s = x_sample.shape[0]
    past_len = page_table.shape[1] * cache_sb_k.shape[2]
    yp, ys = x_prompt, x_sample
    pk, pv, pre, pim, pconv, pmk, pmv = [], [], [], [], [], [], []
    sk, sv, sre, sim, sconv = [], [], [], [], []
    for l in range(DEPTH):
        lp = dict(g_mix=g_mix[l], w_in=w_in[l], sb_bias=sb_bias[l], lam_re=lam_re[l], lam_im=lam_im[l],
                  log_dt=log_dt[l], b_re=b_re[l], b_im=b_im[l], c_re=c_re[l], c_im=c_im[l], d_skip=d_skip[l],
                  w_glu=w_glu[l], b_glu=b_glu[l], g_sb_out=g_sb_out[l], g_ssm_out=g_ssm_out[l],
                  w_out=w_out[l], g_mem_q=g_mem_q[l], w_mq=w_mq[l], w_mo=w_mo[l], g_ffn=g_ffn[l],
                  w_gate=w_gate[l], w_up=w_up[l], conv_w=conv_w[l], conv_b=conv_b[l], w_down=w_down[l])
        mk_p, mv_p = _mem_kv(mem_prompt, g_mem_kv[l], w_mk[l], w_mv[l])
        h0_p = jnp.zeros((n_p, SSM_GROUPS, SSM_STATE), jnp.complex64)
        conv0 = jnp.zeros((n_p, CONV_W - 1, D_FF), yp.dtype)
        yp, kp, vp, hp, cp = _layer(yp, None, None, 0, h0_p, conv0, mk_p, mv_p, lp)
        pk.append(kp); pv.append(vp); pconv.append(cp); pmk.append(mk_p); pmv.append(mv_p)
        pre.append(hp.real); pim.append(hp.imag)
        k_past = cache_sb_k[l][page_table].reshape(n_s, past_len, SB_HEADS, SB_HEAD_DIM)
        v_past = cache_sb_v[l][page_table].reshape(n_s, past_len, SB_HEADS, SB_HEAD_DIM)
        h0_s = lax.complex(state_ssm_re[l].astype(jnp.float32), state_ssm_im[l].astype(jnp.float32))
        ys, ks_, vs_, hs_, cs_ = _layer(ys, k_past, v_past, past_len, h0_s, state_conv[l],
                                        cache_mem_k[l], cache_mem_v[l], lp)
        sk.append(ks_); sv.append(vs_); sconv.append(cs_)
        sre.append(hs_.real); sim.append(hs_.imag)
    y_prompt = _rmsnorm(yp, g_final)
    y_sample = _rmsnorm(ys, g_final)
    return (y_prompt, y_sample,
            jnp.stack(pk), jnp.stack(pv), jnp.stack(pre), jnp.stack(pim), jnp.stack(pconv),
            jnp.stack(pmk), jnp.stack(pmv),
            jnp.stack(sk), jnp.stack(sv), jnp.stack(sre), jnp.stack(sim), jnp.stack(sconv))
```

```python
import functools
import math

import numpy as np
import jax
import jax.numpy as jnp
from jax import lax
from jax.experimental import pallas as pl
from jax.experimental.pallas import tpu as pltpu

F32 = jnp.float32
BF16 = jnp.bfloat16

EPS = 1e-6
SB_HEADS = 8
SB_HEAD_DIM = 64
SB_WIDTH = SB_HEADS * SB_HEAD_DIM
SSM_GROUPS = 32
SSM_GROUP = 16
SSM_STATE = 64
SSM_WIDTH = SSM_GROUPS * SSM_GROUP
SSM_CH = SSM_GROUPS * SSM_STATE
MEM_HEADS = 4
MEM_HEAD_DIM = 256
LANES = 128
N_STRIPS = SSM_CH // LANES
SCAN_ROWS = 128
N_POW = 7
VMEM_LIMIT = 48 * 1024 * 1024

_NT = (((1,), (1,)), ((), ()))


def _cparams(n_axes):
    return pltpu.CompilerParams(
        dimension_semantics=("arbitrary",) * n_axes, vmem_limit_bytes=VMEM_LIMIT)


def _rms(x, g):
    ms = jnp.mean(x * x, axis=-1, keepdims=True)
    return x * lax.rsqrt(ms + EPS) * g


def _gelu(x):
    c = math.sqrt(2.0 / math.pi)
    return x * (0.5 * (1.0 + jnp.tanh(c * (x + 0.044715 * (x * x * x)))))


def _dot(a, b):
    return jnp.dot(a, b, preferred_element_type=F32)


def _full(shape):
    n = len(shape)
    return pl.BlockSpec(shape, lambda *_: (0,) * n)


def _norm_proj_kernel(x_ref, g_ref, *refs, out_kinds):
    n_w = len(out_kinds)
    w_refs, out_refs = refs[:n_w], refs[n_w:]
    h = _rms(x_ref[...], g_ref[...]).astype(BF16)
    oi = 0
    for w_ref, (want_f32, want_bf16, scale) in zip(w_refs, out_kinds):
        r = _dot(h, w_ref[...])
        if want_f32:
            out_refs[oi][...] = r
            oi += 1
        if want_bf16:
            out_refs[oi][...] = (r * scale).astype(BF16)
            oi += 1


def _norm_proj(x, g, ws, out_kinds, tm):
    m, d = x.shape
    out_shapes, out_specs = [], []
    for w, (want_f32, want_bf16, _) in zip(ws, out_kinds):
        n = w.shape[1]
        for want, dt in ((want_f32, F32), (want_bf16, BF16)):
            if want:
                out_shapes.append(jax.ShapeDtypeStruct((m, n), dt))
                out_specs.append(pl.BlockSpec((tm, n), lambda i: (i, 0)))
    return pl.pallas_call(
        functools.partial(_norm_proj_kernel, out_kinds=tuple(out_kinds)),
        grid=(m // tm,),
        in_specs=[pl.BlockSpec((tm, d), lambda i: (i, 0)), _full(g.shape)]
        + [_full(w.shape) for w in ws],
        out_specs=out_specs,
        out_shape=out_shapes,
        compiler_params=_cparams(1),
        name="norm_proj",
    )(x, g, *ws)


def _neg_softplus_and_logsig(z):
    t = jnp.log1p(jnp.exp(-jnp.abs(z)))
    return jnp.minimum(-z, 0.0) - t, jnp.minimum(z, 0.0) - t


def _split_bf16(x):
    hi = x.astype(BF16)
    lo = (x - hi.astype(F32)).astype(BF16)
    return hi, lo


def _sb_prompt_kernel(bias_ref, q_ref, k_ref, v_ref, o_ref, acc_ref, r_ref, *, tq):
    hp = pl.program_id(1)
    i = pl.program_id(2)
    q = q_ref[0].astype(F32)
    lane = lax.broadcasted_iota(jnp.int32, (tq, LANES), 1)
    left = lane < SB_HEAD_DIM
    qs = (jnp.where(left, q, 0.0).astype(BF16), jnp.where(left, 0.0, q).astype(BF16))
    biases = (bias_ref[2 * hp], bias_ref[2 * hp + 1])
    rr = lax.broadcasted_iota(jnp.int32, (tq, tq), 0)
    cc = lax.broadcasted_iota(jnp.int32, (tq, tq), 1)
    later = jnp.where(rr > cc, 1.0, 0.0).astype(BF16)
    causal = cc < rr
    acc_ref[...] = jnp.zeros_like(acc_ref)
    r_ref[...] = jnp.zeros_like(r_ref)

    def block(j, masked):
        off = pl.multiple_of(j * tq, tq)
        kb = k_ref[0, pl.ds(off, tq), :]
        vb = v_ref[0, pl.ds(off, tq), :]
        pvs = []
        for h in range(2):
            z = lax.dot_general(qs[h], kb, _NT, preferred_element_type=F32) + biases[h]
            lnb, lsig = _neg_softplus_and_logsig(z)
            if masked:
                lnb = jnp.where(causal, lnb, 0.0)
            hi, lo = _split_bf16(lnb)
            cs = _dot(jnp.concatenate([hi, lo], axis=0), later)
            r = r_ref[h]
            after = cs[:tq] + cs[tq:] + jnp.concatenate([r] * (tq // LANES), axis=1)
            a = jnp.exp(lsig + after)
            if masked:
                a = jnp.where(causal, a, 0.0)
            pvs.append(_dot(a.astype(BF16), vb))
            r_ref[h] = r + jnp.sum(lnb, axis=-1, keepdims=True)
        acc_ref[...] += jnp.where(left, pvs[0], pvs[1])

    block(i, True)

    def body(jj, carry):
        block(i - jj, False)
        return carry

    lax.fori_loop(1, i + 1, body, 0)
    o_ref[0] = acc_ref[...]


def _sb_prompt(q, k, v, bias, tq=256):
    b, t, w = q.shape
    hp = w // LANES
    return pl.pallas_call(
        functools.partial(_sb_prompt_kernel, tq=tq),
        grid=(b, hp, t // tq),
        in_specs=[
            pl.BlockSpec(memory_space=pltpu.SMEM),
            pl.BlockSpec((1, tq, LANES), lambda bi, h, i: (bi, i, h)),
            pl.BlockSpec((1, t, LANES), lambda bi, h, i: (bi, 0, h)),
            pl.BlockSpec((1, t, LANES), lambda bi, h, i: (bi, 0, h)),
        ],
        out_specs=pl.BlockSpec((1, tq, LANES), lambda bi, h, i: (bi, i, h)),
        out_shape=jax.ShapeDtypeStruct((b, t, w), F32),
        scratch_shapes=[pltpu.VMEM((tq, LANES), F32), pltpu.VMEM((2, tq, LANES), F32)],
        compiler_params=_cparams(3),
        name="sb_prompt",
    )(bias, q, k, v)


def _sb_decode_kernel(pt_ref, q_ref, bias_ref, kc_ref, vc_ref, o_ref, kbuf, vbuf, sem, *,
                      n_pages, page, n_seq):
    b = pl.program_id(0)
    past = n_pages * page

    def copies(seq, slot):
        out = []
        for p in range(n_pages):
            pg = pt_ref[seq * n_pages + p]
            dst = pl.ds(p * page, page)
            out.append(pltpu.make_async_copy(kc_ref.at[pg], kbuf.at[slot, dst], sem.at[0, slot]))
            out.append(pltpu.make_async_copy(vc_ref.at[pg], vbuf.at[slot, dst], sem.at[1, slot]))
        return out

    @pl.when(b == 0)
    def _():
        for c in copies(0, 0):
            c.start()

    @pl.when(b + 1 < n_seq)
    def _():
        for c in copies(b + 1, (b + 1) % 2):
            c.start()

    slot = b % 2
    for c in copies(b, slot):
        c.wait()

    w = SB_WIDTH
    row = lax.broadcasted_iota(jnp.int32, (SB_HEADS, w), 0)
    lane = lax.broadcasted_iota(jnp.int32, (SB_HEADS, w), 1)
    own = (lane // SB_HEAD_DIM) == row
    q = jnp.broadcast_to(q_ref[0].astype(F32), (SB_HEADS, w))
    qbd = jnp.where(own, q, 0.0).astype(BF16)
    kb = kbuf[slot].astype(BF16)
    z = lax.dot_general(qbd, kb, _NT, preferred_element_type=F32) + bias_ref[...]
    lnb, lsig = _neg_softplus_and_logsig(z)

    cw = 256
    nblk = past // cw
    st = jnp.concatenate([lnb[:, j * cw:(j + 1) * cw] for j in range(nblk)], axis=0)
    rr = lax.broadcasted_iota(jnp.int32, (cw, cw), 0)
    cc = lax.broadcasted_iota(jnp.int32, (cw, cw), 1)
    later = jnp.where(rr > cc, 1.0, 0.0).astype(BF16)
    hi, lo = _split_bf16(st)
    cs = _dot(jnp.concatenate([hi, lo], axis=0), later)
    n8 = nblk * SB_HEADS
    within = cs[:n8] + cs[n8:]
    tot = jnp.sum(st, axis=-1, keepdims=True)
    run = jnp.zeros((SB_HEADS, 1), F32)
    parts = [None] * nblk
    for j in reversed(range(nblk)):
        sl = slice(j * SB_HEADS, (j + 1) * SB_HEADS)
        parts[j] = jnp.exp(lsig[:, j * cw:(j + 1) * cw] + within[sl] + run)
        run = run + tot[sl]
    a = jnp.concatenate(parts, axis=1).astype(BF16)
    vb = vbuf[slot].astype(BF16)
    res = _dot(a, vb)
    o_ref[0] = jnp.sum(jnp.where(own, res, 0.0), axis=0, keepdims=True)


def _sb_decode(q, bias_col, cache_k, cache_v, page_table):
    n_seq, n_pages = page_table.shape
    n_pool, page = cache_k.shape[0], cache_k.shape[1]
    w = SB_WIDTH
    kc = cache_k.reshape(n_pool, page, w)
    vc = cache_v.reshape(n_pool, page, w)
    past = n_pages * page
    grid_spec = pltpu.PrefetchScalarGridSpec(
        num_scalar_prefetch=1,
        grid=(n_seq,),
        in_specs=[
            pl.BlockSpec((1, 1, w), lambda b, pt: (b, 0, 0)),
            pl.BlockSpec((SB_HEADS, 1), lambda b, pt: (0, 0)),
            pl.BlockSpec(memory_space=pl.ANY),
            pl.BlockSpec(memory_space=pl.ANY),
        ],
        out_specs=pl.BlockSpec((1, 1, w), lambda b, pt: (b, 0, 0)),
        scratch_shapes=[
            pltpu.VMEM((2, past, w), F32),
            pltpu.VMEM((2, past, w), F32),
            pltpu.SemaphoreType.DMA((2, 2)),
        ],
    )
    out = pl.pallas_call(
        functools.partial(_sb_decode_kernel, n_pages=n_pages, page=page, n_seq=n_seq),
        grid_spec=grid_spec,
        out_shape=jax.ShapeDtypeStruct((n_seq, 1, w), F32),
        compiler_params=_cparams(1),
        name="sb_decode",
    )(page_table.reshape(-1), q.reshape(n_seq, 1, w), bias_col, kc, vc)
    return out.reshape(n_seq, w)


def _s5_prep_kernel(lre_ref, lim_ref, ldt_ref, btr_ref, bti_ref, pre_ref, pim_ref, bbr_ref, bbi_ref):
    lre, lim = lre_ref[...], lim_ref[...]
    dt = jnp.exp(ldt_ref[...])
    mag = jnp.exp(lre * dt)
    br = mag * jnp.cos(lim * dt)
    bi = mag * jnp.sin(lim * dt)
    den = lre * lre + lim * lim
    nr, ni = br - 1.0, bi
    cr = (nr * lre + ni * lim) / den
    ci = (ni * lre - nr * lim) / den
    bbr_ref[...] = cr * btr_ref[...] - ci * bti_ref[...]
    bbi_ref[...] = cr * bti_ref[...] + ci * btr_ref[...]
    pr, pi_ = br, bi
    for k in range(8):
        pre_ref[k:k + 1, :] = pr
        pim_ref[k:k + 1, :] = pi_
        pr, pi_ = pr * pr - pi_ * pi_, 2.0 * pr * pi_


def _s5_prepare(lam_re, lam_im, log_dt, b_re, b_im, c_re, c_im):
    ch = SSM_CH
    lre = lam_re.reshape(1, ch)
    lim = lam_im.reshape(1, ch)
    ldt = jnp.repeat(log_dt, SSM_STATE).reshape(1, ch)
    btr = b_re.reshape(ch, SSM_GROUP).T
    bti = b_im.reshape(ch, SSM_GROUP).T
    pre, pim, bbr, bbi = pl.pallas_call(
        _s5_prep_kernel,
        out_shape=[jax.ShapeDtypeStruct((8, ch), F32)] * 2
        + [jax.ShapeDtypeStruct((SSM_GROUP, ch), F32)] * 2,
        name="s5_prep",
    )(lre, lim, ldt, btr, bti)

    s_idx = np.arange(N_STRIPS)[:, None, None]
    j_idx = np.arange(LANES)[None, :, None]
    c_idx = np.arange(LANES)[None, None, :]
    grp_of_ch = (LANES * (s_idx // 4) + j_idx) // SSM_GROUP
    grp_of_state = (LANES * s_idx + c_idx) // SSM_STATE
    mask = jnp.asarray(grp_of_ch == grp_of_state, F32)

    def b_strips(bb):
        t = bb.reshape(SSM_GROUP, N_STRIPS, LANES).transpose(1, 0, 2)
        return jnp.tile(t, (1, LANES // SSM_GROUP, 1)) * mask

    wb = jnp.concatenate([b_strips(bbr), b_strips(bbi)], axis=2).astype(BF16)

    def c_strips(c):
        t = c.transpose(0, 2, 1).reshape(N_STRIPS, LANES, SSM_GROUP)
        return jnp.tile(t, (1, 1, LANES // SSM_GROUP)) * mask.transpose(0, 2, 1)

    wc = jnp.concatenate([c_strips(c_re), -c_strips(c_im)], axis=1).astype(BF16)

    def pw(p):
        return p.reshape(8, N_STRIPS, LANES).transpose(1, 0, 2)

    lamp = jnp.concatenate([pw(pre), pw(pim)], axis=2)
    return wb, wc, lamp


def _scan_rows(xr, xi, lam, row):
    for k in range(N_POW):
        s = 1 << k
        ar, ai = lam[k:k + 1, :LANES], lam[k:k + 1, LANES:]
        if s < 8:
            keep = row >= s
            sr = jnp.where(keep, pltpu.roll(xr, s, 0), 0.0)
            si = jnp.where(keep, pltpu.roll(xi, s, 0), 0.0)
            xr, xi = xr + ar * sr - ai * si, xi + ar * si + ai * sr
        else:
            pr, pi_ = xr[:-s], xi[:-s]
            nr = xr[s:] + ar * pr - ai * pi_
            ni = xi[s:] + ar * pi_ + ai * pr
            xr = jnp.concatenate([xr[:s], nr], axis=0)
            xi = jnp.concatenate([xi[:s], ni], axis=0)
    return xr, xi


def _glu_out(y, wglu_ref, bglu_ref):
    y = _gelu(y)
    return y * jax.nn.sigmoid(_dot(y.astype(BF16), wglu_ref[...]) + bglu_ref[...])


def _s5_prompt_kernel(u_ref, wb_ref, wc_ref, lamp_ref, dskip_ref, wglu_ref, bglu_ref,
                      o_ref, hre_ref, him_ref, carry_ref, *, chunk):
    c = pl.program_id(1)

    @pl.when(c == 0)
    def _():
        carry_ref[...] = jnp.zeros_like(carry_ref)

    u = u_ref[0]
    ub = u.astype(BF16)
    row = lax.broadcasted_iota(jnp.int32, (SCAN_ROWS, LANES), 0)
    first = row == 0
    y_blocks = []
    for kb in range(SSM_WIDTH // LANES):
        acc = None
        for s in range(4 * kb, 4 * kb + 4):
            res = _dot(ub[:, kb * LANES:(kb + 1) * LANES], wb_ref[s])
            lam = lamp_ref[s]
            prev = carry_ref[s]
            cr, ci = prev[7:8, :LANES], prev[7:8, LANES:]
            lr, li = lam[0:1, :LANES], lam[0:1, LANES:]
            parts = []
            for h in range(chunk // SCAN_ROWS):
                rs = slice(h * SCAN_ROWS, (h + 1) * SCAN_ROWS)
                xr = res[rs, :LANES] + jnp.where(first, lr * cr - li * ci, 0.0)
                xi = res[rs, LANES:] + jnp.where(first, lr * ci + li * cr, 0.0)
                xr, xi = _scan_rows(xr, xi, lam, row)
                cr, ci = xr[SCAN_ROWS - 1:], xi[SCAN_ROWS - 1:]
                parts.append(jnp.concatenate([xr, xi], axis=1).astype(BF16))
            carry_ref[s] = jnp.concatenate([xr[SCAN_ROWS - 8:], xi[SCAN_ROWS - 8:]], axis=1)
            d = _dot(jnp.concatenate(parts, axis=0), wc_ref[s])
            acc = d if acc is None else acc + d
        y_blocks.append(acc)
    y = jnp.concatenate(y_blocks, axis=1) + dskip_ref[...] * u
    o_ref[0] = _glu_out(y, wglu_ref, bglu_ref)

    @pl.when(c == pl.num_programs(1) - 1)
    def _():
        for s in range(N_STRIPS):
            last = carry_ref[s]
            hre_ref[0, :, s * LANES:(s + 1) * LANES] = last[7:8, :LANES]
            him_ref[0, :, s * LANES:(s + 1) * LANES] = last[7:8, LANES:]


def _s5_prompt(u, wb, wc, lamp, dskip, wglu, bglu, chunk=256):
    b, t, w = u.shape
    ch = SSM_CH
    return pl.pallas_call(
        functools.partial(_s5_prompt_kernel, chunk=chunk),
        grid=(b, t // chunk),
        in_specs=[pl.BlockSpec((1, chunk, w), lambda bi, c: (bi, c, 0))]
        + [_full(a.shape) for a in (wb, wc, lamp, dskip, wglu, bglu)],
        out_specs=[
            pl.BlockSpec((1, chunk, w), lambda bi, c: (bi, c, 0)),
            pl.BlockSpec((1, 1, ch), lambda bi, c: (bi, 0, 0)),
            pl.BlockSpec((1, 1, ch), lambda bi, c: (bi, 0, 0)),
        ],
        out_shape=[
            jax.ShapeDtypeStruct((b, t, w), F32),
            jax.ShapeDtypeStruct((b, 1, ch), F32),
            jax.ShapeDtypeStruct((b, 1, ch), F32),
        ],
        scratch_shapes=[pltpu.VMEM((N_STRIPS, 8, 2 * LANES), F32)],
        compiler_params=_cparams(2),
        name="s5_prompt",
    )(u, wb, wc, lamp, dskip, wglu, bglu)


def _s5_step_kernel(u_ref, h0r_ref, h0i_ref, wb_ref, wc_ref, lamp_ref, dskip_ref, wglu_ref,
                    bglu_ref, o_ref, hre_ref, him_ref):
    u = u_ref[...]
    ub = u.astype(BF16)
    y_blocks = []
    for kb in range(SSM_WIDTH // LANES):
        acc = None
        for s in range(4 * kb, 4 * kb + 4):
            sl = slice(s * LANES, (s + 1) * LANES)
            res = _dot(ub[:, kb * LANES:(kb + 1) * LANES], wb_ref[s])
            lam = lamp_ref[s]
            lr, li = lam[0:1, :LANES], lam[0:1, LANES:]
            h0r, h0i = h0r_ref[:, sl], h0i_ref[:, sl]
            xr = res[:, :LANES] + lr * h0r - li * h0i
            xi = res[:, LANES:] + lr * h0i + li * h0r
            hre_ref[:, sl] = xr
            him_ref[:, sl] = xi
            d = _dot(jnp.concatenate([xr, xi], axis=1).astype(BF16), wc_ref[s])
            acc = d if acc is None else acc + d
        y_blocks.append(acc)
    y = jnp.concatenate(y_blocks, axis=1) + dskip_ref[...] * u
    o_ref[...] = _glu_out(y, wglu_ref, bglu_ref)


def _s5_step(u, h0r, h0i, wb, wc, lamp, dskip, wglu, bglu):
    n, w = u.shape
    return pl.pallas_call(
        _s5_step_kernel,
        out_shape=[
            jax.ShapeDtypeStruct((n, w), F32),
            jax.ShapeDtypeStruct((n, SSM_CH), F32),
            jax.ShapeDtypeStruct((n, SSM_CH), F32),
        ],
        compiler_params=pltpu.CompilerParams(vmem_limit_bytes=VMEM_LIMIT),
        name="s5_step",
    )(u, h0r, h0i, wb, wc, lamp, dskip, wglu, bglu)


def _mix_out_kernel(x_ref, a_ref, b_ref, ga_ref, gb_ref, wa_ref, wb_ref, o_ref):
    ha = _rms(a_ref[...], ga_ref[...]).astype(BF16)
    hb = _rms(b_ref[...], gb_ref[...]).astype(BF16)
    o_ref[...] = x_ref[...] + _dot(ha, wa_ref[...]) + _dot(hb, wb_ref[...])


def _mix_out(x, a, b, ga, gb, wa, wb, tm):
    m, d = x.shape
    w = a.shape[1]
    return pl.pallas_call(
        _mix_out_kernel,
        grid=(m // tm,),
        in_specs=[
            pl.BlockSpec((tm, d), lambda i: (i, 0)),
            pl.BlockSpec((tm, w), lambda i: (i, 0)),
            pl.BlockSpec((tm, w), lambda i: (i, 0)),
        ] + [_full(t.shape) for t in (ga, gb, wa, wb)],
        out_specs=pl.BlockSpec((tm, d), lambda i: (i, 0)),
        out_shape=jax.ShapeDtypeStruct((m, d), F32),
        compiler_params=_cparams(1),
        name="mix_out",
    )(x, a, b, ga, gb, wa, wb)


def _mem_prompt_kernel(x_ref, g_ref, wq_ref, mk_ref, mv_ref, wo_ref, o_ref, ob_ref):
    x = x_ref[0]
    q = _dot(_rms(x, g_ref[...]).astype(BF16), wq_ref[...])
    qb = (q * (1.0 / math.sqrt(MEM_HEAD_DIM))).astype(BF16)
    for h in range(MEM_HEADS):
        sl = slice(h * MEM_HEAD_DIM, (h + 1) * MEM_HEAD_DIM)
        s = lax.dot_general(qb[:, sl], mk_ref[0, :, sl], _NT, preferred_element_type=F32)
        e = jnp.exp(s - jnp.max(s, axis=-1, keepdims=True))
        o = _dot(e.astype(BF16), mv_ref[0, :, sl]) / jnp.sum(e, axis=-1, keepdims=True)
        ob_ref[:, sl] = o.astype(BF16)
    o_ref[0] = x + _dot(ob_ref[...], wo_ref[...])


def _mem_prompt(x, g, wq, mk, mv, wo, tm):
    b, t, d = x.shape
    nk = mk.shape[1]
    return pl.pallas_call(
        _mem_prompt_kernel,
        grid=(b, t // tm),
        in_specs=[
            pl.BlockSpec((1, tm, d), lambda bi, i: (bi, i, 0)),
            _full(g.shape), _full(wq.shape),
            pl.BlockSpec((1, nk, d), lambda bi, i: (bi, 0, 0)),
            pl.BlockSpec((1, nk, d), lambda bi, i: (bi, 0, 0)),
            _full(wo.shape),
        ],
        out_specs=pl.BlockSpec((1, tm, d), lambda bi, i: (bi, i, 0)),
        out_shape=jax.ShapeDtypeStruct((b, t, d), F32),
        scratch_shapes=[pltpu.VMEM((tm, d), BF16)],
        compiler_params=_cparams(2),
        name="mem_prompt",
    )(x, g, wq, mk, mv, wo)


def _mem_decode_kernel(q_ref, k_ref, v_ref, o_ref):
    d = MEM_HEADS * MEM_HEAD_DIM
    row = lax.broadcasted_iota(jnp.int32, (8, d), 0)
    lane = lax.broadcasted_iota(jnp.int32, (8, d), 1)
    own = (lane // MEM_HEAD_DIM) == row
    q = jnp.broadcast_to(q_ref[0].astype(F32), (8, d))
    qbd = jnp.where(own, q, 0.0).astype(BF16)
    s = lax.dot_general(qbd, k_ref[0].astype(BF16), _NT, preferred_element_type=F32)
    e = jnp.exp(s - jnp.max(s, axis=-1, keepdims=True))
    p = e / jnp.sum(e, axis=-1, keepdims=True)
    res = _dot(p.astype(BF16), v_ref[0].astype(BF16))
    o_ref[0] = jnp.sum(jnp.where(own, res, 0.0), axis=0, keepdims=True).astype(BF16)


def _mem_decode(q, mem_k, mem_v):
    n, nk, d = mem_k.shape
    out = pl.pallas_call(
        _mem_decode_kernel,
        grid=(n,),
        in_specs=[
            pl.BlockSpec((1, 1, d), lambda b: (b, 0, 0)),
            pl.BlockSpec((1, nk, d), lambda b: (b, 0, 0)),
            pl.BlockSpec((1, nk, d), lambda b: (b, 0, 0)),
        ],
        out_specs=pl.BlockSpec((1, 1, d), lambda b: (b, 0, 0)),
        out_shape=jax.ShapeDtypeStruct((n, 1, d), BF16),
        compiler_params=_cparams(1),
        name="mem_decode",
    )(q.reshape(n, 1, d), mem_k, mem_v)
    return out.reshape(n, d)


def _proj_residual_kernel(x_ref, a_ref, w_ref, o_ref):
    o_ref[...] = x_ref[...] + _dot(a_ref[...], w_ref[...])


def _proj_residual(x, a, w):
    return pl.pallas_call(
        _proj_residual_kernel,
        out_shape=jax.ShapeDtypeStruct(x.shape, F32),
        compiler_params=pltpu.CompilerParams(vmem_limit_bytes=VMEM_LIMIT),
        name="proj_residual",
    )(x, a, w)


FF_CHUNK = 256


def _ffn_prompt_kernel(x_ref, g_ref, wg_ref, wu_ref, cw_ref, cb_ref, wd_ref, gf_ref,
                       y_ref, cs_ref, act_ref, carry_ref, *, tm):
    @pl.when(pl.program_id(1) == 0)
    def _():
        carry_ref[...] = jnp.zeros_like(carry_ref)

    x = x_ref[0]
    h = _rms(x, g_ref[...]).astype(BF16)
    row = lax.broadcasted_iota(jnp.int32, (tm, FF_CHUNK), 0)
    d_ff = wg_ref.shape[1]
    for c in range(d_ff // FF_CHUNK):
        sl = slice(c * FF_CHUNK, (c + 1) * FF_CHUNK)
        g = _dot(h, wg_ref[:, sl])
        up = _dot(h, wu_ref[:, sl])
        prev = carry_ref[:, sl]
        p1, p2 = prev[7:8], prev[6:7]
        g1 = jnp.where(row == 0, p1, pltpu.roll(g, 1, 0))
        g2 = jnp.where(row == 0, p2, jnp.where(row == 1, p1, pltpu.roll(g, 2, 0)))
        conv = cb_ref[:, sl] + cw_ref[0:1, sl] * g2 + cw_ref[1:2, sl] * g1 + cw_ref[2:3, sl] * g
        act_ref[:, sl] = (_gelu(conv) * up).astype(BF16)
        carry_ref[:, sl] = g[tm - 8:]
        cs_ref[0, :, sl] = g[tm - 2:]
    x3 = x + _dot(act_ref[...], wd_ref[...])
    y_ref[0] = _rms(x3, gf_ref[...])


def _ffn_prompt(x, g, wg, wu, cw, cb, wd, gf, tm):
    b, t, d = x.shape
    d_ff = wg.shape[1]
    return pl.pallas_call(
        functools.partial(_ffn_prompt_kernel, tm=tm),
        grid=(b, t // tm),
        in_specs=[pl.BlockSpec((1, tm, d), lambda bi, i: (bi, i, 0))]
        + [_full(a.shape) for a in (g, wg, wu, cw, cb, wd, gf)],
        out_specs=[
            pl.BlockSpec((1, tm, d), lambda bi, i: (bi, i, 0)),
            pl.BlockSpec((1, 2, d_ff), lambda bi, i: (bi, 0, 0)),
        ],
        out_shape=[
            jax.ShapeDtypeStruct((b, t, d), F32),
            jax.ShapeDtypeStruct((b, 2, d_ff), F32),
        ],
        scratch_shapes=[pltpu.VMEM((tm, d_ff), BF16), pltpu.VMEM((8, d_ff), F32)],
        compiler_params=_cparams(2),
        name="ffn_prompt",
    )(x, g, wg, wu, cw, cb, wd, gf)


def _ffn_step_kernel(x_ref, g_ref, wg_ref, wu_ref, cw_ref, cb_ref, wd_ref, gf_ref, p0_ref, p1_ref,
                     y_ref, gate_ref, act_ref):
    x = x_ref[...]
    h = _rms(x, g_ref[...]).astype(BF16)
    d_ff = wg_ref.shape[1]
    for c in range(d_ff // FF_CHUNK):
        sl = slice(c * FF_CHUNK, (c + 1) * FF_CHUNK)
        g = _dot(h, wg_ref[:, sl])
        up = _dot(h, wu_ref[:, sl])
        conv = (cb_ref[:, sl] + cw_ref[0:1, sl] * p0_ref[:, sl] + cw_ref[1:2, sl] * p1_ref[:, sl]
                + cw_ref[2:3, sl] * g)
        act_ref[:, sl] = (_gelu(conv) * up).astype(BF16)
        gate_ref[:, sl] = g
    x3 = x + _dot(act_ref[...], wd_ref[...])
    y_ref[...] = _rms(x3, gf_ref[...])


def _ffn_step(x, g, wg, wu, cw, cb, wd, gf, p0, p1):
    n, d = x.shape
    d_ff = wg.shape[1]
    return pl.pallas_call(
        _ffn_step_kernel,
        out_shape=[jax.ShapeDtypeStruct((n, d), F32), jax.ShapeDtypeStruct((n, d_ff), F32)],
        scratch_shapes=[pltpu.VMEM((n, d_ff), BF16)],
        compiler_params=pltpu.CompilerParams(vmem_limit_bytes=VMEM_LIMIT),
        name="ffn_step",
    )(x, g, wg, wu, cw, cb, wd, gf, p0, p1)


def kernel(x_prompt, x_sample, cache_sb_k, cache_sb_v, page_table, state_ssm_re, state_ssm_im, state_conv, cache_mem_k, cache_mem_v, mem_prompt, g_mix, w_in, sb_bias, lam_re, lam_im, log_dt, b_re, b_im, c_re, c_im, d_skip, w_glu, b_glu, g_sb_out, g_ssm_out, w_out, g_mem_q, g_mem_kv, w_mq, w_mk, w_mv, w_mo, g_ffn, w_gate, w_up, conv_w, conv_b, w_down, g_final):
    depth = w_in.shape[0]
    n_p, t_p, d = x_prompt.shape
    n_s = x_sample.shape[0]
    assert x_sample.shape[1] == 1
    tm = 512
    q_scale = 1.0 / math.sqrt(SB_HEAD_DIM)
    row = lambda a: a.reshape(1, -1)
    gf = row(g_final)

    yp = x_prompt.reshape(n_p * t_p, d)
    ys = x_sample.reshape(n_s, d)
    outs = {k: [] for k in ("pk", "pv", "pre", "pim", "pconv", "pmk", "pmv",
                            "sk", "sv", "sre", "sim", "sconv")}
    y_prompt = y_sample = None
    for l in range(depth):
        w_in_b = w_in[l].astype(BF16)
        w_q, w_k, w_v, w_u = (w_in_b[:, j * SB_WIDTH:(j + 1) * SB_WIDTH] for j in range(4))
        w_out_b = w_out[l].astype(BF16)
        wo_a, wo_b = w_out_b[:SB_WIDTH], w_out_b[SB_WIDTH:]
        wglu_b = w_glu[l].astype(BF16)
        wmq, wmk, wmv, wmo = (w[l].astype(BF16) for w in (w_mq, w_mk, w_mv, w_mo))
        wg, wu, wd = (w[l].astype(BF16) for w in (w_gate, w_up, w_down))
        wb, wc, lamp = _s5_prepare(lam_re[l], lam_im[l], log_dt[l], b_re[l], b_im[l], c_re[l], c_im[l])
        s5_w = (wb, wc, lamp, row(d_skip[l]), wglu_b, row(b_glu[l]))
        ffn_w = (row(g_ffn[l]), wg, wu, conv_w[l], row(conv_b[l]), wd, gf)
        in_kinds = [(False, True, q_scale), (True, True, 1.0), (True, True, 1.0), (True, False, 1.0)]

        q_b, k_f, k_b, v_f, v_b, u_f = _norm_proj(
            yp, row(g_mix[l]), [w_q, w_k, w_v, w_u], in_kinds, tm)
        sh = (n_p, t_p, SB_WIDTH)
        o_sb = _sb_prompt(q_b.reshape(sh), k_b.reshape(sh), v_b.reshape(sh), sb_bias[l])
        o_ssm, hre, him = _s5_prompt(u_f.reshape(sh), *s5_w)
        x1 = _mix_out(yp, o_sb.reshape(-1, SB_WIDTH), o_ssm.reshape(-1, SSM_WIDTH),
                      row(g_sb_out[l]), row(g_ssm_out[l]), wo_a, wo_b, tm)
        n_mem = mem_prompt.shape[1]
        mk_f, mk_b, mv_f, mv_b = _norm_proj(
            mem_prompt.reshape(n_p * n_mem, d), row(g_mem_kv[l]), [wmk, wmv],
            [(True, True, 1.0), (True, True, 1.0)], n_mem)
        x2 = _mem_prompt(x1.reshape(n_p, t_p, d), row(g_mem_q[l]), wmq,
                         mk_b.reshape(n_p, n_mem, d), mv_b.reshape(n_p, n_mem, d), wmo, tm)
        y3, cs_p = _ffn_prompt(x2, *ffn_w, tm)
        if l + 1 < depth:
            raise NotImplementedError("final norm is fused into the last layer's FFN")
        y_prompt = y3
        outs["pk"].append(k_f.reshape(n_p, t_p, SB_HEADS, SB_HEAD_DIM))
        outs["pv"].append(v_f.reshape(n_p, t_p, SB_HEADS, SB_HEAD_DIM))
        outs["pre"].append(hre.reshape(n_p, SSM_GROUPS, SSM_STATE))
        outs["pim"].append(him.reshape(n_p, SSM_GROUPS, SSM_STATE))
        outs["pconv"].append(cs_p)
        outs["pmk"].append(mk_f.reshape(n_p, n_mem, MEM_HEADS, MEM_HEAD_DIM))
        outs["pmv"].append(mv_f.reshape(n_p, n_mem, MEM_HEADS, MEM_HEAD_DIM))

        qs_b, ks_f, vs_f, us_f = _norm_proj(
            ys, row(g_mix[l]), [w_q, w_k, w_v, w_u],
            [(False, True, q_scale), (True, False, 1.0), (True, False, 1.0), (True, False, 1.0)], n_s)
        os_sb = _sb_decode(qs_b, sb_bias[l].reshape(SB_HEADS, 1), cache_sb_k[l], cache_sb_v[l],
                           page_table)
        os_ssm, hsr, hsi = _s5_step(us_f, state_ssm_re[l].reshape(n_s, SSM_CH),
                                    state_ssm_im[l].reshape(n_s, SSM_CH), *s5_w)
        x1s = _mix_out(ys, os_sb, os_ssm, row(g_sb_out[l]), row(g_ssm_out[l]), wo_a, wo_b, n_s)
        (qm_b,) = _norm_proj(x1s, row(g_mem_q[l]), [wmq],
                             [(False, True, 1.0 / math.sqrt(MEM_HEAD_DIM))], n_s)
        n_mk = cache_mem_k.shape[2]
        om = _mem_decode(qm_b, cache_mem_k[l].reshape(n_s, n_mk, d), cache_mem_v[l].reshape(n_s, n_mk, d))
        x2s = _proj_residual(x1s, om, wmo)
        y3s, gate_s = _ffn_step(x2s, *ffn_w, state_conv[l][:, 0], state_conv[l][:, 1])
        y_sample = y3s
        outs["sk"].append(ks_f.reshape(n_s, 1, SB_HEADS, SB_HEAD_DIM))
        outs["sv"].append(vs_f.reshape(n_s, 1, SB_HEADS, SB_HEAD_DIM))
        outs["sre"].append(hsr.reshape(n_s, SSM_GROUPS, SSM_STATE))
        outs["sim"].append(hsi.reshape(n_s, SSM_GROUPS, SSM_STATE))
        outs["sconv"].append(jnp.stack([state_conv[l][:, 1], gate_s], axis=1))

    st = lambda k: jnp.stack(outs[k])
    return (y_prompt, y_sample.reshape(n_s, 1, d),
            st("pk"), st("pv"), st("pre"), st("pim"), st("pconv"), st("pmk"), st("pmv"),
            st("sk"), st("sv"), st("sre"), st("sim"), st("sconv"))
```

```python
import functools
import math

import numpy as np
import jax
import jax.numpy as jnp
from jax import lax
from jax.experimental import pallas as pl
from jax.experimental.pallas import tpu as pltpu

F32 = jnp.float32
BF16 = jnp.bfloat16

EPS = 1e-6
SB_HEADS = 8
SB_HEAD_DIM = 64
SB_WIDTH = SB_HEADS * SB_HEAD_DIM
SSM_GROUPS = 32
SSM_GROUP = 16
SSM_STATE = 64
SSM_WIDTH = SSM_GROUPS * SSM_GROUP
SSM_CH = SSM_GROUPS * SSM_STATE
MEM_HEADS = 4
MEM_HEAD_DIM = 256
LANES = 128
N_STRIPS = SSM_CH // LANES
SCAN_ROWS = 128
N_POW = 7
VMEM_LIMIT = 48 * 1024 * 1024

_NT = (((1,), (1,)), ((), ()))


def _cparams(n_axes):
    return pltpu.CompilerParams(
        dimension_semantics=("arbitrary",) * n_axes, vmem_limit_bytes=VMEM_LIMIT)


def _rms(x, g):
    ms = jnp.mean(x * x, axis=-1, keepdims=True)
    return x * lax.rsqrt(ms + EPS) * g


def _gelu(x):
    c = math.sqrt(2.0 / math.pi)
    return x * (0.5 * (1.0 + jnp.tanh(c * (x + 0.044715 * (x * x * x)))))


def _dot(a, b):
    return jnp.dot(a, b, preferred_element_type=F32)


def _full(shape):
    n = len(shape)
    return pl.BlockSpec(shape, lambda *_: (0,) * n)


def _norm_proj_kernel(x_ref, g_ref, *refs, out_kinds):
    n_w = len(out_kinds)
    w_refs, out_refs = refs[:n_w], refs[n_w:]
    h = _rms(x_ref[...], g_ref[...]).astype(BF16)
    oi = 0
    for w_ref, (want_f32, want_bf16, scale) in zip(w_refs, out_kinds):
        r = _dot(h, w_ref[...])
        if scale != 1.0:
            r = r * scale
        if want_f32:
            out_refs[oi][...] = r
            oi += 1
        if want_bf16:
            out_refs[oi][...] = r.astype(BF16)
            oi += 1


def _norm_proj(x, g, ws, out_kinds, tm):
    m, d = x.shape
    out_shapes, out_specs = [], []
    for w, (want_f32, want_bf16, _) in zip(ws, out_kinds):
        n = w.shape[1]
        for want, dt in ((want_f32, F32), (want_bf16, BF16)):
            if want:
                out_shapes.append(jax.ShapeDtypeStruct((m, n), dt))
                out_specs.append(pl.BlockSpec((tm, n), lambda i: (i, 0)))
    return pl.pallas_call(
        functools.partial(_norm_proj_kernel, out_kinds=tuple(out_kinds)),
        grid=(m // tm,),
        in_specs=[pl.BlockSpec((tm, d), lambda i: (i, 0)), _full(g.shape)]
        + [_full(w.shape) for w in ws],
        out_specs=out_specs,
        out_shape=out_shapes,
        compiler_params=_cparams(1),
        name="norm_proj",
    )(x, g, *ws)


SB_SUB = 256
LOG2E = 1.4426950408889634


def _softplus_and_logsig(z):
    e = jnp.exp2(jnp.abs(z) * -LOG2E)
    sp = jnp.maximum(z, 0.0) + jnp.log(1.0 + e)
    return sp, z - sp


def _bf16_pieces(x, n):
    out = []
    for _ in range(n):
        p = x.astype(BF16).astype(F32)
        out.append(p)
        x = x - p
    return out


def _sb_prompt_kernel(bias_ref, q_ref, k_ref, v_ref, o_ref, acc_ref, r_ref, *, tq):
    i = pl.program_id(2)
    q = q_ref[0].astype(F32)
    lane = lax.broadcasted_iota(jnp.int32, (tq, LANES), 1)
    left = lane < SB_HEAD_DIM
    nsub = tq // SB_SUB
    bias_rows = jnp.concatenate([jnp.broadcast_to(bias_ref[0, 0:1, :], (tq, LANES)),
                                 jnp.broadcast_to(bias_ref[0, 1:2, :], (tq, LANES))], axis=0)
    qq = jnp.concatenate([jnp.where(left, q, 0.0), jnp.where(left, 0.0, q)], axis=0)
    qq = jnp.concatenate([qq, bias_rows], axis=1).astype(BF16)
    k_ones = jnp.ones((tq, LANES), BF16)
    rr = lax.broadcasted_iota(jnp.int32, (SB_SUB, SB_SUB), 0)
    cc = lax.broadcasted_iota(jnp.int32, (SB_SUB, SB_SUB), 1)
    later = jnp.where(rr > cc, 1.0, 0.0).astype(BF16)
    qrow = lax.broadcasted_iota(jnp.int32, (tq, SB_SUB), 0)
    kcol = lax.broadcasted_iota(jnp.int32, (tq, SB_SUB), 1)
    acc_ref[...] = jnp.zeros_like(acc_ref)
    r_ref[...] = jnp.zeros_like(r_ref)

    def both(fn, x):
        return jnp.concatenate([fn(x[:tq]), fn(x[tq:])], axis=0)

    def run(j, masked):
        off = pl.multiple_of(j * tq, tq)
        kb = jnp.concatenate([k_ref[0, pl.ds(off, tq), :], k_ones], axis=1)
        vb = v_ref[0, pl.ds(off, tq), :]
        z = lax.dot_general(qq, kb, _NT, preferred_element_type=F32)
        sp, lsig = _softplus_and_logsig(z)
        r = r_ref[...]
        parts = [None] * nsub
        for n in reversed(range(nsub)):
            sl = slice(n * SB_SUB, (n + 1) * SB_SUB)
            spn = sp[:, sl]
            if masked:
                causal = (kcol + n * SB_SUB) < qrow
                spn = both(lambda x: jnp.where(causal, x, 0.0), spn)
            cs = _dot(spn.astype(BF16), later)
            a = jnp.exp((lsig[:, sl] - jnp.concatenate([r] * (SB_SUB // LANES), axis=1)) - cs)
            if masked:
                a = both(lambda x: jnp.where(causal, x, 0.0), a)
            parts[n] = a.astype(BF16)
            r = r + (cs[:, 0:1] + spn[:, 0:1])
        r_ref[...] = r
        pv = _dot(jnp.concatenate(parts, axis=1), vb)
        acc_ref[...] += jnp.where(left, pv[:tq], pv[tq:])

    run(i, True)

    def body(jj, carry):
        run(i - jj, False)
        return carry

    lax.fori_loop(1, i + 1, body, 0)
    o_ref[0] = acc_ref[...]


def _sb_prompt(q, k, v, bias, tq=2 * SB_SUB):
    b, t, w = q.shape
    hp = w // LANES
    pieces = jnp.stack(_bf16_pieces(bias, 3), axis=-1)
    bias = jnp.pad(pieces, ((0, 0), (0, LANES - 3))).reshape(hp, 2, LANES)
    return pl.pallas_call(
        functools.partial(_sb_prompt_kernel, tq=tq),
        grid=(b, hp, t // tq),
        in_specs=[
            pl.BlockSpec((1, 2, LANES), lambda bi, h, i: (h, 0, 0)),
            pl.BlockSpec((1, tq, LANES), lambda bi, h, i: (bi, i, h)),
            pl.BlockSpec((1, t, LANES), lambda bi, h, i: (bi, 0, h)),
            pl.BlockSpec((1, t, LANES), lambda bi, h, i: (bi, 0, h)),
        ],
        out_specs=pl.BlockSpec((1, tq, LANES), lambda bi, h, i: (bi, i, h)),
        out_shape=jax.ShapeDtypeStruct((b, t, w), F32),
        scratch_shapes=[pltpu.VMEM((tq, LANES), F32), pltpu.VMEM((2 * tq, LANES), F32)],
        compiler_params=_cparams(3),
        name="sb_prompt",
    )(bias, q, k, v)


def _sb_decode_kernel(pt_ref, q_ref, bias_ref, kc_ref, vc_ref, o_ref, kbuf, vbuf, sem, *,
                      layer, n_pages, page, n_seq):
    b = pl.program_id(0)

    def copies(seq, slot):
        out = []
        for p in range(n_pages):
            pg = pt_ref[seq * n_pages + p]
            out.append(pltpu.make_async_copy(kc_ref.at[layer, pg], kbuf.at[slot, p], sem.at[0, slot]))
            out.append(pltpu.make_async_copy(vc_ref.at[layer, pg], vbuf.at[slot, p], sem.at[1, slot]))
        return out

    @pl.when(b == 0)
    def _():
        for c in copies(0, 0):
            c.start()

    @pl.when(b + 1 < n_seq)
    def _():
        for c in copies(b + 1, (b + 1) % 2):
            c.start()

    slot = b % 2
    for c in copies(b, slot):
        c.wait()

    nh = SB_HEADS
    w = page * nh
    row = lax.broadcasted_iota(jnp.int32, (nh, w), 0)
    lane = lax.broadcasted_iota(jnp.int32, (nh, w), 1)
    own = (lane % nh) == row
    q8 = q_ref[0]
    zrows = []
    for p in range(n_pages):
        k2 = kbuf[slot, p].reshape(w, SB_HEAD_DIM).astype(BF16)
        zf = lax.dot_general(q8, k2, _NT, preferred_element_type=F32)
        zrows.append(jnp.sum(jnp.where(own, zf, 0.0), axis=0, keepdims=True))
    z = jnp.concatenate(zrows, axis=0) + bias_ref[...]
    sp, lsig = _softplus_and_logsig(z)

    lane_p = lax.broadcasted_iota(jnp.int32, (n_pages, w), 1)
    row_p = lax.broadcasted_iota(jnp.int32, (n_pages, w), 0)
    incl = sp
    tot = sp
    step = nh
    while step < w:
        incl = incl + jnp.where(lane_p < w - step, pltpu.roll(incl, w - step, 1), 0.0)
        tot = tot + pltpu.roll(tot, step, 1)
        step *= 2
    pages_incl = tot
    step = 1
    while step < n_pages:
        pages_incl = pages_incl + jnp.where(
            row_p < n_pages - step, pltpu.roll(pages_incl, n_pages - step, 0), 0.0)
        step *= 2
    after = (incl - sp) + (pages_incl - tot)
    a = jnp.exp(lsig - after)

    acc = jnp.zeros((nh, SB_HEAD_DIM), F32)
    for p in range(n_pages):
        ap = jnp.where(own, jnp.broadcast_to(a[p:p + 1, :], (nh, w)), 0.0).astype(BF16)
        v2 = vbuf[slot, p].reshape(w, SB_HEAD_DIM).astype(BF16)
        acc = acc + _dot(ap, v2)
    o_ref[0] = acc


def _sb_decode(q, bias, cache_k, cache_v, page_table, layer):
    n_seq, n_pages = page_table.shape
    page = cache_k.shape[2]
    nh, dh = SB_HEADS, SB_HEAD_DIM
    bias_row = jnp.tile(bias, page).reshape(1, page * nh)
    grid_spec = pltpu.PrefetchScalarGridSpec(
        num_scalar_prefetch=1,
        grid=(n_seq,),
        in_specs=[
            pl.BlockSpec((1, nh, dh), lambda b, pt: (b, 0, 0)),
            pl.BlockSpec((1, page * nh), lambda b, pt: (0, 0)),
            pl.BlockSpec(memory_space=pl.ANY),
            pl.BlockSpec(memory_space=pl.ANY),
        ],
        out_specs=pl.BlockSpec((1, nh, dh), lambda b, pt: (b, 0, 0)),
        scratch_shapes=[
            pltpu.VMEM((2, n_pages, page, nh, dh), F32),
            pltpu.VMEM((2, n_pages, page, nh, dh), F32),
            pltpu.SemaphoreType.DMA((2, 2)),
        ],
    )
    return pl.pallas_call(
        functools.partial(_sb_decode_kernel, layer=layer, n_pages=n_pages, page=page, n_seq=n_seq),
        grid_spec=grid_spec,
        out_shape=jax.ShapeDtypeStruct((n_seq, nh, dh), F32),
        compiler_params=_cparams(1),
        name="sb_decode",
    )(page_table.reshape(-1), q, bias_row, cache_k, cache_v)


def _s5_prep_kernel(lre_ref, lim_ref, ldt_ref, btr_ref, bti_ref, pre_ref, pim_ref, bbr_ref, bbi_ref):
    lre, lim = lre_ref[...], lim_ref[...]
    dt = jnp.exp(ldt_ref[...])
    mag = jnp.exp(lre * dt)
    br = mag * jnp.cos(lim * dt)
    bi = mag * jnp.sin(lim * dt)
    den = lre * lre + lim * lim
    nr, ni = br - 1.0, bi
    cr = (nr * lre + ni * lim) / den
    ci = (ni * lre - nr * lim) / den
    bbr_ref[...] = cr * btr_ref[...] - ci * bti_ref[...]
    bbi_ref[...] = cr * bti_ref[...] + ci * btr_ref[...]
    pr, pi_ = br, bi
    for k in range(8):
        pre_ref[k:k + 1, :] = pr
        pim_ref[k:k + 1, :] = pi_
        pr, pi_ = pr * pr - pi_ * pi_, 2.0 * pr * pi_


def _s5_prepare(lam_re, lam_im, log_dt, b_re, b_im, c_re, c_im):
    ch = SSM_CH
    lre = lam_re.reshape(1, ch)
    lim = lam_im.reshape(1, ch)
    ldt = jnp.repeat(log_dt, SSM_STATE).reshape(1, ch)
    btr = b_re.reshape(ch, SSM_GROUP).T
    bti = b_im.reshape(ch, SSM_GROUP).T
    pre, pim, bbr, bbi = pl.pallas_call(
        _s5_prep_kernel,
        out_shape=[jax.ShapeDtypeStruct((8, ch), F32)] * 2
        + [jax.ShapeDtypeStruct((SSM_GROUP, ch), F32)] * 2,
        name="s5_prep",
    )(lre, lim, ldt, btr, bti)

    s_idx = np.arange(N_STRIPS)[:, None, None]
    j_idx = np.arange(LANES)[None, :, None]
    c_idx = np.arange(LANES)[None, None, :]
    grp_of_ch = (LANES * (s_idx // 4) + j_idx) // SSM_GROUP
    grp_of_state = (LANES * s_idx + c_idx) // SSM_STATE
    mask = jnp.asarray(grp_of_ch == grp_of_state, F32)

    def b_strips(bb):
        t = bb.reshape(SSM_GROUP, N_STRIPS, LANES).transpose(1, 0, 2)
        return jnp.tile(t, (1, LANES // SSM_GROUP, 1)) * mask

    wb = jnp.concatenate([b_strips(bbr), b_strips(bbi)], axis=2).astype(BF16)

    def c_strips(c):
        t = c.transpose(0, 2, 1).reshape(N_STRIPS, LANES, SSM_GROUP)
        return jnp.tile(t, (1, 1, LANES // SSM_GROUP)) * mask.transpose(0, 2, 1)

    wc = jnp.concatenate([c_strips(c_re), -c_strips(c_im)], axis=1).astype(BF16)

    def pw(p):
        return p.reshape(8, N_STRIPS, LANES).transpose(1, 0, 2)

    lamp = jnp.concatenate([pw(pre), pw(pim)], axis=2)
    return wb, wc, lamp


def _scan_rows(xr, xi, lam, row):
    for k in range(N_POW):
        s = 1 << k
        ar, ai = lam[k:k + 1, :LANES], lam[k:k + 1, LANES:]
        if s < 8:
            keep = row >= s
            sr = jnp.where(keep, pltpu.roll(xr, s, 0), 0.0)
            si = jnp.where(keep, pltpu.roll(xi, s, 0), 0.0)
            xr, xi = xr + ar * sr - ai * si, xi + ar * si + ai * sr
        else:
            pr, pi_ = xr[:-s], xi[:-s]
            nr = xr[s:] + ar * pr - ai * pi_
            ni = xi[s:] + ar * pi_ + ai * pr
            xr = jnp.concatenate([xr[:s], nr], axis=0)
            xi = jnp.concatenate([xi[:s], ni], axis=0)
    return xr, xi


def _glu_out(y, wglu_ref, bglu_ref):
    y = _gelu(y)
    return y * jax.nn.sigmoid(_dot(y.astype(BF16), wglu_ref[...]) + bglu_ref[...])


def _s5_prompt_kernel(u_ref, wb_ref, wc_ref, lamp_ref, dskip_ref, wglu_ref, bglu_ref,
                      o_ref, hre_ref, him_ref, carry_ref, *, chunk):
    c = pl.program_id(1)

    @pl.when(c == 0)
    def _():
        carry_ref[...] = jnp.zeros_like(carry_ref)

    u = u_ref[0]
    ub = u.astype(BF16)
    row = lax.broadcasted_iota(jnp.int32, (SCAN_ROWS, LANES), 0)
    first = row == 0
    y_blocks = []
    for kb in range(SSM_WIDTH // LANES):
        acc = None
        for s in range(4 * kb, 4 * kb + 4):
            res = _dot(ub[:, kb * LANES:(kb + 1) * LANES], wb_ref[s])
            lam = lamp_ref[s]
            prev = carry_ref[s]
            cr, ci = prev[7:8, :LANES], prev[7:8, LANES:]
            lr, li = lam[0:1, :LANES], lam[0:1, LANES:]
            parts = []
            for h in range(chunk // SCAN_ROWS):
                rs = slice(h * SCAN_ROWS, (h + 1) * SCAN_ROWS)
                xr = res[rs, :LANES] + jnp.where(first, lr * cr - li * ci, 0.0)
                xi = res[rs, LANES:] + jnp.where(first, lr * ci + li * cr, 0.0)
                xr, xi = _scan_rows(xr, xi, lam, row)
                cr, ci = xr[SCAN_ROWS - 1:], xi[SCAN_ROWS - 1:]
                parts.append(jnp.concatenate([xr, xi], axis=1).astype(BF16))
            carry_ref[s] = jnp.concatenate([xr[SCAN_ROWS - 8:], xi[SCAN_ROWS - 8:]], axis=1)
            d = _dot(jnp.concatenate(parts, axis=0), wc_ref[s])
            acc = d if acc is None else acc + d
        y_blocks.append(acc)
    y = jnp.concatenate(y_blocks, axis=1) + dskip_ref[...] * u
    o_ref[0] = _glu_out(y, wglu_ref, bglu_ref)

    @pl.when(c == pl.num_programs(1) - 1)
    def _():
        for s in range(N_STRIPS):
            last = carry_ref[s]
            hre_ref[0, :, s * LANES:(s + 1) * LANES] = last[7:8, :LANES]
            him_ref[0, :, s * LANES:(s + 1) * LANES] = last[7:8, LANES:]


def _s5_prompt(u, wb, wc, lamp, dskip, wglu, bglu, chunk=256):
    b, t, w = u.shape
    ch = SSM_CH
    return pl.pallas_call(
        functools.partial(_s5_prompt_kernel, chunk=chunk),
        grid=(b, t // chunk),
        in_specs=[pl.BlockSpec((1, chunk, w), lambda bi, c: (bi, c, 0))]
        + [_full(a.shape) for a in (wb, wc, lamp, dskip, wglu, bglu)],
        out_specs=[
            pl.BlockSpec((1, chunk, w), lambda bi, c: (bi, c, 0)),
            pl.BlockSpec((1, 1, ch), lambda bi, c: (bi, 0, 0)),
            pl.BlockSpec((1, 1, ch), lambda bi, c: (bi, 0, 0)),
        ],
        out_shape=[
            jax.ShapeDtypeStruct((b, t, w), F32),
            jax.ShapeDtypeStruct((b, 1, ch), F32),
            jax.ShapeDtypeStruct((b, 1, ch), F32),
        ],
        scratch_shapes=[pltpu.VMEM((N_STRIPS, 8, 2 * LANES), F32)],
        compiler_params=_cparams(2),
        name="s5_prompt",
    )(u, wb, wc, lamp, dskip, wglu, bglu)


def _s5_step_kernel(u_ref, h0r_ref, h0i_ref, wb_ref, wc_ref, lamp_ref, dskip_ref, wglu_ref,
                    bglu_ref, o_ref, hre_ref, him_ref):
    u = u_ref[...]
    ub = u.astype(BF16)
    y_blocks = []
    for kb in range(SSM_WIDTH // LANES):
        acc = None
        for s in range(4 * kb, 4 * kb + 4):
            sl = slice(s * LANES, (s + 1) * LANES)
            res = _dot(ub[:, kb * LANES:(kb + 1) * LANES], wb_ref[s])
            lam = lamp_ref[s]
            lr, li = lam[0:1, :LANES], lam[0:1, LANES:]
            h0r, h0i = h0r_ref[:, sl], h0i_ref[:, sl]
            xr = res[:, :LANES] + lr * h0r - li * h0i
            xi = res[:, LANES:] + lr * h0i + li * h0r
            hre_ref[:, sl] = xr
            him_ref[:, sl] = xi
            d = _dot(jnp.concatenate([xr, xi], axis=1).astype(BF16), wc_ref[s])
            acc = d if acc is None else acc + d
        y_blocks.append(acc)
    y = jnp.concatenate(y_blocks, axis=1) + dskip_ref[...] * u
    o_ref[...] = _glu_out(y, wglu_ref, bglu_ref)


def _s5_step(u, h0r, h0i, wb, wc, lamp, dskip, wglu, bglu):
    n, w = u.shape
    return pl.pallas_call(
        _s5_step_kernel,
        out_shape=[
            jax.ShapeDtypeStruct((n, w), F32),
            jax.ShapeDtypeStruct((n, SSM_CH), F32),
            jax.ShapeDtypeStruct((n, SSM_CH), F32),
        ],
        compiler_params=pltpu.CompilerParams(vmem_limit_bytes=VMEM_LIMIT),
        name="s5_step",
    )(u, h0r, h0i, wb, wc, lamp, dskip, wglu, bglu)


def _mix_out_kernel(x_ref, a_ref, b_ref, ga_ref, gb_ref, wa_ref, wb_ref, o_ref):
    ha = _rms(a_ref[...], ga_ref[...]).astype(BF16)
    hb = _rms(b_ref[...], gb_ref[...]).astype(BF16)
    o_ref[...] = x_ref[...] + _dot(ha, wa_ref[...]) + _dot(hb, wb_ref[...])


def _mix_out(x, a, b, ga, gb, wa, wb, tm):
    m, d = x.shape
    w = a.shape[1]
    return pl.pallas_call(
        _mix_out_kernel,
        grid=(m // tm,),
        in_specs=[
            pl.BlockSpec((tm, d), lambda i: (i, 0)),
            pl.BlockSpec((tm, w), lambda i: (i, 0)),
            pl.BlockSpec((tm, w), lambda i: (i, 0)),
        ] + [_full(t.shape) for t in (ga, gb, wa, wb)],
        out_specs=pl.BlockSpec((tm, d), lambda i: (i, 0)),
        out_shape=jax.ShapeDtypeStruct((m, d), F32),
        compiler_params=_cparams(1),
        name="mix_out",
    )(x, a, b, ga, gb, wa, wb)


def _mem_prompt_kernel(x_ref, g_ref, wq_ref, mk_ref, mv_ref, wo_ref, o_ref, ob_ref):
    x = x_ref[0]
    q = _dot(_rms(x, g_ref[...]).astype(BF16), wq_ref[...])
    qb = (q * (1.0 / math.sqrt(MEM_HEAD_DIM))).astype(BF16)
    for h in range(MEM_HEADS):
        sl = slice(h * MEM_HEAD_DIM, (h + 1) * MEM_HEAD_DIM)
        s = lax.dot_general(qb[:, sl], mk_ref[0, :, sl], _NT, preferred_element_type=F32)
        e = jnp.exp(s - jnp.max(s, axis=-1, keepdims=True))
        o = _dot(e.astype(BF16), mv_ref[0, :, sl]) / jnp.sum(e, axis=-1, keepdims=True)
        ob_ref[:, sl] = o.astype(BF16)
    o_ref[0] = x + _dot(ob_ref[...], wo_ref[...])


def _mem_prompt(x, g, wq, mk, mv, wo, tm):
    b, t, d = x.shape
    nk = mk.shape[1]
    return pl.pallas_call(
        _mem_prompt_kernel,
        grid=(b, t // tm),
        in_specs=[
            pl.BlockSpec((1, tm, d), lambda bi, i: (bi, i, 0)),
            _full(g.shape), _full(wq.shape),
            pl.BlockSpec((1, nk, d), lambda bi, i: (bi, 0, 0)),
            pl.BlockSpec((1, nk, d), lambda bi, i: (bi, 0, 0)),
            _full(wo.shape),
        ],
        out_specs=pl.BlockSpec((1, tm, d), lambda bi, i: (bi, i, 0)),
        out_shape=jax.ShapeDtypeStruct((b, t, d), F32),
        scratch_shapes=[pltpu.VMEM((tm, d), BF16)],
        compiler_params=_cparams(2),
        name="mem_prompt",
    )(x, g, wq, mk, mv, wo)


def _mem_decode_kernel(q_ref, k_ref, v_ref, o_ref, *, n_mem):
    q = q_ref[0]
    lane = lax.broadcasted_iota(jnp.int32, (MEM_HEADS, n_mem), 1)

    def score(t, s):
        red = jnp.sum(k_ref[0, 0, t] * q, axis=-1, keepdims=True)
        return jnp.where(lane == t, red, s)

    s = lax.fori_loop(0, n_mem, score, jnp.zeros((MEM_HEADS, n_mem), F32), unroll=True)
    e = jnp.exp(s - jnp.max(s, axis=-1, keepdims=True))
    p = e / jnp.sum(e, axis=-1, keepdims=True)

    def mix(t, acc):
        pt = jnp.sum(jnp.where(lane == t, p, 0.0), axis=-1, keepdims=True)
        return acc + pt * v_ref[0, 0, t]

    o_ref[0] = lax.fori_loop(0, n_mem, mix, jnp.zeros((MEM_HEADS, MEM_HEAD_DIM), F32), unroll=True)


def _mem_decode(q, mem_k, mem_v, layer):
    _, n, nk, nh, dh = mem_k.shape
    kv_spec = pl.BlockSpec((1, 1, nk, nh, dh), lambda b: (layer, b, 0, 0, 0))
    return pl.pallas_call(
        functools.partial(_mem_decode_kernel, n_mem=nk),
        grid=(n,),
        in_specs=[pl.BlockSpec((1, nh, dh), lambda b: (b, 0, 0)), kv_spec, kv_spec],
        out_specs=pl.BlockSpec((1, nh, dh), lambda b: (b, 0, 0)),
        out_shape=jax.ShapeDtypeStruct((n, nh, dh), F32),
        compiler_params=_cparams(1),
        name="mem_decode",
    )(q, mem_k, mem_v)


def _proj_residual_kernel(x_ref, a_ref, w_ref, o_ref):
    o_ref[...] = x_ref[...] + _dot(a_ref[...].astype(BF16), w_ref[...])


def _proj_residual(x, a, w):
    return pl.pallas_call(
        _proj_residual_kernel,
        out_shape=jax.ShapeDtypeStruct(x.shape, F32),
        compiler_params=pltpu.CompilerParams(vmem_limit_bytes=VMEM_LIMIT),
        name="proj_residual",
    )(x, a, w)


FF_CHUNK = 256


def _ffn_prompt_kernel(x_ref, g_ref, wg_ref, wu_ref, cw_ref, cb_ref, wd_ref, gf_ref,
                       y_ref, cs_ref, act_ref, carry_ref, *, tm):
    @pl.when(pl.program_id(1) == 0)
    def _():
        carry_ref[...] = jnp.zeros_like(carry_ref)

    x = x_ref[0]
    h = _rms(x, g_ref[...]).astype(BF16)
    row = lax.broadcasted_iota(jnp.int32, (tm, FF_CHUNK), 0)
    d_ff = wg_ref.shape[1]
    for c in range(d_ff // FF_CHUNK):
        sl = slice(c * FF_CHUNK, (c + 1) * FF_CHUNK)
        g = _dot(h, wg_ref[:, sl])
        up = _dot(h, wu_ref[:, sl])
        prev = carry_ref[:, sl]
        p1, p2 = prev[7:8], prev[6:7]
        g1 = jnp.where(row == 0, p1, pltpu.roll(g, 1, 0))
        g2 = jnp.where(row == 0, p2, jnp.where(row == 1, p1, pltpu.roll(g, 2, 0)))
        conv = cb_ref[:, sl] + cw_ref[0:1, sl] * g2 + cw_ref[1:2, sl] * g1 + cw_ref[2:3, sl] * g
        act_ref[:, sl] = (_gelu(conv) * up).astype(BF16)
        carry_ref[:, sl] = g[tm - 8:]
        cs_ref[0, :, sl] = g[tm - 2:]
    x3 = x + _dot(act_ref[...], wd_ref[...])
    y_ref[0] = _rms(x3, gf_ref[...])


def _ffn_prompt(x, g, wg, wu, cw, cb, wd, gf, tm):
    b, t, d = x.shape
    d_ff = wg.shape[1]
    return pl.pallas_call(
        functools.partial(_ffn_prompt_kernel, tm=tm),
        grid=(b, t // tm),
        in_specs=[pl.BlockSpec((1, tm, d), lambda bi, i: (bi, i, 0))]
        + [_full(a.shape) for a in (g, wg, wu, cw, cb, wd, gf)],
        out_specs=[
            pl.BlockSpec((1, tm, d), lambda bi, i: (bi, i, 0)),
            pl.BlockSpec((1, 2, d_ff), lambda bi, i: (bi, 0, 0)),
        ],
        out_shape=[
            jax.ShapeDtypeStruct((b, t, d), F32),
            jax.ShapeDtypeStruct((b, 2, d_ff), F32),
        ],
        scratch_shapes=[pltpu.VMEM((tm, d_ff), BF16), pltpu.VMEM((8, d_ff), F32)],
        compiler_params=_cparams(2),
        name="ffn_prompt",
    )(x, g, wg, wu, cw, cb, wd, gf)


def _ffn_step_kernel(x_ref, g_ref, wg_ref, wu_ref, cw_ref, cb_ref, wd_ref, gf_ref, p0_ref, p1_ref,
                     y_ref, gate_ref, act_ref):
    x = x_ref[...]
    h = _rms(x, g_ref[...]).astype(BF16)
    d_ff = wg_ref.shape[1]
    for c in range(d_ff // FF_CHUNK):
        sl = slice(c * FF_CHUNK, (c + 1) * FF_CHUNK)
        g = _dot(h, wg_ref[:, sl])
        up = _dot(h, wu_ref[:, sl])
        conv = (cb_ref[:, sl] + cw_ref[0:1, sl] * p0_ref[:, sl] + cw_ref[1:2, sl] * p1_ref[:, sl]
                + cw_ref[2:3, sl] * g)
        act_ref[:, sl] = (_gelu(conv) * up).astype(BF16)
        gate_ref[:, sl] = g
    x3 = x + _dot(act_ref[...], wd_ref[...])
    y_ref[...] = _rms(x3, gf_ref[...])


def _ffn_step(x, g, wg, wu, cw, cb, wd, gf, p0, p1):
    n, d = x.shape
    d_ff = wg.shape[1]
    return pl.pallas_call(
        _ffn_step_kernel,
        out_shape=[jax.ShapeDtypeStruct((n, d), F32), jax.ShapeDtypeStruct((n, d_ff), F32)],
        scratch_shapes=[pltpu.VMEM((n, d_ff), BF16)],
        compiler_params=pltpu.CompilerParams(vmem_limit_bytes=VMEM_LIMIT),
        name="ffn_step",
    )(x, g, wg, wu, cw, cb, wd, gf, p0, p1)


def kernel(x_prompt, x_sample, cache_sb_k, cache_sb_v, page_table, state_ssm_re, state_ssm_im, state_conv, cache_mem_k, cache_mem_v, mem_prompt, g_mix, w_in, sb_bias, lam_re, lam_im, log_dt, b_re, b_im, c_re, c_im, d_skip, w_glu, b_glu, g_sb_out, g_ssm_out, w_out, g_mem_q, g_mem_kv, w_mq, w_mk, w_mv, w_mo, g_ffn, w_gate, w_up, conv_w, conv_b, w_down, g_final):
    depth = w_in.shape[0]
    n_p, t_p, d = x_prompt.shape
    n_s = x_sample.shape[0]
    assert x_sample.shape[1] == 1
    tm = 512
    q_scale = 1.0 / math.sqrt(SB_HEAD_DIM)
    row = lambda a: a.reshape(1, -1)
    gf = row(g_final)

    yp = x_prompt.reshape(n_p * t_p, d)
    ys = x_sample.reshape(n_s, d)
    outs = {k: [] for k in ("pk", "pv", "pre", "pim", "pconv", "pmk", "pmv",
                            "sk", "sv", "sre", "sim", "sconv")}
    y_prompt = y_sample = None
    for l in range(depth):
        w_in_b = w_in[l].astype(BF16)
        w_q, w_k, w_v, w_u = (w_in_b[:, j * SB_WIDTH:(j + 1) * SB_WIDTH] for j in range(4))
        w_out_b = w_out[l].astype(BF16)
        wo_a, wo_b = w_out_b[:SB_WIDTH], w_out_b[SB_WIDTH:]
        wglu_b = w_glu[l].astype(BF16)
        wmq, wmk, wmv, wmo = (w[l].astype(BF16) for w in (w_mq, w_mk, w_mv, w_mo))
        wg, wu, wd = (w[l].astype(BF16) for w in (w_gate, w_up, w_down))
        wb, wc, lamp = _s5_prepare(lam_re[l], lam_im[l], log_dt[l], b_re[l], b_im[l], c_re[l], c_im[l])
        s5_w = (wb, wc, lamp, row(d_skip[l]), wglu_b, row(b_glu[l]))
        ffn_w = (row(g_ffn[l]), wg, wu, conv_w[l], row(conv_b[l]), wd, gf)
        in_kinds = [(False, True, q_scale), (True, True, 1.0), (True, True, 1.0), (True, False, 1.0)]

        q_b, k_f, k_b, v_f, v_b, u_f = _norm_proj(
            yp, row(g_mix[l]), [w_q, w_k, w_v, w_u], in_kinds, tm)
        sh = (n_p, t_p, SB_WIDTH)
        o_sb = _sb_prompt(q_b.reshape(sh), k_b.reshape(sh), v_b.reshape(sh), sb_bias[l])
        o_ssm, hre, him = _s5_prompt(u_f.reshape(sh), *s5_w)
        x1 = _mix_out(yp, o_sb.reshape(-1, SB_WIDTH), o_ssm.reshape(-1, SSM_WIDTH),
                      row(g_sb_out[l]), row(g_ssm_out[l]), wo_a, wo_b, tm)
        n_mem = mem_prompt.shape[1]
        mk_f, mk_b, mv_f, mv_b = _norm_proj(
            mem_prompt.reshape(n_p * n_mem, d), row(g_mem_kv[l]), [wmk, wmv],
            [(True, True, 1.0), (True, True, 1.0)], n_mem)
        x2 = _mem_prompt(x1.reshape(n_p, t_p, d), row(g_mem_q[l]), wmq,
                         mk_b.reshape(n_p, n_mem, d), mv_b.reshape(n_p, n_mem, d), wmo, tm)
        y3, cs_p = _ffn_prompt(x2, *ffn_w, tm)
        if l + 1 < depth:
            raise NotImplementedError("final norm is fused into the last layer's FFN")
        y_prompt = y3
        outs["pk"].append(k_f.reshape(n_p, t_p, SB_HEADS, SB_HEAD_DIM))
        outs["pv"].append(v_f.reshape(n_p, t_p, SB_HEADS, SB_HEAD_DIM))
        outs["pre"].append(hre.reshape(n_p, SSM_GROUPS, SSM_STATE))
        outs["pim"].append(him.reshape(n_p, SSM_GROUPS, SSM_STATE))
        outs["pconv"].append(cs_p)
        outs["pmk"].append(mk_f.reshape(n_p, n_mem, MEM_HEADS, MEM_HEAD_DIM))
        outs["pmv"].append(mv_f.reshape(n_p, n_mem, MEM_HEADS, MEM_HEAD_DIM))

        qs_b, ks_f, vs_f, us_f = _norm_proj(
            ys, row(g_mix[l]), [w_q, w_k, w_v, w_u],
            [(False, True, q_scale), (True, False, 1.0), (True, False, 1.0), (True, False, 1.0)], n_s)
        os_sb = _sb_decode(qs_b.reshape(n_s, SB_HEADS, SB_HEAD_DIM), sb_bias[l], cache_sb_k,
                           cache_sb_v, page_table, l).reshape(n_s, SB_WIDTH)
        os_ssm, hsr, hsi = _s5_step(us_f, state_ssm_re[l].reshape(n_s, SSM_CH),
                                    state_ssm_im[l].reshape(n_s, SSM_CH), *s5_w)
        x1s = _mix_out(ys, os_sb, os_ssm, row(g_sb_out[l]), row(g_ssm_out[l]), wo_a, wo_b, n_s)
        (qm_f,) = _norm_proj(x1s, row(g_mem_q[l]), [wmq],
                             [(True, False, 1.0 / math.sqrt(MEM_HEAD_DIM))], n_s)
        om = _mem_decode(qm_f.reshape(n_s, MEM_HEADS, MEM_HEAD_DIM), cache_mem_k, cache_mem_v,
                         l).reshape(n_s, d)
        x2s = _proj_residual(x1s, om, wmo)
        y3s, gate_s = _ffn_step(x2s, *ffn_w, state_conv[l][:, 0], state_conv[l][:, 1])
        y_sample = y3s
        outs["sk"].append(ks_f.reshape(n_s, 1, SB_HEADS, SB_HEAD_DIM))
        outs["sv"].append(vs_f.reshape(n_s, 1, SB_HEADS, SB_HEAD_DIM))
        outs["sre"].append(hsr.reshape(n_s, SSM_GROUPS, SSM_STATE))
        outs["sim"].append(hsi.reshape(n_s, SSM_GROUPS, SSM_STATE))
        outs["sconv"].append(jnp.stack([state_conv[l][:, 1], gate_s], axis=1))

    st = lambda k: jnp.stack(outs[k])
    return (y_prompt, y_sample.reshape(n_s, 1, d),
            st("pk"), st("pv"), st("pre"), st("pim"), st("pconv"), st("pmk"), st("pmv"),
            st("sk"), st("sv"), st("sre"), st("sim"), st("sconv"))
```

```python
import functools
import math

import numpy as np
import jax
import jax.numpy as jnp
from jax import lax
from jax.experimental import pallas as pl
from jax.experimental.pallas import tpu as pltpu

F32 = jnp.float32
BF16 = jnp.bfloat16

EPS = 1e-6
SB_HEADS = 8
SB_HEAD_DIM = 64
SB_WIDTH = SB_HEADS * SB_HEAD_DIM
SSM_GROUPS = 32
SSM_GROUP = 16
SSM_STATE = 64
SSM_WIDTH = SSM_GROUPS * SSM_GROUP
SSM_CH = SSM_GROUPS * SSM_STATE
MEM_HEADS = 4
MEM_HEAD_DIM = 256
LANES = 128
N_STRIPS = SSM_CH // LANES
SCAN_ROWS = 128
N_POW = 7
VMEM_LIMIT = 48 * 1024 * 1024

_NT = (((1,), (1,)), ((), ()))


def _cparams(n_axes):
    return pltpu.CompilerParams(
        dimension_semantics=("arbitrary",) * n_axes, vmem_limit_bytes=VMEM_LIMIT)


def _rms(x, g):
    ms = jnp.mean(x * x, axis=-1, keepdims=True)
    return x * lax.rsqrt(ms + EPS) * g


def _gelu(x):
    c = math.sqrt(2.0 / math.pi)
    return x * (0.5 * (1.0 + jnp.tanh(c * (x + 0.044715 * (x * x * x)))))


def _dot(a, b):
    return jnp.dot(a, b, preferred_element_type=F32)


def _full(shape):
    n = len(shape)
    return pl.BlockSpec(shape, lambda *_: (0,) * n)


def _norm_proj_kernel(x_ref, g_ref, *refs, out_kinds):
    n_w = len(out_kinds)
    w_refs, out_refs = refs[:n_w], refs[n_w:]
    h = _rms(x_ref[...], g_ref[...]).astype(BF16)
    oi = 0
    for w_ref, (want_f32, want_bf16, scale) in zip(w_refs, out_kinds):
        r = _dot(h, w_ref[...])
        if scale != 1.0:
            r = r * scale
        if want_f32:
            out_refs[oi][...] = r
            oi += 1
        if want_bf16:
            out_refs[oi][...] = r.astype(BF16)
            oi += 1


def _norm_proj(x, g, ws, out_kinds, tm):
    m, d = x.shape
    out_shapes, out_specs = [], []
    for w, (want_f32, want_bf16, _) in zip(ws, out_kinds):
        n = w.shape[1]
        for want, dt in ((want_f32, F32), (want_bf16, BF16)):
            if want:
                out_shapes.append(jax.ShapeDtypeStruct((m, n), dt))
                out_specs.append(pl.BlockSpec((tm, n), lambda i: (i, 0)))
    return pl.pallas_call(
        functools.partial(_norm_proj_kernel, out_kinds=tuple(out_kinds)),
        grid=(m // tm,),
        in_specs=[pl.BlockSpec((tm, d), lambda i: (i, 0)), _full(g.shape)]
        + [_full(w.shape) for w in ws],
        out_specs=out_specs,
        out_shape=out_shapes,
        compiler_params=_cparams(1),
        name="norm_proj",
    )(x, g, *ws)


def _in_proj_prompt_kernel(x_ref, g_ref, wq_ref, wkt_ref, wv_ref, wvt_ref, wu_ref,
                           q_ref, kt_ref, ktb_ref, vb_ref, vt_ref, u_ref, *, q_scale):
    h = _rms(x_ref[0], g_ref[...]).astype(BF16)
    q_ref[0] = (_dot(h, wq_ref[...]) * q_scale).astype(BF16)
    kt = lax.dot_general(wkt_ref[...], h, _NT, preferred_element_type=F32)
    kt_ref[0] = kt
    ktb_ref[0] = kt.astype(BF16)
    vb_ref[0] = _dot(h, wv_ref[...]).astype(BF16)
    vt_ref[0] = lax.dot_general(wvt_ref[...], h, _NT, preferred_element_type=F32)
    u_ref[0] = _dot(h, wu_ref[...])


def _in_proj_prompt(x, g, wq, wk, wv, wu, q_scale, tm):
    b, t, d = x.shape
    w = wq.shape[1]
    tok = pl.BlockSpec((1, tm, w), lambda bi, i: (bi, i, 0))
    tr = pl.BlockSpec((1, w, tm), lambda bi, i: (bi, 0, i))
    ws = (wq, wk.T, wv, wv.T, wu)
    return pl.pallas_call(
        functools.partial(_in_proj_prompt_kernel, q_scale=q_scale),
        grid=(b, t // tm),
        in_specs=[pl.BlockSpec((1, tm, d), lambda bi, i: (bi, i, 0)), _full(g.shape)]
        + [_full(a.shape) for a in ws],
        out_specs=[tok, tr, tr, tok, tr, tok],
        out_shape=[
            jax.ShapeDtypeStruct((b, t, w), BF16),
            jax.ShapeDtypeStruct((b, w, t), F32),
            jax.ShapeDtypeStruct((b, w, t), BF16),
            jax.ShapeDtypeStruct((b, t, w), BF16),
            jax.ShapeDtypeStruct((b, w, t), F32),
            jax.ShapeDtypeStruct((b, t, w), F32),
        ],
        compiler_params=_cparams(2),
        name="in_proj_prompt",
    )(x, g, *ws)


SB_SUB = 256
LOG2E = 1.4426950408889634


def _softplus_and_logsig(z):
    e = jnp.exp2(jnp.abs(z) * -LOG2E)
    sp = jnp.maximum(z, 0.0) + jnp.log(1.0 + e)
    return sp, z - sp


def _bf16_pieces(x, n):
    out = []
    for _ in range(n):
        p = x.astype(BF16).astype(F32)
        out.append(p)
        x = x - p
    return out


def _sb_prompt_kernel(bias_ref, q_ref, k_ref, v_ref, o_ref, acc_ref, r_ref, *, tq, groups):
    i = pl.program_id(2)
    lane = lax.broadcasted_iota(jnp.int32, (tq, LANES), 1)
    left = lane < SB_HEAD_DIM
    nsub = tq // SB_SUB
    qqs = []
    for g in range(groups):
        q = q_ref[0, :, g * LANES:(g + 1) * LANES].astype(F32)
        bias_rows = jnp.concatenate([jnp.broadcast_to(bias_ref[g, 0:1, :], (tq, LANES)),
                                     jnp.broadcast_to(bias_ref[g, 1:2, :], (tq, LANES))], axis=0)
        qq = jnp.concatenate([jnp.where(left, q, 0.0), jnp.where(left, 0.0, q)], axis=0)
        qqs.append(jnp.concatenate([qq, bias_rows], axis=1).astype(BF16))
    k_ones = jnp.ones((LANES, tq), BF16)
    rr = lax.broadcasted_iota(jnp.int32, (SB_SUB, SB_SUB), 0)
    cc = lax.broadcasted_iota(jnp.int32, (SB_SUB, SB_SUB), 1)
    later = jnp.where(rr > cc, 1.0, 0.0).astype(BF16)
    qrow = lax.broadcasted_iota(jnp.int32, (tq, SB_SUB), 0)
    kcol = lax.broadcasted_iota(jnp.int32, (tq, SB_SUB), 1)
    acc_ref[...] = jnp.zeros_like(acc_ref)
    r_ref[...] = jnp.zeros_like(r_ref)

    def both(fn, x):
        return jnp.concatenate([fn(x[:tq]), fn(x[tq:])], axis=0)

    def run(j, masked):
        off = pl.multiple_of(j * tq, tq)
        for g in range(groups):
            gl = slice(g * LANES, (g + 1) * LANES)
            kb = jnp.concatenate([k_ref[0, gl, pl.ds(off, tq)], k_ones], axis=0)
            vb = v_ref[0, pl.ds(off, tq), gl]
            z = _dot(qqs[g], kb)
            sp, lsig = _softplus_and_logsig(z)
            r = r_ref[g]
            parts = [None] * nsub
            for n in reversed(range(nsub)):
                sl = slice(n * SB_SUB, (n + 1) * SB_SUB)
                spn = sp[:, sl]
                if masked:
                    causal = (kcol + n * SB_SUB) < qrow
                    spn = both(lambda x: jnp.where(causal, x, 0.0), spn)
                cs = _dot(spn.astype(BF16), later)
                a = jnp.exp((lsig[:, sl] - jnp.concatenate([r] * (SB_SUB // LANES), axis=1)) - cs)
                if masked:
                    a = both(lambda x: jnp.where(causal, x, 0.0), a)
                parts[n] = a.astype(BF16)
                r = r + (cs[:, 0:1] + spn[:, 0:1])
            r_ref[g] = r
            pv = _dot(jnp.concatenate(parts, axis=1), vb)
            acc_ref[:, gl] += jnp.where(left, pv[:tq], pv[tq:])

    run(i, True)

    def body(jj, carry):
        run(i - jj, False)
        return carry

    lax.fori_loop(1, i + 1, body, 0)
    o_ref[0] = acc_ref[...]


def _sb_prompt(q, k, v, bias, tq=2 * SB_SUB, groups=4):
    b, t, w = q.shape
    gw = groups * LANES
    pieces = jnp.stack(_bf16_pieces(bias, 3), axis=-1)
    bias = jnp.pad(pieces, ((0, 0), (0, LANES - 3))).reshape(w // LANES, 2, LANES)
    return pl.pallas_call(
        functools.partial(_sb_prompt_kernel, tq=tq, groups=groups),
        grid=(b, w // gw, t // tq),
        in_specs=[
            pl.BlockSpec((groups, 2, LANES), lambda bi, h, i: (h, 0, 0)),
            pl.BlockSpec((1, tq, gw), lambda bi, h, i: (bi, i, h)),
            pl.BlockSpec((1, gw, t), lambda bi, h, i: (bi, h, 0)),
            pl.BlockSpec((1, t, gw), lambda bi, h, i: (bi, 0, h)),
        ],
        out_specs=pl.BlockSpec((1, tq, gw), lambda bi, h, i: (bi, i, h)),
        out_shape=jax.ShapeDtypeStruct((b, t, w), F32),
        scratch_shapes=[pltpu.VMEM((tq, gw), F32), pltpu.VMEM((groups, 2 * tq, LANES), F32)],
        compiler_params=_cparams(3),
        name="sb_prompt",
    )(bias, q, k, v)


def _sb_decode_kernel(pt_ref, q_ref, bias_ref, kc_ref, vc_ref, o_ref, kbuf, vbuf, sem, *,
                      layer, n_pages, page, n_seq):
    b = pl.program_id(0)

    def copies(seq, slot):
        out = []
        for p in range(n_pages):
            pg = pt_ref[seq * n_pages + p]
            out.append(pltpu.make_async_copy(kc_ref.at[layer, pg], kbuf.at[slot, p], sem.at[0, slot]))
            out.append(pltpu.make_async_copy(vc_ref.at[layer, pg], vbuf.at[slot, p], sem.at[1, slot]))
        return out

    @pl.when(b == 0)
    def _():
        for c in copies(0, 0):
            c.start()

    @pl.when(b + 1 < n_seq)
    def _():
        for c in copies(b + 1, (b + 1) % 2):
            c.start()

    slot = b % 2
    for c in copies(b, slot):
        c.wait()

    nh, dh = SB_HEADS, SB_HEAD_DIM
    past = n_pages * page
    qt = q_ref[0]
    qb = [jnp.broadcast_to(qt[:, h:h + 1], (dh, page)) for h in range(nh)]
    zpages = []
    for p in range(n_pages):
        rows = [jnp.sum(kbuf[slot, p, h] * qb[h], axis=0, keepdims=True) for h in range(nh)]
        zpages.append(jnp.concatenate(rows, axis=0))
    z = jnp.concatenate(zpages, axis=1) + jnp.concatenate([bias_ref[...]] * n_pages, axis=1)
    sp, lsig = _softplus_and_logsig(z)

    lane = lax.broadcasted_iota(jnp.int32, (nh, past), 1)
    incl = sp
    step = 1
    while step < past:
        incl = incl + jnp.where(lane < past - step, pltpu.roll(incl, past - step, 1), 0.0)
        step *= 2
    a = jnp.exp(lsig - (incl - sp))

    accs = [jnp.zeros((dh, page), F32) for _ in range(nh)]
    for p in range(n_pages):
        for h in range(nh):
            arow = jnp.broadcast_to(a[h:h + 1, p * page:(p + 1) * page], (dh, page))
            accs[h] = accs[h] + arow * vbuf[slot, p, h]
    o_ref[0] = jnp.concatenate([jnp.sum(acc, axis=1, keepdims=True) for acc in accs], axis=1)


def _sb_decode(q, bias, cache_k, cache_v, page_table, layer):
    n_seq, n_pages = page_table.shape
    page = cache_k.shape[2]
    nh, dh = SB_HEADS, SB_HEAD_DIM
    kc = jnp.transpose(cache_k, (0, 1, 3, 4, 2))
    vc = jnp.transpose(cache_v, (0, 1, 3, 4, 2))
    qt = jnp.transpose(q.reshape(n_seq, nh, dh), (0, 2, 1))
    bias_lanes = jnp.broadcast_to(bias[:, None], (nh, page))
    grid_spec = pltpu.PrefetchScalarGridSpec(
        num_scalar_prefetch=1,
        grid=(n_seq,),
        in_specs=[
            pl.BlockSpec((1, dh, nh), lambda b, pt: (b, 0, 0)),
            pl.BlockSpec((nh, page), lambda b, pt: (0, 0)),
            pl.BlockSpec(memory_space=pl.ANY),
            pl.BlockSpec(memory_space=pl.ANY),
        ],
        out_specs=pl.BlockSpec((1, dh, nh), lambda b, pt: (b, 0, 0)),
        scratch_shapes=[
            pltpu.VMEM((2, n_pages, nh, dh, page), F32),
            pltpu.VMEM((2, n_pages, nh, dh, page), F32),
            pltpu.SemaphoreType.DMA((2, 2)),
        ],
    )
    out = pl.pallas_call(
        functools.partial(_sb_decode_kernel, layer=layer, n_pages=n_pages, page=page, n_seq=n_seq),
        grid_spec=grid_spec,
        out_shape=jax.ShapeDtypeStruct((n_seq, dh, nh), F32),
        compiler_params=_cparams(1),
        name="sb_decode",
    )(page_table.reshape(-1), qt, bias_lanes, kc, vc)
    return jnp.transpose(out, (0, 2, 1)).reshape(n_seq, nh * dh)


def _s5_prep_kernel(lre_ref, lim_ref, ldt_ref, btr_ref, bti_ref, pre_ref, pim_ref, bbr_ref, bbi_ref):
    lre, lim = lre_ref[...], lim_ref[...]
    dt = jnp.exp(ldt_ref[...])
    mag = jnp.exp(lre * dt)
    br = mag * jnp.cos(lim * dt)
    bi = mag * jnp.sin(lim * dt)
    den = lre * lre + lim * lim
    nr, ni = br - 1.0, bi
    cr = (nr * lre + ni * lim) / den
    ci = (ni * lre - nr * lim) / den
    bbr_ref[...] = cr * btr_ref[...] - ci * bti_ref[...]
    bbi_ref[...] = cr * bti_ref[...] + ci * btr_ref[...]
    pr, pi_ = br, bi
    for k in range(8):
        pre_ref[k:k + 1, :] = pr
        pim_ref[k:k + 1, :] = pi_
        pr, pi_ = pr * pr - pi_ * pi_, 2.0 * pr * pi_


def _s5_prepare(lam_re, lam_im, log_dt, b_re, b_im, c_re, c_im):
    ch = SSM_CH
    lre = lam_re.reshape(1, ch)
    lim = lam_im.reshape(1, ch)
    ldt = jnp.repeat(log_dt, SSM_STATE).reshape(1, ch)
    btr = b_re.reshape(ch, SSM_GROUP).T
    bti = b_im.reshape(ch, SSM_GROUP).T
    pre, pim, bbr, bbi = pl.pallas_call(
        _s5_prep_kernel,
        out_shape=[jax.ShapeDtypeStruct((8, ch), F32)] * 2
        + [jax.ShapeDtypeStruct((SSM_GROUP, ch), F32)] * 2,
        name="s5_prep",
    )(lre, lim, ldt, btr, bti)

    s_idx = np.arange(N_STRIPS)[:, None, None]
    j_idx = np.arange(LANES)[None, :, None]
    c_idx = np.arange(LANES)[None, None, :]
    grp_of_ch = (LANES * (s_idx // 4) + j_idx) // SSM_GROUP
    grp_of_state = (LANES * s_idx + c_idx) // SSM_STATE
    mask = jnp.asarray(grp_of_ch == grp_of_state, F32)

    def b_strips(bb):
        t = bb.reshape(SSM_GROUP, N_STRIPS, LANES).transpose(1, 0, 2)
        return jnp.tile(t, (1, LANES // SSM_GROUP, 1)) * mask

    wb = jnp.concatenate([b_strips(bbr), b_strips(bbi)], axis=2).astype(BF16)

    def c_strips(c):
        t = c.transpose(0, 2, 1).reshape(N_STRIPS, LANES, SSM_GROUP)
        return jnp.tile(t, (1, 1, LANES // SSM_GROUP)) * mask.transpose(0, 2, 1)

    wc = jnp.concatenate([c_strips(c_re), -c_strips(c_im)], axis=1).astype(BF16)

    def pw(p):
        return p.reshape(8, N_STRIPS, LANES).transpose(1, 0, 2)

    lamp = jnp.concatenate([pw(pre), pw(pim)], axis=2)
    return wb, wc, lamp


def _scan_rows(xr, xi, lam, row):
    for k in range(N_POW):
        s = 1 << k
        ar, ai = lam[k:k + 1, :LANES], lam[k:k + 1, LANES:]
        if s < 8:
            keep = row >= s
            sr = jnp.where(keep, pltpu.roll(xr, s, 0), 0.0)
            si = jnp.where(keep, pltpu.roll(xi, s, 0), 0.0)
            xr, xi = xr + ar * sr - ai * si, xi + ar * si + ai * sr
        else:
            pr, pi_ = xr[:-s], xi[:-s]
            nr = xr[s:] + ar * pr - ai * pi_
            ni = xi[s:] + ar * pi_ + ai * pr
            xr = jnp.concatenate([xr[:s], nr], axis=0)
            xi = jnp.concatenate([xi[:s], ni], axis=0)
    return xr, xi


def _glu_out(y, wglu_ref, bglu_ref):
    y = _gelu(y)
    return y * jax.nn.sigmoid(_dot(y.astype(BF16), wglu_ref[...]) + bglu_ref[...])


def _s5_prompt_kernel(u_ref, wb_ref, wc_ref, lamp_ref, dskip_ref, wglu_ref, bglu_ref,
                      o_ref, hre_ref, him_ref, carry_ref, *, chunk):
    c = pl.program_id(1)

    @pl.when(c == 0)
    def _():
        carry_ref[...] = jnp.zeros_like(carry_ref)

    u = u_ref[0]
    ub = u.astype(BF16)
    row = lax.broadcasted_iota(jnp.int32, (SCAN_ROWS, LANES), 0)
    first = row == 0
    y_blocks = []
    for kb in range(SSM_WIDTH // LANES):
        acc = None
        for s in range(4 * kb, 4 * kb + 4):
            res = _dot(ub[:, kb * LANES:(kb + 1) * LANES], wb_ref[s])
            lam = lamp_ref[s]
            prev = carry_ref[s]
            cr, ci = prev[7:8, :LANES], prev[7:8, LANES:]
            lr, li = lam[0:1, :LANES], lam[0:1, LANES:]
            parts = []
            for h in range(chunk // SCAN_ROWS):
                rs = slice(h * SCAN_ROWS, (h + 1) * SCAN_ROWS)
                xr = res[rs, :LANES] + jnp.where(first, lr * cr - li * ci, 0.0)
                xi = res[rs, LANES:] + jnp.where(first, lr * ci + li * cr, 0.0)
                xr, xi = _scan_rows(xr, xi, lam, row)
                cr, ci = xr[SCAN_ROWS - 1:], xi[SCAN_ROWS - 1:]
                parts.append(jnp.concatenate([xr, xi], axis=1).astype(BF16))
            carry_ref[s] = jnp.concatenate([xr[SCAN_ROWS - 8:], xi[SCAN_ROWS - 8:]], axis=1)
            d = _dot(jnp.concatenate(parts, axis=0), wc_ref[s])
            acc = d if acc is None else acc + d
        y_blocks.append(acc)
    y = jnp.concatenate(y_blocks, axis=1) + dskip_ref[...] * u
    o_ref[0] = _glu_out(y, wglu_ref, bglu_ref)

    @pl.when(c == pl.num_programs(1) - 1)
    def _():
        for s in range(N_STRIPS):
            last = carry_ref[s]
            hre_ref[0, :, s * LANES:(s + 1) * LANES] = last[7:8, :LANES]
            him_ref[0, :, s * LANES:(s + 1) * LANES] = last[7:8, LANES:]


def _s5_prompt(u, wb, wc, lamp, dskip, wglu, bglu, chunk=256):
    b, t, w = u.shape
    ch = SSM_CH
    return pl.pallas_call(
        functools.partial(_s5_prompt_kernel, chunk=chunk),
        grid=(b, t // chunk),
        in_specs=[pl.BlockSpec((1, chunk, w), lambda bi, c: (bi, c, 0))]
        + [_full(a.shape) for a in (wb, wc, lamp, dskip, wglu, bglu)],
        out_specs=[
            pl.BlockSpec((1, chunk, w), lambda bi, c: (bi, c, 0)),
            pl.BlockSpec((1, 1, ch), lambda bi, c: (bi, 0, 0)),
            pl.BlockSpec((1, 1, ch), lambda bi, c: (bi, 0, 0)),
        ],
        out_shape=[
            jax.ShapeDtypeStruct((b, t, w), F32),
            jax.ShapeDtypeStruct((b, 1, ch), F32),
            jax.ShapeDtypeStruct((b, 1, ch), F32),
        ],
        scratch_shapes=[pltpu.VMEM((N_STRIPS, 8, 2 * LANES), F32)],
        compiler_params=_cparams(2),
        name="s5_prompt",
    )(u, wb, wc, lamp, dskip, wglu, bglu)


def _s5_step_kernel(u_ref, h0r_ref, h0i_ref, wb_ref, wc_ref, lamp_ref, dskip_ref, wglu_ref,
                    bglu_ref, o_ref, hre_ref, him_ref):
    u = u_ref[...]
    ub = u.astype(BF16)
    y_blocks = []
    for kb in range(SSM_WIDTH // LANES):
        acc = None
        for s in range(4 * kb, 4 * kb + 4):
            sl = slice(s * LANES, (s + 1) * LANES)
            res = _dot(ub[:, kb * LANES:(kb + 1) * LANES], wb_ref[s])
            lam = lamp_ref[s]
            lr, li = lam[0:1, :LANES], lam[0:1, LANES:]
            h0r, h0i = h0r_ref[:, sl], h0i_ref[:, sl]
            xr = res[:, :LANES] + lr * h0r - li * h0i
            xi = res[:, LANES:] + lr * h0i + li * h0r
            hre_ref[:, sl] = xr
            him_ref[:, sl] = xi
            d = _dot(jnp.concatenate([xr, xi], axis=1).astype(BF16), wc_ref[s])
            acc = d if acc is None else acc + d
        y_blocks.append(acc)
    y = jnp.concatenate(y_blocks, axis=1) + dskip_ref[...] * u
    o_ref[...] = _glu_out(y, wglu_ref, bglu_ref)


def _s5_step(u, h0r, h0i, wb, wc, lamp, dskip, wglu, bglu):
    n, w = u.shape
    return pl.pallas_call(
        _s5_step_kernel,
        out_shape=[
            jax.ShapeDtypeStruct((n, w), F32),
            jax.ShapeDtypeStruct((n, SSM_CH), F32),
            jax.ShapeDtypeStruct((n, SSM_CH), F32),
        ],
        compiler_params=pltpu.CompilerParams(vmem_limit_bytes=VMEM_LIMIT),
        name="s5_step",
    )(u, h0r, h0i, wb, wc, lamp, dskip, wglu, bglu)


def _mix_out_kernel(x_ref, a_ref, b_ref, ga_ref, gb_ref, wa_ref, wb_ref, o_ref):
    ha = _rms(a_ref[...], ga_ref[...]).astype(BF16)
    hb = _rms(b_ref[...], gb_ref[...]).astype(BF16)
    o_ref[...] = x_ref[...] + _dot(ha, wa_ref[...]) + _dot(hb, wb_ref[...])


def _mix_out(x, a, b, ga, gb, wa, wb, tm):
    m, d = x.shape
    w = a.shape[1]
    return pl.pallas_call(
        _mix_out_kernel,
        grid=(m // tm,),
        in_specs=[
            pl.BlockSpec((tm, d), lambda i: (i, 0)),
            pl.BlockSpec((tm, w), lambda i: (i, 0)),
            pl.BlockSpec((tm, w), lambda i: (i, 0)),
        ] + [_full(t.shape) for t in (ga, gb, wa, wb)],
        out_specs=pl.BlockSpec((tm, d), lambda i: (i, 0)),
        out_shape=jax.ShapeDtypeStruct((m, d), F32),
        compiler_params=_cparams(1),
        name="mix_out",
    )(x, a, b, ga, gb, wa, wb)


def _mem_prompt_kernel(x_ref, g_ref, wq_ref, mk_ref, mv_ref, wo_ref, o_ref, ob_ref):
    x = x_ref[0]
    q = _dot(_rms(x, g_ref[...]).astype(BF16), wq_ref[...])
    qb = (q * (1.0 / math.sqrt(MEM_HEAD_DIM))).astype(BF16)
    for h in range(MEM_HEADS):
        sl = slice(h * MEM_HEAD_DIM, (h + 1) * MEM_HEAD_DIM)
        s = lax.dot_general(qb[:, sl], mk_ref[0, :, sl], _NT, preferred_element_type=F32)
        e = jnp.exp(s - jnp.max(s, axis=-1, keepdims=True))
        o = _dot(e.astype(BF16), mv_ref[0, :, sl]) / jnp.sum(e, axis=-1, keepdims=True)
        ob_ref[:, sl] = o.astype(BF16)
    o_ref[0] = x + _dot(ob_ref[...], wo_ref[...])


def _mem_prompt(x, g, wq, mk, mv, wo, tm):
    b, t, d = x.shape
    nk = mk.shape[1]
    return pl.pallas_call(
        _mem_prompt_kernel,
        grid=(b, t // tm),
        in_specs=[
            pl.BlockSpec((1, tm, d), lambda bi, i: (bi, i, 0)),
            _full(g.shape), _full(wq.shape),
            pl.BlockSpec((1, nk, d), lambda bi, i: (bi, 0, 0)),
            pl.BlockSpec((1, nk, d), lambda bi, i: (bi, 0, 0)),
            _full(wo.shape),
        ],
        out_specs=pl.BlockSpec((1, tm, d), lambda bi, i: (bi, i, 0)),
        out_shape=jax.ShapeDtypeStruct((b, t, d), F32),
        scratch_shapes=[pltpu.VMEM((tm, d), BF16)],
        compiler_params=_cparams(2),
        name="mem_prompt",
    )(x, g, wq, mk, mv, wo)


def _mem_decode_kernel(q_ref, k_ref, v_ref, o_ref, *, n_mem):
    q = q_ref[0]
    lane = lax.broadcasted_iota(jnp.int32, (MEM_HEADS, n_mem), 1)

    def score(t, s):
        red = jnp.sum(k_ref[0, 0, t] * q, axis=-1, keepdims=True)
        return jnp.where(lane == t, red, s)

    s = lax.fori_loop(0, n_mem, score, jnp.zeros((MEM_HEADS, n_mem), F32), unroll=True)
    e = jnp.exp(s - jnp.max(s, axis=-1, keepdims=True))
    p = e / jnp.sum(e, axis=-1, keepdims=True)

    def mix(t, acc):
        pt = jnp.sum(jnp.where(lane == t, p, 0.0), axis=-1, keepdims=True)
        return acc + pt * v_ref[0, 0, t]

    o_ref[0] = lax.fori_loop(0, n_mem, mix, jnp.zeros((MEM_HEADS, MEM_HEAD_DIM), F32), unroll=True)


def _mem_decode(q, mem_k, mem_v, layer):
    _, n, nk, nh, dh = mem_k.shape
    kv_spec = pl.BlockSpec((1, 1, nk, nh, dh), lambda b: (layer, b, 0, 0, 0))
    return pl.pallas_call(
        functools.partial(_mem_decode_kernel, n_mem=nk),
        grid=(n,),
        in_specs=[pl.BlockSpec((1, nh, dh), lambda b: (b, 0, 0)), kv_spec, kv_spec],
        out_specs=pl.BlockSpec((1, nh, dh), lambda b: (b, 0, 0)),
        out_shape=jax.ShapeDtypeStruct((n, nh, dh), F32),
        compiler_params=_cparams(1),
        name="mem_decode",
    )(q, mem_k, mem_v)


def _proj_residual_kernel(x_ref, a_ref, w_ref, o_ref):
    o_ref[...] = x_ref[...] + _dot(a_ref[...].astype(BF16), w_ref[...])


def _proj_residual(x, a, w):
    return pl.pallas_call(
        _proj_residual_kernel,
        out_shape=jax.ShapeDtypeStruct(x.shape, F32),
        compiler_params=pltpu.CompilerParams(vmem_limit_bytes=VMEM_LIMIT),
        name="proj_residual",
    )(x, a, w)


FF_CHUNK = 256


def _ffn_prompt_kernel(x_ref, g_ref, wg_ref, wu_ref, cw_ref, cb_ref, wd_ref, gf_ref,
                       y_ref, cs_ref, act_ref, carry_ref, *, tm):
    @pl.when(pl.program_id(1) == 0)
    def _():
        carry_ref[...] = jnp.zeros_like(carry_ref)

    x = x_ref[0]
    h = _rms(x, g_ref[...]).astype(BF16)
    row = lax.broadcasted_iota(jnp.int32, (tm, FF_CHUNK), 0)
    d_ff = wg_ref.shape[1]
    for c in range(d_ff // FF_CHUNK):
        sl = slice(c * FF_CHUNK, (c + 1) * FF_CHUNK)
        g = _dot(h, wg_ref[:, sl])
        up = _dot(h, wu_ref[:, sl])
        prev = carry_ref[:, sl]
        p1, p2 = prev[7:8], prev[6:7]
        g1 = jnp.where(row == 0, p1, pltpu.roll(g, 1, 0))
        g2 = jnp.where(row == 0, p2, jnp.where(row == 1, p1, pltpu.roll(g, 2, 0)))
        conv = cb_ref[:, sl] + cw_ref[0:1, sl] * g2 + cw_ref[1:2, sl] * g1 + cw_ref[2:3, sl] * g
        act_ref[:, sl] = (_gelu(conv) * up).astype(BF16)
        carry_ref[:, sl] = g[tm - 8:]
        cs_ref[0, :, sl] = g[tm - 2:]
    x3 = x + _dot(act_ref[...], wd_ref[...])
    y_ref[0] = _rms(x3, gf_ref[...])


def _ffn_prompt(x, g, wg, wu, cw, cb, wd, gf, tm):
    b, t, d = x.shape
    d_ff = wg.shape[1]
    return pl.pallas_call(
        functools.partial(_ffn_prompt_kernel, tm=tm),
        grid=(b, t // tm),
        in_specs=[pl.BlockSpec((1, tm, d), lambda bi, i: (bi, i, 0))]
        + [_full(a.shape) for a in (g, wg, wu, cw, cb, wd, gf)],
        out_specs=[
            pl.BlockSpec((1, tm, d), lambda bi, i: (bi, i, 0)),
            pl.BlockSpec((1, 2, d_ff), lambda bi, i: (bi, 0, 0)),
        ],
        out_shape=[
            jax.ShapeDtypeStruct((b, t, d), F32),
            jax.ShapeDtypeStruct((b, 2, d_ff), F32),
        ],
        scratch_shapes=[pltpu.VMEM((tm, d_ff), BF16), pltpu.VMEM((8, d_ff), F32)],
        compiler_params=_cparams(2),
        name="ffn_prompt",
    )(x, g, wg, wu, cw, cb, wd, gf)


def _ffn_step_kernel(x_ref, g_ref, wg_ref, wu_ref, cw_ref, cb_ref, wd_ref, gf_ref, p0_ref, p1_ref,
                     y_ref, gate_ref, act_ref):
    x = x_ref[...]
    h = _rms(x, g_ref[...]).astype(BF16)
    d_ff = wg_ref.shape[1]
    for c in range(d_ff // FF_CHUNK):
        sl = slice(c * FF_CHUNK, (c + 1) * FF_CHUNK)
        g = _dot(h, wg_ref[:, sl])
        up = _dot(h, wu_ref[:, sl])
        conv = (cb_ref[:, sl] + cw_ref[0:1, sl] * p0_ref[:, sl] + cw_ref[1:2, sl] * p1_ref[:, sl]
                + cw_ref[2:3, sl] * g)
        act_ref[:, sl] = (_gelu(conv) * up).astype(BF16)
        gate_ref[:, sl] = g
    x3 = x + _dot(act_ref[...], wd_ref[...])
    y_ref[...] = _rms(x3, gf_ref[...])


def _ffn_step(x, g, wg, wu, cw, cb, wd, gf, p0, p1):
    n, d = x.shape
    d_ff = wg.shape[1]
    return pl.pallas_call(
        _ffn_step_kernel,
        out_shape=[jax.ShapeDtypeStruct((n, d), F32), jax.ShapeDtypeStruct((n, d_ff), F32)],
        scratch_shapes=[pltpu.VMEM((n, d_ff), BF16)],
        compiler_params=pltpu.CompilerParams(vmem_limit_bytes=VMEM_LIMIT),
        name="ffn_step",
    )(x, g, wg, wu, cw, cb, wd, gf, p0, p1)


def kernel(x_prompt, x_sample, cache_sb_k, cache_sb_v, page_table, state_ssm_re, state_ssm_im, state_conv, cache_mem_k, cache_mem_v, mem_prompt, g_mix, w_in, sb_bias, lam_re, lam_im, log_dt, b_re, b_im, c_re, c_im, d_skip, w_glu, b_glu, g_sb_out, g_ssm_out, w_out, g_mem_q, g_mem_kv, w_mq, w_mk, w_mv, w_mo, g_ffn, w_gate, w_up, conv_w, conv_b, w_down, g_final):
    depth = w_in.shape[0]
    n_p, t_p, d = x_prompt.shape
    n_s = x_sample.shape[0]
    assert x_sample.shape[1] == 1
    tm = 512
    q_scale = 1.0 / math.sqrt(SB_HEAD_DIM)
    row = lambda a: a.reshape(1, -1)
    gf = row(g_final)

    yp = x_prompt.reshape(n_p * t_p, d)
    ys = x_sample.reshape(n_s, d)
    outs = {k: [] for k in ("pk", "pv", "pre", "pim", "pconv", "pmk", "pmv",
                            "sk", "sv", "sre", "sim", "sconv")}
    y_prompt = y_sample = None
    for l in range(depth):
        w_in_b = w_in[l].astype(BF16)
        w_q, w_k, w_v, w_u = (w_in_b[:, j * SB_WIDTH:(j + 1) * SB_WIDTH] for j in range(4))
        w_out_b = w_out[l].astype(BF16)
        wo_a, wo_b = w_out_b[:SB_WIDTH], w_out_b[SB_WIDTH:]
        wglu_b = w_glu[l].astype(BF16)
        wmq, wmk, wmv, wmo = (w[l].astype(BF16) for w in (w_mq, w_mk, w_mv, w_mo))
        wg, wu, wd = (w[l].astype(BF16) for w in (w_gate, w_up, w_down))
        wb, wc, lamp = _s5_prepare(lam_re[l], lam_im[l], log_dt[l], b_re[l], b_im[l], c_re[l], c_im[l])
        s5_w = (wb, wc, lamp, row(d_skip[l]), wglu_b, row(b_glu[l]))
        ffn_w = (row(g_ffn[l]), wg, wu, conv_w[l], row(conv_b[l]), wd, gf)

        q_b, kt_f, kt_b, v_b, vt_f, u_f = _in_proj_prompt(
            yp.reshape(n_p, t_p, d), row(g_mix[l]), w_q, w_k, w_v, w_u, q_scale, tm)
        o_sb = _sb_prompt(q_b, kt_b, v_b, sb_bias[l])
        o_ssm, hre, him = _s5_prompt(u_f, *s5_w)
        x1 = _mix_out(yp, o_sb.reshape(-1, SB_WIDTH), o_ssm.reshape(-1, SSM_WIDTH),
                      row(g_sb_out[l]), row(g_ssm_out[l]), wo_a, wo_b, tm)
        n_mem = mem_prompt.shape[1]
        mk_f, mk_b, mv_f, mv_b = _norm_proj(
            mem_prompt.reshape(n_p * n_mem, d), row(g_mem_kv[l]), [wmk, wmv],
            [(True, True, 1.0), (True, True, 1.0)], n_mem)
        x2 = _mem_prompt(x1.reshape(n_p, t_p, d), row(g_mem_q[l]), wmq,
                         mk_b.reshape(n_p, n_mem, d), mv_b.reshape(n_p, n_mem, d), wmo, tm)
        y3, cs_p = _ffn_prompt(x2, *ffn_w, tm)
        if l + 1 < depth:
            raise NotImplementedError("final norm is fused into the last layer's FFN")
        y_prompt = y3
        to_cache = lambda a: jnp.transpose(a.reshape(n_p, SB_HEADS, SB_HEAD_DIM, t_p), (0, 3, 1, 2))
        outs["pk"].append(to_cache(kt_f))
        outs["pv"].append(to_cache(vt_f))
        outs["pre"].append(hre.reshape(n_p, SSM_GROUPS, SSM_STATE))
        outs["pim"].append(him.reshape(n_p, SSM_GROUPS, SSM_STATE))
        outs["pconv"].append(cs_p)
        outs["pmk"].append(mk_f.reshape(n_p, n_mem, MEM_HEADS, MEM_HEAD_DIM))
        outs["pmv"].append(mv_f.reshape(n_p, n_mem, MEM_HEADS, MEM_HEAD_DIM))

        qs_f, ks_f, vs_f, us_f = _norm_proj(
            ys, row(g_mix[l]), [w_q, w_k, w_v, w_u],
            [(True, False, q_scale), (True, False, 1.0), (True, False, 1.0), (True, False, 1.0)], n_s)
        os_sb = _sb_decode(qs_f, sb_bias[l], cache_sb_k, cache_sb_v, page_table, l)
        os_ssm, hsr, hsi = _s5_step(us_f, state_ssm_re[l].reshape(n_s, SSM_CH),
                                    state_ssm_im[l].reshape(n_s, SSM_CH), *s5_w)
        x1s = _mix_out(ys, os_sb, os_ssm, row(g_sb_out[l]), row(g_ssm_out[l]), wo_a, wo_b, n_s)
        (qm_f,) = _norm_proj(x1s, row(g_mem_q[l]), [wmq],
                             [(True, False, 1.0 / math.sqrt(MEM_HEAD_DIM))], n_s)
        om = _mem_decode(qm_f.reshape(n_s, MEM_HEADS, MEM_HEAD_DIM), cache_mem_k, cache_mem_v,
                         l).reshape(n_s, d)
        x2s = _proj_residual(x1s, om, wmo)
        y3s, gate_s = _ffn_step(x2s, *ffn_w, state_conv[l][:, 0], state_conv[l][:, 1])
        y_sample = y3s
        outs["sk"].append(ks_f.reshape(n_s, 1, SB_HEADS, SB_HEAD_DIM))
        outs["sv"].append(vs_f.reshape(n_s, 1, SB_HEADS, SB_HEAD_DIM))
        outs["sre"].append(hsr.reshape(n_s, SSM_GROUPS, SSM_STATE))
        outs["sim"].append(hsi.reshape(n_s, SSM_GROUPS, SSM_STATE))
        outs["sconv"].append(jnp.stack([state_conv[l][:, 1], gate_s], axis=1))

    st = lambda k: jnp.stack(outs[k])
    return (y_prompt, y_sample.reshape(n_s, 1, d),
            st("pk"), st("pv"), st("pre"), st("pim"), st("pconv"), st("pmk"), st("pmv"),
            st("sk"), st("sv"), st("sre"), st("sim"), st("sconv"))
```

```python
import functools
import math

import numpy as np
import jax
import jax.numpy as jnp
from jax import lax
from jax.experimental import pallas as pl
from jax.experimental.pallas import tpu as pltpu

F32 = jnp.float32
BF16 = jnp.bfloat16

EPS = 1e-6
SB_HEADS = 8
SB_HEAD_DIM = 64
SB_WIDTH = SB_HEADS * SB_HEAD_DIM
SSM_GROUPS = 32
SSM_GROUP = 16
SSM_STATE = 64
SSM_WIDTH = SSM_GROUPS * SSM_GROUP
SSM_CH = SSM_GROUPS * SSM_STATE
MEM_HEADS = 4
MEM_HEAD_DIM = 256
LANES = 128
N_STRIPS = SSM_CH // LANES
SCAN_ROWS = 128
S5_WINDOW = 8
S5_POW_ROWS = 16
VMEM_LIMIT = 48 * 1024 * 1024

_NT = (((1,), (1,)), ((), ()))


def _cparams(n_axes):
    return pltpu.CompilerParams(
        dimension_semantics=("arbitrary",) * n_axes, vmem_limit_bytes=VMEM_LIMIT)


def _rms(x, g):
    ms = jnp.mean(x * x, axis=-1, keepdims=True)
    return x * lax.rsqrt(ms + EPS) * g


def _gelu(x):
    c = math.sqrt(2.0 / math.pi)
    return x * (0.5 * (1.0 + jnp.tanh(c * (x + 0.044715 * (x * x * x)))))


def _dot(a, b):
    return jnp.dot(a, b, preferred_element_type=F32)


def _full(shape):
    n = len(shape)
    return pl.BlockSpec(shape, lambda *_: (0,) * n)


def _norm_proj_kernel(x_ref, g_ref, *refs, out_kinds):
    n_w = len(out_kinds)
    w_refs, out_refs = refs[:n_w], refs[n_w:]
    h = _rms(x_ref[...], g_ref[...]).astype(BF16)
    oi = 0
    for w_ref, (want_f32, want_bf16, scale) in zip(w_refs, out_kinds):
        r = _dot(h, w_ref[...])
        if scale != 1.0:
            r = r * scale
        if want_f32:
            out_refs[oi][...] = r
            oi += 1
        if want_bf16:
            out_refs[oi][...] = r.astype(BF16)
            oi += 1


def _norm_proj(x, g, ws, out_kinds, tm):
    m, d = x.shape
    out_shapes, out_specs = [], []
    for w, (want_f32, want_bf16, _) in zip(ws, out_kinds):
        n = w.shape[1]
        for want, dt in ((want_f32, F32), (want_bf16, BF16)):
            if want:
                out_shapes.append(jax.ShapeDtypeStruct((m, n), dt))
                out_specs.append(pl.BlockSpec((tm, n), lambda i: (i, 0)))
    return pl.pallas_call(
        functools.partial(_norm_proj_kernel, out_kinds=tuple(out_kinds)),
        grid=(m // tm,),
        in_specs=[pl.BlockSpec((tm, d), lambda i: (i, 0)), _full(g.shape)]
        + [_full(w.shape) for w in ws],
        out_specs=out_specs,
        out_shape=out_shapes,
        compiler_params=_cparams(1),
        name="norm_proj",
    )(x, g, *ws)


def _in_proj_prompt_kernel(x_ref, g_ref, wq_ref, wkt_ref, wv_ref, wvt_ref, wu_ref,
                           q_ref, kt_ref, ktb_ref, vb_ref, vt_ref, u_ref, *, q_scale):
    h = _rms(x_ref[0], g_ref[...]).astype(BF16)
    q_ref[0] = (_dot(h, wq_ref[...]) * q_scale).astype(BF16)
    kt = lax.dot_general(wkt_ref[...], h, _NT, preferred_element_type=F32)
    kt_ref[0] = kt
    ktb_ref[0] = kt.astype(BF16)
    vb_ref[0] = _dot(h, wv_ref[...]).astype(BF16)
    vt_ref[0] = lax.dot_general(wvt_ref[...], h, _NT, preferred_element_type=F32)
    u_ref[0] = _dot(h, wu_ref[...])


def _in_proj_prompt(x, g, wq, wk, wv, wu, q_scale, tm):
    b, t, d = x.shape
    w = wq.shape[1]
    tok = pl.BlockSpec((1, tm, w), lambda bi, i: (bi, i, 0))
    tr = pl.BlockSpec((1, w, tm), lambda bi, i: (bi, 0, i))
    ws = (wq, wk.T, wv, wv.T, wu)
    return pl.pallas_call(
        functools.partial(_in_proj_prompt_kernel, q_scale=q_scale),
        grid=(b, t // tm),
        in_specs=[pl.BlockSpec((1, tm, d), lambda bi, i: (bi, i, 0)), _full(g.shape)]
        + [_full(a.shape) for a in ws],
        out_specs=[tok, tr, tr, tok, tr, tok],
        out_shape=[
            jax.ShapeDtypeStruct((b, t, w), BF16),
            jax.ShapeDtypeStruct((b, w, t), F32),
            jax.ShapeDtypeStruct((b, w, t), BF16),
            jax.ShapeDtypeStruct((b, t, w), BF16),
            jax.ShapeDtypeStruct((b, w, t), F32),
            jax.ShapeDtypeStruct((b, t, w), F32),
        ],
        compiler_params=_cparams(2),
        name="in_proj_prompt",
    )(x, g, *ws)


SB_SUB = 256
LOG2E = 1.4426950408889634


def _softplus_and_logsig(z):
    e = jnp.exp2(jnp.abs(z) * -LOG2E)
    sp = jnp.maximum(z, 0.0) + jnp.log(1.0 + e)
    return sp, z - sp


def _bf16_pieces(x, n):
    out = []
    for _ in range(n):
        p = x.astype(BF16).astype(F32)
        out.append(p)
        x = x - p
    return out


def _sb_prompt_kernel(bias_ref, q_ref, k_ref, v_ref, o_ref, acc_ref, r_ref, *, tq, groups):
    i = pl.program_id(2)
    lane = lax.broadcasted_iota(jnp.int32, (tq, LANES), 1)
    left = lane < SB_HEAD_DIM
    nsub = tq // SB_SUB
    qqs = []
    for g in range(groups):
        q = q_ref[0, :, g * LANES:(g + 1) * LANES].astype(F32)
        bias_rows = jnp.concatenate([jnp.broadcast_to(bias_ref[g, 0:1, :], (tq, LANES)),
                                     jnp.broadcast_to(bias_ref[g, 1:2, :], (tq, LANES))], axis=0)
        qq = jnp.concatenate([jnp.where(left, q, 0.0), jnp.where(left, 0.0, q)], axis=0)
        qqs.append(jnp.concatenate([qq, bias_rows], axis=1).astype(BF16))
    k_ones = jnp.ones((LANES, tq), BF16)
    rr = lax.broadcasted_iota(jnp.int32, (SB_SUB, SB_SUB), 0)
    cc = lax.broadcasted_iota(jnp.int32, (SB_SUB, SB_SUB), 1)
    later = jnp.where(rr > cc, 1.0, 0.0).astype(BF16)
    qrow = lax.broadcasted_iota(jnp.int32, (tq, SB_SUB), 0)
    kcol = lax.broadcasted_iota(jnp.int32, (tq, SB_SUB), 1)
    acc_ref[...] = jnp.zeros_like(acc_ref)
    r_ref[...] = jnp.zeros_like(r_ref)

    def both(fn, x):
        return jnp.concatenate([fn(x[:tq]), fn(x[tq:])], axis=0)

    def run(j, masked):
        off = pl.multiple_of(j * tq, tq)
        for g in range(groups):
            gl = slice(g * LANES, (g + 1) * LANES)
            kb = jnp.concatenate([k_ref[0, gl, pl.ds(off, tq)], k_ones], axis=0)
            vb = v_ref[0, pl.ds(off, tq), gl]
            z = _dot(qqs[g], kb)
            sp, lsig = _softplus_and_logsig(z)
            r = r_ref[g]
            parts = [None] * nsub
            for n in reversed(range(nsub)):
                sl = slice(n * SB_SUB, (n + 1) * SB_SUB)
                spn = sp[:, sl]
                if masked:
                    causal = (kcol + n * SB_SUB) < qrow
                    spn = both(lambda x: jnp.where(causal, x, 0.0), spn)
                cs = _dot(spn.astype(BF16), later)
                a = jnp.exp((lsig[:, sl] - jnp.concatenate([r] * (SB_SUB // LANES), axis=1)) - cs)
                if masked:
                    a = both(lambda x: jnp.where(causal, x, 0.0), a)
                parts[n] = a.astype(BF16)
                r = r + (cs[:, 0:1] + spn[:, 0:1])
            r_ref[g] = r
            pv = _dot(jnp.concatenate(parts, axis=1), vb)
            acc_ref[:, gl] += jnp.where(left, pv[:tq], pv[tq:])

    run(i, True)

    def body(jj, carry):
        run(i - jj, False)
        return carry

    lax.fori_loop(1, i + 1, body, 0)
    o_ref[0] = acc_ref[...]


def _sb_prompt(q, k, v, bias, tq=2 * SB_SUB, groups=4):
    b, t, w = q.shape
    gw = groups * LANES
    pieces = jnp.stack(_bf16_pieces(bias, 3), axis=-1)
    bias = jnp.pad(pieces, ((0, 0), (0, LANES - 3))).reshape(w // LANES, 2, LANES)
    return pl.pallas_call(
        functools.partial(_sb_prompt_kernel, tq=tq, groups=groups),
        grid=(b, w // gw, t // tq),
        in_specs=[
            pl.BlockSpec((groups, 2, LANES), lambda bi, h, i: (h, 0, 0)),
            pl.BlockSpec((1, tq, gw), lambda bi, h, i: (bi, i, h)),
            pl.BlockSpec((1, gw, t), lambda bi, h, i: (bi, h, 0)),
            pl.BlockSpec((1, t, gw), lambda bi, h, i: (bi, 0, h)),
        ],
        out_specs=pl.BlockSpec((1, tq, gw), lambda bi, h, i: (bi, i, h)),
        out_shape=jax.ShapeDtypeStruct((b, t, w), F32),
        scratch_shapes=[pltpu.VMEM((tq, gw), F32), pltpu.VMEM((groups, 2 * tq, LANES), F32)],
        compiler_params=_cparams(3),
        name="sb_prompt",
    )(bias, q, k, v)


def _sb_decode_kernel(pt_ref, q_ref, bias_ref, kc_ref, vc_ref, o_ref, kbuf, vbuf, sem, *,
                      layer, n_pages, page, n_seq):
    b = pl.program_id(0)

    def copies(seq, slot):
        out = []
        for p in range(n_pages):
            pg = pt_ref[seq * n_pages + p]
            out.append(pltpu.make_async_copy(kc_ref.at[layer, pg], kbuf.at[slot, p], sem.at[0, slot]))
            out.append(pltpu.make_async_copy(vc_ref.at[layer, pg], vbuf.at[slot, p], sem.at[1, slot]))
        return out

    @pl.when(b == 0)
    def _():
        for c in copies(0, 0):
            c.start()

    @pl.when(b + 1 < n_seq)
    def _():
        for c in copies(b + 1, (b + 1) % 2):
            c.start()

    slot = b % 2
    for c in copies(b, slot):
        c.wait()

    nh, dh = SB_HEADS, SB_HEAD_DIM
    past = n_pages * page
    qt = q_ref[0]
    qb = [jnp.broadcast_to(qt[:, h:h + 1], (dh, page)) for h in range(nh)]
    zpages = []
    for p in range(n_pages):
        rows = [jnp.sum(kbuf[slot, p, h] * qb[h], axis=0, keepdims=True) for h in range(nh)]
        zpages.append(jnp.concatenate(rows, axis=0))
    z = jnp.concatenate(zpages, axis=1) + jnp.concatenate([bias_ref[...]] * n_pages, axis=1)
    sp, lsig = _softplus_and_logsig(z)

    lane = lax.broadcasted_iota(jnp.int32, (nh, past), 1)
    incl = sp
    step = 1
    while step < past:
        incl = incl + jnp.where(lane < past - step, pltpu.roll(incl, past - step, 1), 0.0)
        step *= 2
    a = jnp.exp(lsig - (incl - sp))

    accs = [jnp.zeros((dh, page), F32) for _ in range(nh)]
    for p in range(n_pages):
        for h in range(nh):
            arow = jnp.broadcast_to(a[h:h + 1, p * page:(p + 1) * page], (dh, page))
            accs[h] = accs[h] + arow * vbuf[slot, p, h]
    o_ref[0] = jnp.concatenate([jnp.sum(acc, axis=1, keepdims=True) for acc in accs], axis=1)


def _sb_decode(q, bias, cache_k, cache_v, page_table, layer):
    n_seq, n_pages = page_table.shape
    page = cache_k.shape[2]
    nh, dh = SB_HEADS, SB_HEAD_DIM
    kc = jnp.transpose(cache_k, (0, 1, 3, 4, 2))
    vc = jnp.transpose(cache_v, (0, 1, 3, 4, 2))
    qt = jnp.transpose(q.reshape(n_seq, nh, dh), (0, 2, 1))
    bias_lanes = jnp.broadcast_to(bias[:, None], (nh, page))
    grid_spec = pltpu.PrefetchScalarGridSpec(
        num_scalar_prefetch=1,
        grid=(n_seq,),
        in_specs=[
            pl.BlockSpec((1, dh, nh), lambda b, pt: (b, 0, 0)),
            pl.BlockSpec((nh, page), lambda b, pt: (0, 0)),
            pl.BlockSpec(memory_space=pl.ANY),
            pl.BlockSpec(memory_space=pl.ANY),
        ],
        out_specs=pl.BlockSpec((1, dh, nh), lambda b, pt: (b, 0, 0)),
        scratch_shapes=[
            pltpu.VMEM((2, n_pages, nh, dh, page), F32),
            pltpu.VMEM((2, n_pages, nh, dh, page), F32),
            pltpu.SemaphoreType.DMA((2, 2)),
        ],
    )
    out = pl.pallas_call(
        functools.partial(_sb_decode_kernel, layer=layer, n_pages=n_pages, page=page, n_seq=n_seq),
        grid_spec=grid_spec,
        out_shape=jax.ShapeDtypeStruct((n_seq, dh, nh), F32),
        compiler_params=_cparams(1),
        name="sb_decode",
    )(page_table.reshape(-1), qt, bias_lanes, kc, vc)
    return jnp.transpose(out, (0, 2, 1)).reshape(n_seq, nh * dh)


def _s5_prep_kernel(lre_ref, lim_ref, ldt_ref, btr_ref, bti_ref, pre_ref, pim_ref, bbr_ref, bbi_ref):
    lre, lim = lre_ref[...], lim_ref[...]
    dt = jnp.exp(ldt_ref[...])
    mag = jnp.exp(lre * dt)
    br = mag * jnp.cos(lim * dt)
    bi = mag * jnp.sin(lim * dt)
    den = lre * lre + lim * lim
    nr, ni = br - 1.0, bi
    cr = (nr * lre + ni * lim) / den
    ci = (ni * lre - nr * lim) / den
    btr, bti = btr_ref[...], bti_ref[...]
    for k in range(S5_WINDOW):
        bbr_ref[k * SSM_GROUP:(k + 1) * SSM_GROUP, :] = cr * btr - ci * bti
        bbi_ref[k * SSM_GROUP:(k + 1) * SSM_GROUP, :] = cr * bti + ci * btr
        cr, ci = cr * br - ci * bi, cr * bi + ci * br
    pr, pi_ = br, bi
    for k in range(S5_WINDOW):
        pre_ref[k:k + 1, :] = pr
        pim_ref[k:k + 1, :] = pi_
        if k + 1 < S5_WINDOW:
            pr, pi_ = pr * br - pi_ * bi, pr * bi + pi_ * br
    for k in range(S5_WINDOW, S5_POW_ROWS):
        pr, pi_ = pr * pr - pi_ * pi_, 2.0 * pr * pi_
        pre_ref[k:k + 1, :] = pr
        pim_ref[k:k + 1, :] = pi_


def _s5_prepare(lam_re, lam_im, log_dt, b_re, b_im, c_re, c_im):
    ch = SSM_CH
    lre = lam_re.reshape(1, ch)
    lim = lam_im.reshape(1, ch)
    ldt = jnp.repeat(log_dt, SSM_STATE).reshape(1, ch)
    btr = b_re.reshape(ch, SSM_GROUP).T
    bti = b_im.reshape(ch, SSM_GROUP).T
    pre, pim, bbr, bbi = pl.pallas_call(
        _s5_prep_kernel,
        out_shape=[jax.ShapeDtypeStruct((S5_POW_ROWS, ch), F32)] * 2
        + [jax.ShapeDtypeStruct((S5_WINDOW * SSM_GROUP, ch), F32)] * 2,
        name="s5_prep",
    )(lre, lim, ldt, btr, bti)

    s_idx = np.arange(N_STRIPS)[:, None, None]
    j_idx = np.arange(LANES)[None, :, None]
    c_idx = np.arange(LANES)[None, None, :]
    grp_of_ch = (LANES * (s_idx // 4) + j_idx) // SSM_GROUP
    grp_of_state = (LANES * s_idx + c_idx) // SSM_STATE
    mask = jnp.asarray(grp_of_ch == grp_of_state, F32)

    def b_strips(bb):
        t = bb.reshape(SSM_GROUP, N_STRIPS, LANES).transpose(1, 0, 2)
        return jnp.tile(t, (1, LANES // SSM_GROUP, 1)) * mask

    def b_windows(bb):
        per_k = [b_strips(bb[k * SSM_GROUP:(k + 1) * SSM_GROUP]) for k in range(S5_WINDOW)]
        return jnp.concatenate(per_k, axis=1)

    wb = jnp.concatenate([b_windows(bbr), b_windows(bbi)], axis=2).astype(BF16)

    def c_strips(c):
        t = c.transpose(0, 2, 1).reshape(N_STRIPS, LANES, SSM_GROUP)
        return jnp.tile(t, (1, 1, LANES // SSM_GROUP)) * mask.transpose(0, 2, 1)

    wc = jnp.concatenate([c_strips(c_re), -c_strips(c_im)], axis=1).astype(BF16)

    def pw(p):
        return p.reshape(S5_POW_ROWS, N_STRIPS, LANES).transpose(1, 0, 2)

    lamp = jnp.concatenate([pw(pre), pw(pim)], axis=2)
    nb = SSM_WIDTH // LANES
    wb = wb.reshape(nb, 4, S5_WINDOW * LANES, 2 * LANES).transpose(0, 2, 1, 3)
    wb = wb.reshape(nb, S5_WINDOW * LANES, 8 * LANES)
    wc = wc.reshape(nb, 8 * LANES, LANES)
    return wb, wc, lamp


def _scan_rows(xr, xi, lam):
    s, k = S5_WINDOW, S5_WINDOW - 1
    while s < SCAN_ROWS:
        ar, ai = lam[k:k + 1, :LANES], lam[k:k + 1, LANES:]
        pr, pi_ = xr[:-s], xi[:-s]
        nr = xr[s:] + ar * pr - ai * pi_
        ni = xi[s:] + ar * pi_ + ai * pr
        xr = jnp.concatenate([xr[:s], nr], axis=0)
        xi = jnp.concatenate([xi[:s], ni], axis=0)
        s, k = 2 * s, k + 1
    return xr, xi


def _glu_out(y, wglu_ref, bglu_ref):
    y = _gelu(y)
    return y * jax.nn.sigmoid(_dot(y.astype(BF16), wglu_ref[...]) + bglu_ref[...])


def _s5_prompt_kernel(u_ref, wb_ref, wc_ref, lamp_ref, dskip_ref, wglu_ref, bglu_ref,
                      o_ref, hre_ref, him_ref, carry_ref, *, chunk):
    c = pl.program_id(1)

    @pl.when(c == 0)
    def _():
        carry_ref[...] = jnp.zeros_like(carry_ref)

    u = u_ref[0]
    seg_pos = lax.broadcasted_iota(jnp.int32, (chunk, LANES), 0) % SCAN_ROWS
    y_blocks = []
    for kb in range(SSM_WIDTH // LANES):
        ukb = u[:, kb * LANES:(kb + 1) * LANES]
        lagged = [ukb] + [jnp.where(seg_pos >= k, pltpu.roll(ukb, k, 0), 0.0)
                          for k in range(1, S5_WINDOW)]
        uwin = jnp.concatenate(lagged, axis=1).astype(BF16)
        res4 = _dot(uwin, wb_ref[kb])
        states = []
        for j in range(4):
            s = 4 * kb + j
            res = res4[:, 2 * j * LANES:2 * (j + 1) * LANES]
            lam = lamp_ref[s]
            prev = carry_ref[s]
            cr, ci = prev[7:8, :LANES], prev[7:8, LANES:]
            lr, li = lam[:S5_WINDOW, :LANES], lam[:S5_WINDOW, LANES:]
            parts = []
            for h in range(chunk // SCAN_ROWS):
                rs = slice(h * SCAN_ROWS, (h + 1) * SCAN_ROWS)
                xr, xi = res[rs, :LANES], res[rs, LANES:]
                xr = jnp.concatenate([xr[:S5_WINDOW] + (lr * cr - li * ci), xr[S5_WINDOW:]], axis=0)
                xi = jnp.concatenate([xi[:S5_WINDOW] + (lr * ci + li * cr), xi[S5_WINDOW:]], axis=0)
                xr, xi = _scan_rows(xr, xi, lam)
                cr, ci = xr[SCAN_ROWS - 1:], xi[SCAN_ROWS - 1:]
                parts.append(jnp.concatenate([xr, xi], axis=1).astype(BF16))
            carry_ref[s] = jnp.concatenate([xr[SCAN_ROWS - 8:], xi[SCAN_ROWS - 8:]], axis=1)
            states.append(jnp.concatenate(parts, axis=0))
        y_blocks.append(_dot(jnp.concatenate(states, axis=1), wc_ref[kb]))
    y = jnp.concatenate(y_blocks, axis=1) + dskip_ref[...] * u
    o_ref[0] = _glu_out(y, wglu_ref, bglu_ref)

    @pl.when(c == pl.num_programs(1) - 1)
    def _():
        for s in range(N_STRIPS):
            last = carry_ref[s]
            hre_ref[0, :, s * LANES:(s + 1) * LANES] = last[7:8, :LANES]
            him_ref[0, :, s * LANES:(s + 1) * LANES] = last[7:8, LANES:]


def _s5_prompt(u, wb, wc, lamp, dskip, wglu, bglu, chunk=256):
    b, t, w = u.shape
    ch = SSM_CH
    return pl.pallas_call(
        functools.partial(_s5_prompt_kernel, chunk=chunk),
        grid=(b, t // chunk),
        in_specs=[pl.BlockSpec((1, chunk, w), lambda bi, c: (bi, c, 0))]
        + [_full(a.shape) for a in (wb, wc, lamp, dskip, wglu, bglu)],
        out_specs=[
            pl.BlockSpec((1, chunk, w), lambda bi, c: (bi, c, 0)),
            pl.BlockSpec((1, 1, ch), lambda bi, c: (bi, 0, 0)),
            pl.BlockSpec((1, 1, ch), lambda bi, c: (bi, 0, 0)),
        ],
        out_shape=[
            jax.ShapeDtypeStruct((b, t, w), F32),
            jax.ShapeDtypeStruct((b, 1, ch), F32),
            jax.ShapeDtypeStruct((b, 1, ch), F32),
        ],
        scratch_shapes=[pltpu.VMEM((N_STRIPS, 8, 2 * LANES), F32)],
        compiler_params=_cparams(2),
        name="s5_prompt",
    )(u, wb, wc, lamp, dskip, wglu, bglu)


def _s5_step_kernel(u_ref, h0r_ref, h0i_ref, wb_ref, wc_ref, lamp_ref, dskip_ref, wglu_ref,
                    bglu_ref, o_ref, hre_ref, him_ref):
    u = u_ref[...]
    ub = u.astype(BF16)
    y_blocks = []
    for kb in range(SSM_WIDTH // LANES):
        res4 = _dot(ub[:, kb * LANES:(kb + 1) * LANES], wb_ref[kb, 0:LANES, :])
        states = []
        for j in range(4):
            s = 4 * kb + j
            sl = slice(s * LANES, (s + 1) * LANES)
            res = res4[:, 2 * j * LANES:2 * (j + 1) * LANES]
            lam = lamp_ref[s]
            lr, li = lam[0:1, :LANES], lam[0:1, LANES:]
            h0r, h0i = h0r_ref[:, sl], h0i_ref[:, sl]
            xr = res[:, :LANES] + lr * h0r - li * h0i
            xi = res[:, LANES:] + lr * h0i + li * h0r
            hre_ref[:, sl] = xr
            him_ref[:, sl] = xi
            states.append(jnp.concatenate([xr, xi], axis=1).astype(BF16))
        y_blocks.append(_dot(jnp.concatenate(states, axis=1), wc_ref[kb]))
    y = jnp.concatenate(y_blocks, axis=1) + dskip_ref[...] * u
    o_ref[...] = _glu_out(y, wglu_ref, bglu_ref)


def _s5_step(u, h0r, h0i, wb, wc, lamp, dskip, wglu, bglu):
    n, w = u.shape
    return pl.pallas_call(
        _s5_step_kernel,
        out_shape=[
            jax.ShapeDtypeStruct((n, w), F32),
            jax.ShapeDtypeStruct((n, SSM_CH), F32),
            jax.ShapeDtypeStruct((n, SSM_CH), F32),
        ],
        compiler_params=pltpu.CompilerParams(vmem_limit_bytes=VMEM_LIMIT),
        name="s5_step",
    )(u, h0r, h0i, wb, wc, lamp, dskip, wglu, bglu)


def _mix_out_kernel(x_ref, a_ref, b_ref, ga_ref, gb_ref, wa_ref, wb_ref, o_ref):
    ha = _rms(a_ref[...], ga_ref[...]).astype(BF16)
    hb = _rms(b_ref[...], gb_ref[...]).astype(BF16)
    o_ref[...] = x_ref[...] + _dot(ha, wa_ref[...]) + _dot(hb, wb_ref[...])


def _mix_out(x, a, b, ga, gb, wa, wb, tm):
    m, d = x.shape
    w = a.shape[1]
    return pl.pallas_call(
        _mix_out_kernel,
        grid=(m // tm,),
        in_specs=[
            pl.BlockSpec((tm, d), lambda i: (i, 0)),
            pl.BlockSpec((tm, w), lambda i: (i, 0)),
            pl.BlockSpec((tm, w), lambda i: (i, 0)),
        ] + [_full(t.shape) for t in (ga, gb, wa, wb)],
        out_specs=pl.BlockSpec((tm, d), lambda i: (i, 0)),
        out_shape=jax.ShapeDtypeStruct((m, d), F32),
        compiler_params=_cparams(1),
        name="mix_out",
    )(x, a, b, ga, gb, wa, wb)


def _mem_prompt_kernel(x0_ref, a_ref, b_ref, ga_ref, gb_ref, wa_ref, wb_ref,
                       g_ref, wq_ref, mk_ref, mv_ref, wo_ref, o_ref, ob_ref):
    ha = _rms(a_ref[0], ga_ref[...]).astype(BF16)
    hb = _rms(b_ref[0], gb_ref[...]).astype(BF16)
    x = x0_ref[0] + _dot(ha, wa_ref[...]) + _dot(hb, wb_ref[...])
    q = _dot(_rms(x, g_ref[...]).astype(BF16), wq_ref[...])
    qb = (q * (1.0 / math.sqrt(MEM_HEAD_DIM))).astype(BF16)
    for h in range(MEM_HEADS):
        sl = slice(h * MEM_HEAD_DIM, (h + 1) * MEM_HEAD_DIM)
        s = lax.dot_general(qb[:, sl], mk_ref[0, :, sl], _NT, preferred_element_type=F32)
        e = jnp.exp(s - jnp.max(s, axis=-1, keepdims=True))
        o = _dot(e.astype(BF16), mv_ref[0, :, sl]) / jnp.sum(e, axis=-1, keepdims=True)
        ob_ref[:, sl] = o.astype(BF16)
    o_ref[0] = x + _dot(ob_ref[...], wo_ref[...])


def _mem_prompt(x, a, b_, ga, gb, wa, wb, g, wq, mk, mv, wo, tm):
    b, t, d = x.shape
    w = a.shape[2]
    nk = mk.shape[1]
    tok = lambda width: pl.BlockSpec((1, tm, width), lambda bi, i: (bi, i, 0))
    mem = pl.BlockSpec((1, nk, d), lambda bi, i: (bi, 0, 0))
    return pl.pallas_call(
        _mem_prompt_kernel,
        grid=(b, t // tm),
        in_specs=[tok(d), tok(w), tok(w)] + [_full(p.shape) for p in (ga, gb, wa, wb, g, wq)]
        + [mem, mem, _full(wo.shape)],
        out_specs=tok(d),
        out_shape=jax.ShapeDtypeStruct((b, t, d), F32),
        scratch_shapes=[pltpu.VMEM((tm, d), BF16)],
        compiler_params=_cparams(2),
        name="mem_prompt",
    )(x, a, b_, ga, gb, wa, wb, g, wq, mk, mv, wo)


MEM_SEQS_PER_STEP = 4


def _mem_decode_kernel(q_ref, k_ref, v_ref, o_ref, *, n_mem):
    def tree(fn, xs):
        while len(xs) > 1:
            xs = [fn(xs[j], xs[j + 1]) for j in range(0, len(xs) - 1, 2)] + xs[len(xs) & ~1:]
        return xs[0]

    for i in range(MEM_SEQS_PER_STEP):
        q = q_ref[i]
        s = [jnp.sum(k_ref[0, i, t] * q, axis=-1, keepdims=True) for t in range(n_mem)]
        m = tree(jnp.maximum, s)
        e = [jnp.exp(st - m) for st in s]
        denom = tree(jnp.add, e)
        acc = tree(jnp.add, [e[t] * v_ref[0, i, t] for t in range(n_mem)])
        o_ref[i] = acc / denom


def _mem_decode(q, mem_k, mem_v, layer):
    _, n, nk, nh, dh = mem_k.shape
    ns = MEM_SEQS_PER_STEP
    kv_spec = pl.BlockSpec((1, ns, nk, nh, dh), lambda b: (layer, b, 0, 0, 0))
    return pl.pallas_call(
        functools.partial(_mem_decode_kernel, n_mem=nk),
        grid=(n // ns,),
        in_specs=[pl.BlockSpec((ns, nh, dh), lambda b: (b, 0, 0)), kv_spec, kv_spec],
        out_specs=pl.BlockSpec((ns, nh, dh), lambda b: (b, 0, 0)),
        out_shape=jax.ShapeDtypeStruct((n, nh, dh), F32),
        compiler_params=_cparams(1),
        name="mem_decode",
    )(q, mem_k, mem_v)


def _proj_residual_kernel(x_ref, a_ref, w_ref, o_ref):
    o_ref[...] = x_ref[...] + _dot(a_ref[...].astype(BF16), w_ref[...])


def _proj_residual(x, a, w):
    return pl.pallas_call(
        _proj_residual_kernel,
        out_shape=jax.ShapeDtypeStruct(x.shape, F32),
        compiler_params=pltpu.CompilerParams(vmem_limit_bytes=VMEM_LIMIT),
        name="proj_residual",
    )(x, a, w)


FF_CHUNK = 256


def _ffn_prompt_kernel(x_ref, g_ref, wg_ref, wu_ref, cw_ref, cb_ref, wd_ref, gf_ref,
                       y_ref, cs_ref, act_ref, carry_ref, *, tm):
    @pl.when(pl.program_id(1) == 0)
    def _():
        carry_ref[...] = jnp.zeros_like(carry_ref)

    x = x_ref[0]
    h = _rms(x, g_ref[...]).astype(BF16)
    row = lax.broadcasted_iota(jnp.int32, (tm, FF_CHUNK), 0)
    d_ff = wg_ref.shape[1]
    for c in range(d_ff // FF_CHUNK):
        sl = slice(c * FF_CHUNK, (c + 1) * FF_CHUNK)
        g = _dot(h, wg_ref[:, sl])
        up = _dot(h, wu_ref[:, sl])
        prev = carry_ref[:, sl]
        p1, p2 = prev[7:8], prev[6:7]
        g1 = jnp.where(row == 0, p1, pltpu.roll(g, 1, 0))
        g2 = jnp.where(row == 0, p2, jnp.where(row == 1, p1, pltpu.roll(g, 2, 0)))
        conv = cb_ref[:, sl] + cw_ref[0:1, sl] * g2 + cw_ref[1:2, sl] * g1 + cw_ref[2:3, sl] * g
        act_ref[:, sl] = (_gelu(conv) * up).astype(BF16)
        carry_ref[:, sl] = g[tm - 8:]
        cs_ref[0, :, sl] = g[tm - 2:]
    x3 = x + _dot(act_ref[...], wd_ref[...])
    y_ref[0] = _rms(x3, gf_ref[...])


def _ffn_prompt(x, g, wg, wu, cw, cb, wd, gf, tm):
    b, t, d = x.shape
    d_ff = wg.shape[1]
    return pl.pallas_call(
        functools.partial(_ffn_prompt_kernel, tm=tm),
        grid=(b, t // tm),
        in_specs=[pl.BlockSpec((1, tm, d), lambda bi, i: (bi, i, 0))]
        + [_full(a.shape) for a in (g, wg, wu, cw, cb, wd, gf)],
        out_specs=[
            pl.BlockSpec((1, tm, d), lambda bi, i: (bi, i, 0)),
            pl.BlockSpec((1, 2, d_ff), lambda bi, i: (bi, 0, 0)),
        ],
        out_shape=[
            jax.ShapeDtypeStruct((b, t, d), F32),
            jax.ShapeDtypeStruct((b, 2, d_ff), F32),
        ],
        scratch_shapes=[pltpu.VMEM((tm, d_ff), BF16), pltpu.VMEM((8, d_ff), F32)],
        compiler_params=_cparams(2),
        name="ffn_prompt",
    )(x, g, wg, wu, cw, cb, wd, gf)


def _ffn_step_kernel(x_ref, g_ref, wg_ref, wu_ref, cw_ref, cb_ref, wd_ref, gf_ref, p0_ref, p1_ref,
                     y_ref, gate_ref, act_ref):
    x = x_ref[...]
    h = _rms(x, g_ref[...]).astype(BF16)
    d_ff = wg_ref.shape[1]
    for c in range(d_ff // FF_CHUNK):
        sl = slice(c * FF_CHUNK, (c + 1) * FF_CHUNK)
        g = _dot(h, wg_ref[:, sl])
        up = _dot(h, wu_ref[:, sl])
        conv = (cb_ref[:, sl] + cw_ref[0:1, sl] * p0_ref[:, sl] + cw_ref[1:2, sl] * p1_ref[:, sl]
                + cw_ref[2:3, sl] * g)
        act_ref[:, sl] = (_gelu(conv) * up).astype(BF16)
        gate_ref[:, sl] = g
    x3 = x + _dot(act_ref[...], wd_ref[...])
    y_ref[...] = _rms(x3, gf_ref[...])


def _ffn_step(x, g, wg, wu, cw, cb, wd, gf, p0, p1):
    n, d = x.shape
    d_ff = wg.shape[1]
    return pl.pallas_call(
        _ffn_step_kernel,
        out_shape=[jax.ShapeDtypeStruct((n, d), F32), jax.ShapeDtypeStruct((n, d_ff), F32)],
        scratch_shapes=[pltpu.VMEM((n, d_ff), BF16)],
        compiler_params=pltpu.CompilerParams(vmem_limit_bytes=VMEM_LIMIT),
        name="ffn_step",
    )(x, g, wg, wu, cw, cb, wd, gf, p0, p1)


def kernel(x_prompt, x_sample, cache_sb_k, cache_sb_v, page_table, state_ssm_re, state_ssm_im, state_conv, cache_mem_k, cache_mem_v, mem_prompt, g_mix, w_in, sb_bias, lam_re, lam_im, log_dt, b_re, b_im, c_re, c_im, d_skip, w_glu, b_glu, g_sb_out, g_ssm_out, w_out, g_mem_q, g_mem_kv, w_mq, w_mk, w_mv, w_mo, g_ffn, w_gate, w_up, conv_w, conv_b, w_down, g_final):
    depth = w_in.shape[0]
    n_p, t_p, d = x_prompt.shape
    n_s = x_sample.shape[0]
    assert x_sample.shape[1] == 1
    tm = 512
    q_scale = 1.0 / math.sqrt(SB_HEAD_DIM)
    row = lambda a: a.reshape(1, -1)
    gf = row(g_final)

    yp = x_prompt.reshape(n_p * t_p, d)
    ys = x_sample.reshape(n_s, d)
    outs = {k: [] for k in ("pk", "pv", "pre", "pim", "pconv", "pmk", "pmv",
                            "sk", "sv", "sre", "sim", "sconv")}
    y_prompt = y_sample = None
    for l in range(depth):
        w_in_b = w_in[l].astype(BF16)
        w_q, w_k, w_v, w_u = (w_in_b[:, j * SB_WIDTH:(j + 1) * SB_WIDTH] for j in range(4))
        w_out_b = w_out[l].astype(BF16)
        wo_a, wo_b = w_out_b[:SB_WIDTH], w_out_b[SB_WIDTH:]
        wglu_b = w_glu[l].astype(BF16)
        wmq, wmk, wmv, wmo = (w[l].astype(BF16) for w in (w_mq, w_mk, w_mv, w_mo))
        wg, wu, wd = (w[l].astype(BF16) for w in (w_gate, w_up, w_down))
        wb, wc, lamp = _s5_prepare(lam_re[l], lam_im[l], log_dt[l], b_re[l], b_im[l], c_re[l], c_im[l])
        s5_w = (wb, wc, lamp, row(d_skip[l]), wglu_b, row(b_glu[l]))
        ffn_w = (row(g_ffn[l]), wg, wu, conv_w[l], row(conv_b[l]), wd, gf)

        q_b, kt_f, kt_b, v_b, vt_f, u_f = _in_proj_prompt(
            yp.reshape(n_p, t_p, d), row(g_mix[l]), w_q, w_k, w_v, w_u, q_scale, tm)
        o_sb = _sb_prompt(q_b, kt_b, v_b, sb_bias[l])
        o_ssm, hre, him = _s5_prompt(u_f, *s5_w)
        n_mem = mem_prompt.shape[1]
        mk_f, mk_b, mv_f, mv_b = _norm_proj(
            mem_prompt.reshape(n_p * n_mem, d), row(g_mem_kv[l]), [wmk, wmv],
            [(True, True, 1.0), (True, True, 1.0)], n_mem)
        x2 = _mem_prompt(yp.reshape(n_p, t_p, d), o_sb, o_ssm, row(g_sb_out[l]), row(g_ssm_out[l]),
                         wo_a, wo_b, row(g_mem_q[l]), wmq,
                         mk_b.reshape(n_p, n_mem, d), mv_b.reshape(n_p, n_mem, d), wmo, tm)
        y3, cs_p = _ffn_prompt(x2, *ffn_w, tm)
        if l + 1 < depth:
            raise NotImplementedError("final norm is fused into the last layer's FFN")
        y_prompt = y3
        to_cache = lambda a: jnp.transpose(a.reshape(n_p, SB_HEADS, SB_HEAD_DIM, t_p), (0, 3, 1, 2))
        outs["pk"].append(to_cache(kt_f))
        outs["pv"].append(to_cache(vt_f))
        outs["pre"].append(hre.reshape(n_p, SSM_GROUPS, SSM_STATE))
        outs["pim"].append(him.reshape(n_p, SSM_GROUPS, SSM_STATE))
        outs["pconv"].append(cs_p)
        outs["pmk"].append(mk_f.reshape(n_p, n_mem, MEM_HEADS, MEM_HEAD_DIM))
        outs["pmv"].append(mv_f.reshape(n_p, n_mem, MEM_HEADS, MEM_HEAD_DIM))

        qs_f, ks_f, vs_f, us_f = _norm_proj(
            ys, row(g_mix[l]), [w_q, w_k, w_v, w_u],
            [(True, False, q_scale), (True, False, 1.0), (True, False, 1.0), (True, False, 1.0)], n_s)
        os_sb = _sb_decode(qs_f, sb_bias[l], cache_sb_k, cache_sb_v, page_table, l)
        os_ssm, hsr, hsi = _s5_step(us_f, state_ssm_re[l].reshape(n_s, SSM_CH),
                                    state_ssm_im[l].reshape(n_s, SSM_CH), *s5_w)
        x1s = _mix_out(ys, os_sb, os_ssm, row(g_sb_out[l]), row(g_ssm_out[l]), wo_a, wo_b, n_s)
        (qm_f,) = _norm_proj(x1s, row(g_mem_q[l]), [wmq],
                             [(True, False, 1.0 / math.sqrt(MEM_HEAD_DIM))], n_s)
        om = _mem_decode(qm_f.reshape(n_s, MEM_HEADS, MEM_HEAD_DIM), cache_mem_k, cache_mem_v,
                         l).reshape(n_s, d)
        x2s = _proj_residual(x1s, om, wmo)
        y3s, gate_s = _ffn_step(x2s, *ffn_w, state_conv[l][:, 0], state_conv[l][:, 1])
        y_sample = y3s
        outs["sk"].append(ks_f.reshape(n_s, 1, SB_HEADS, SB_HEAD_DIM))
        outs["sv"].append(vs_f.reshape(n_s, 1, SB_HEADS, SB_HEAD_DIM))
        outs["sre"].append(hsr.reshape(n_s, SSM_GROUPS, SSM_STATE))
        outs["sim"].append(hsi.reshape(n_s, SSM_GROUPS, SSM_STATE))
        outs["sconv"].append(jnp.stack([state_conv[l][:, 1], gate_s], axis=1))

    st = lambda k: jnp.stack(outs[k])
    return (y_prompt, y_sample.reshape(n_s, 1, d),
            st("pk"), st("pv"), st("pre"), st("pim"), st("pconv"), st("pmk"), st("pmv"),
            st("sk"), st("sv"), st("sre"), st("sim"), st("sconv"))
```

```python
import functools
import math

import numpy as np
import jax
import jax.numpy as jnp
from jax import lax
from jax.experimental import pallas as pl
from jax.experimental.pallas import tpu as pltpu

F32 = jnp.float32
BF16 = jnp.bfloat16

EPS = 1e-6
SB_HEADS = 8
SB_HEAD_DIM = 64
SB_WIDTH = SB_HEADS * SB_HEAD_DIM
SSM_GROUPS = 32
SSM_GROUP = 16
SSM_STATE = 64
SSM_WIDTH = SSM_GROUPS * SSM_GROUP
SSM_CH = SSM_GROUPS * SSM_STATE
MEM_HEADS = 4
MEM_HEAD_DIM = 256
LANES = 128
N_STRIPS = SSM_CH // LANES
SCAN_ROWS = 128
S5_WINDOW = 8
S5_POW_ROWS = 16
VMEM_LIMIT = 48 * 1024 * 1024
VMEM_LIMIT_FFN = 56 * 1024 * 1024

_NT = (((1,), (1,)), ((), ()))


def _cparams(n_axes, vmem_limit=VMEM_LIMIT):
    return pltpu.CompilerParams(
        dimension_semantics=("arbitrary",) * n_axes, vmem_limit_bytes=vmem_limit)


def _rms(x, g):
    ms = jnp.mean(x * x, axis=-1, keepdims=True)
    return x * lax.rsqrt(ms + EPS) * g


def _gelu(x):
    c = math.sqrt(2.0 / math.pi)
    return x * (0.5 * (1.0 + jnp.tanh(c * (x + 0.044715 * (x * x * x)))))


def _dot(a, b):
    return jnp.dot(a, b, preferred_element_type=F32)


def _full(shape):
    n = len(shape)
    return pl.BlockSpec(shape, lambda *_: (0,) * n)


def _norm_proj_kernel(x_ref, g_ref, *refs, out_kinds):
    n_w = len(out_kinds)
    w_refs, out_refs = refs[:n_w], refs[n_w:]
    h = _rms(x_ref[...], g_ref[...]).astype(BF16)
    oi = 0
    for w_ref, (want_f32, want_bf16, scale) in zip(w_refs, out_kinds):
        r = _dot(h, w_ref[...])
        if scale != 1.0:
            r = r * scale
        if want_f32:
            out_refs[oi][...] = r
            oi += 1
        if want_bf16:
            out_refs[oi][...] = r.astype(BF16)
            oi += 1


def _norm_proj(x, g, ws, out_kinds, tm):
    m, d = x.shape
    out_shapes, out_specs = [], []
    for w, (want_f32, want_bf16, _) in zip(ws, out_kinds):
        n = w.shape[1]
        for want, dt in ((want_f32, F32), (want_bf16, BF16)):
            if want:
                out_shapes.append(jax.ShapeDtypeStruct((m, n), dt))
                out_specs.append(pl.BlockSpec((tm, n), lambda i: (i, 0)))
    return pl.pallas_call(
        functools.partial(_norm_proj_kernel, out_kinds=tuple(out_kinds)),
        grid=(m // tm,),
        in_specs=[pl.BlockSpec((tm, d), lambda i: (i, 0)), _full(g.shape)]
        + [_full(w.shape) for w in ws],
        out_specs=out_specs,
        out_shape=out_shapes,
        compiler_params=_cparams(1),
        name="norm_proj",
    )(x, g, *ws)


def _in_proj_prompt_kernel(x_ref, g_ref, wq_ref, wkt_ref, wv_ref, wvt_ref, wu_ref,
                           q_ref, kt_ref, ktb_ref, vb_ref, vt_ref, u_ref, *, q_scale):
    h = _rms(x_ref[0], g_ref[...]).astype(BF16)
    q_ref[0] = (_dot(h, wq_ref[...]) * q_scale).astype(BF16)
    kt = lax.dot_general(wkt_ref[...], h, _NT, preferred_element_type=F32)
    kt_ref[0] = kt
    ktb_ref[0] = kt.astype(BF16)
    vb_ref[0] = _dot(h, wv_ref[...]).astype(BF16)
    vt_ref[0] = lax.dot_general(wvt_ref[...], h, _NT, preferred_element_type=F32)
    u_ref[0] = _dot(h, wu_ref[...])


def _in_proj_prompt(x, g, wq, wk, wv, wu, q_scale, tm):
    b, t, d = x.shape
    w = wq.shape[1]
    tok = pl.BlockSpec((1, tm, w), lambda bi, i: (bi, i, 0))
    tr = pl.BlockSpec((1, w, tm), lambda bi, i: (bi, 0, i))
    ws = (wq, wk.T, wv, wv.T, wu)
    return pl.pallas_call(
        functools.partial(_in_proj_prompt_kernel, q_scale=q_scale),
        grid=(b, t // tm),
        in_specs=[pl.BlockSpec((1, tm, d), lambda bi, i: (bi, i, 0)), _full(g.shape)]
        + [_full(a.shape) for a in ws],
        out_specs=[tok, tr, tr, tok, tr, tok],
        out_shape=[
            jax.ShapeDtypeStruct((b, t, w), BF16),
            jax.ShapeDtypeStruct((b, w, t), F32),
            jax.ShapeDtypeStruct((b, w, t), BF16),
            jax.ShapeDtypeStruct((b, t, w), BF16),
            jax.ShapeDtypeStruct((b, w, t), F32),
            jax.ShapeDtypeStruct((b, t, w), F32),
        ],
        compiler_params=_cparams(2),
        name="in_proj_prompt",
    )(x, g, *ws)


SB_SUB = 256
LOG2E = 1.4426950408889634


def _softplus_and_logsig(z):
    e = jnp.exp2(jnp.abs(z) * -LOG2E)
    sp = jnp.maximum(z, 0.0) + jnp.log(1.0 + e)
    return sp, z - sp


def _bf16_pieces(x, n):
    out = []
    for _ in range(n):
        p = x.astype(BF16).astype(F32)
        out.append(p)
        x = x - p
    return out


def _sb_prompt_kernel(bias_ref, q_ref, k_ref, v_ref, o_ref, acc_ref, r_ref, *, tq, groups):
    i = pl.program_id(2)
    lane = lax.broadcasted_iota(jnp.int32, (tq, LANES), 1)
    left = lane < SB_HEAD_DIM
    nsub = tq // SB_SUB
    qqs = []
    for g in range(groups):
        q = q_ref[0, :, g * LANES:(g + 1) * LANES].astype(F32)
        bias_rows = jnp.concatenate([jnp.broadcast_to(bias_ref[g, 0:1, :], (tq, LANES)),
                                     jnp.broadcast_to(bias_ref[g, 1:2, :], (tq, LANES))], axis=0)
        qq = jnp.concatenate([jnp.where(left, q, 0.0), jnp.where(left, 0.0, q)], axis=0)
        qqs.append(jnp.concatenate([qq, bias_rows], axis=1).astype(BF16))
    k_ones = jnp.ones((LANES, tq), BF16)
    rr = lax.broadcasted_iota(jnp.int32, (SB_SUB, SB_SUB), 0)
    cc = lax.broadcasted_iota(jnp.int32, (SB_SUB, SB_SUB), 1)
    later = jnp.where(rr > cc, 1.0, 0.0).astype(BF16)
    qrow = lax.broadcasted_iota(jnp.int32, (tq, SB_SUB), 0)
    kcol = lax.broadcasted_iota(jnp.int32, (tq, SB_SUB), 1)
    acc_ref[...] = jnp.zeros_like(acc_ref)
    r_ref[...] = jnp.zeros_like(r_ref)

    def both(fn, x):
        return jnp.concatenate([fn(x[:tq]), fn(x[tq:])], axis=0)

    def run(j, masked):
        off = pl.multiple_of(j * tq, tq)
        for g in range(groups):
            gl = slice(g * LANES, (g + 1) * LANES)
            kb = jnp.concatenate([k_ref[0, gl, pl.ds(off, tq)], k_ones], axis=0)
            vb = v_ref[0, pl.ds(off, tq), gl]
            z = _dot(qqs[g], kb)
            sp, lsig = _softplus_and_logsig(z)
            r = r_ref[g]
            parts = [None] * nsub
            for n in reversed(range(nsub)):
                sl = slice(n * SB_SUB, (n + 1) * SB_SUB)
                spn = sp[:, sl]
                if masked:
                    causal = (kcol + n * SB_SUB) < qrow
                    spn = both(lambda x: jnp.where(causal, x, 0.0), spn)
                cs = _dot(spn.astype(BF16), later)
                a = jnp.exp((lsig[:, sl] - jnp.concatenate([r] * (SB_SUB // LANES), axis=1)) - cs)
                if masked:
                    a = both(lambda x: jnp.where(causal, x, 0.0), a)
                parts[n] = a.astype(BF16)
                r = r + (cs[:, 0:1] + spn[:, 0:1])
            r_ref[g] = r
            pv = _dot(jnp.concatenate(parts, axis=1), vb)
            acc_ref[:, gl] += jnp.where(left, pv[:tq], pv[tq:])

    run(i, True)

    def body(jj, carry):
        run(i - jj, False)
        return carry

    lax.fori_loop(1, i + 1, body, 0)
    o_ref[0] = acc_ref[...]


def _sb_prompt(q, k, v, bias, tq=2 * SB_SUB, groups=4):
    b, t, w = q.shape
    gw = groups * LANES
    pieces = jnp.stack(_bf16_pieces(bias, 3), axis=-1)
    bias = jnp.pad(pieces, ((0, 0), (0, LANES - 3))).reshape(w // LANES, 2, LANES)
    return pl.pallas_call(
        functools.partial(_sb_prompt_kernel, tq=tq, groups=groups),
        grid=(b, w // gw, t // tq),
        in_specs=[
            pl.BlockSpec((groups, 2, LANES), lambda bi, h, i: (h, 0, 0)),
            pl.BlockSpec((1, tq, gw), lambda bi, h, i: (bi, i, h)),
            pl.BlockSpec((1, gw, t), lambda bi, h, i: (bi, h, 0)),
            pl.BlockSpec((1, t, gw), lambda bi, h, i: (bi, 0, h)),
        ],
        out_specs=pl.BlockSpec((1, tq, gw), lambda bi, h, i: (bi, i, h)),
        out_shape=jax.ShapeDtypeStruct((b, t, w), F32),
        scratch_shapes=[pltpu.VMEM((tq, gw), F32), pltpu.VMEM((groups, 2 * tq, LANES), F32)],
        compiler_params=_cparams(3),
        name="sb_prompt",
    )(bias, q, k, v)


def _sb_decode_one(qt, bias, k_pages, v_pages, n_pages, page):
    nh, dh = SB_HEADS, SB_HEAD_DIM
    past = n_pages * page
    qb = [jnp.broadcast_to(qt[:, h:h + 1], (dh, page)) for h in range(nh)]
    zpages = []
    for p in range(n_pages):
        rows = [jnp.sum(k_pages[p, h] * qb[h], axis=0, keepdims=True) for h in range(nh)]
        zpages.append(jnp.concatenate(rows, axis=0))
    z = jnp.concatenate(zpages, axis=1) + jnp.concatenate([bias] * n_pages, axis=1)
    sp, lsig = _softplus_and_logsig(z)

    lane = lax.broadcasted_iota(jnp.int32, (nh, past), 1)
    incl = sp
    step = 1
    while step < past:
        incl = incl + jnp.where(lane < past - step, pltpu.roll(incl, past - step, 1), 0.0)
        step *= 2
    a = jnp.exp(lsig - (incl - sp))

    accs = [jnp.zeros((dh, page), F32) for _ in range(nh)]
    for p in range(n_pages):
        for h in range(nh):
            arow = jnp.broadcast_to(a[h:h + 1, p * page:(p + 1) * page], (dh, page))
            accs[h] = accs[h] + arow * v_pages[p, h]
    return jnp.concatenate([jnp.sum(acc, axis=1, keepdims=True) for acc in accs], axis=1)


def _s5_prep_kernel(lre_ref, lim_ref, ldt_ref, btr_ref, bti_ref, pre_ref, pim_ref, bbr_ref, bbi_ref):
    lre, lim = lre_ref[...], lim_ref[...]
    dt = jnp.exp(ldt_ref[...])
    mag = jnp.exp(lre * dt)
    br = mag * jnp.cos(lim * dt)
    bi = mag * jnp.sin(lim * dt)
    den = lre * lre + lim * lim
    nr, ni = br - 1.0, bi
    cr = (nr * lre + ni * lim) / den
    ci = (ni * lre - nr * lim) / den
    btr, bti = btr_ref[...], bti_ref[...]
    for k in range(S5_WINDOW):
        bbr_ref[k * SSM_GROUP:(k + 1) * SSM_GROUP, :] = cr * btr - ci * bti
        bbi_ref[k * SSM_GROUP:(k + 1) * SSM_GROUP, :] = cr * bti + ci * btr
        cr, ci = cr * br - ci * bi, cr * bi + ci * br
    pr, pi_ = br, bi
    for k in range(S5_WINDOW):
        pre_ref[k:k + 1, :] = pr
        pim_ref[k:k + 1, :] = pi_
        if k + 1 < S5_WINDOW:
            pr, pi_ = pr * br - pi_ * bi, pr * bi + pi_ * br
    for k in range(S5_WINDOW, S5_POW_ROWS):
        pr, pi_ = pr * pr - pi_ * pi_, 2.0 * pr * pi_
        pre_ref[k:k + 1, :] = pr
        pim_ref[k:k + 1, :] = pi_


def _s5_prepare(lam_re, lam_im, log_dt, b_re, b_im, c_re, c_im):
    ch = SSM_CH
    lre = lam_re.reshape(1, ch)
    lim = lam_im.reshape(1, ch)
    ldt = jnp.repeat(log_dt, SSM_STATE).reshape(1, ch)
    btr = b_re.reshape(ch, SSM_GROUP).T
    bti = b_im.reshape(ch, SSM_GROUP).T
    pre, pim, bbr, bbi = pl.pallas_call(
        _s5_prep_kernel,
        out_shape=[jax.ShapeDtypeStruct((S5_POW_ROWS, ch), F32)] * 2
        + [jax.ShapeDtypeStruct((S5_WINDOW * SSM_GROUP, ch), F32)] * 2,
        name="s5_prep",
    )(lre, lim, ldt, btr, bti)

    s_idx = np.arange(N_STRIPS)[:, None, None]
    j_idx = np.arange(LANES)[None, :, None]
    c_idx = np.arange(LANES)[None, None, :]
    grp_of_ch = (LANES * (s_idx // 4) + j_idx) // SSM_GROUP
    grp_of_state = (LANES * s_idx + c_idx) // SSM_STATE
    mask = jnp.asarray(grp_of_ch == grp_of_state, F32)

    def b_strips(bb):
        t = bb.reshape(SSM_GROUP, N_STRIPS, LANES).transpose(1, 0, 2)
        return jnp.tile(t, (1, LANES // SSM_GROUP, 1)) * mask

    def b_windows(bb):
        per_k = [b_strips(bb[k * SSM_GROUP:(k + 1) * SSM_GROUP]) for k in range(S5_WINDOW)]
        return jnp.concatenate(per_k, axis=1)

    wb = jnp.concatenate([b_windows(bbr), b_windows(bbi)], axis=2).astype(BF16)

    def c_strips(c):
        t = c.transpose(0, 2, 1).reshape(N_STRIPS, LANES, SSM_GROUP)
        return jnp.tile(t, (1, 1, LANES // SSM_GROUP)) * mask.transpose(0, 2, 1)

    wc = jnp.concatenate([c_strips(c_re), -c_strips(c_im)], axis=1).astype(BF16)

    def pw(p):
        return p.reshape(S5_POW_ROWS, N_STRIPS, LANES).transpose(1, 0, 2)

    lamp = jnp.concatenate([pw(pre), pw(pim)], axis=2)
    nb = SSM_WIDTH // LANES
    wb = wb.reshape(nb, 4, S5_WINDOW * LANES, 2 * LANES).transpose(0, 2, 1, 3)
    wb = wb.reshape(nb, S5_WINDOW * LANES, 8 * LANES)
    wc = wc.reshape(nb, 8 * LANES, LANES)
    return wb, wc, lamp


def _scan_rows(xr, xi, lam):
    s, k = S5_WINDOW, S5_WINDOW - 1
    while s < SCAN_ROWS:
        ar, ai = lam[k:k + 1, :LANES], lam[k:k + 1, LANES:]
        pr, pi_ = xr[:-s], xi[:-s]
        nr = xr[s:] + ar * pr - ai * pi_
        ni = xi[s:] + ar * pi_ + ai * pr
        xr = jnp.concatenate([xr[:s], nr], axis=0)
        xi = jnp.concatenate([xi[:s], ni], axis=0)
        s, k = 2 * s, k + 1
    return xr, xi


def _glu_out(y, wglu_ref, bglu_ref):
    y = _gelu(y)
    return y * jax.nn.sigmoid(_dot(y.astype(BF16), wglu_ref[...]) + bglu_ref[...])


def _s5_prompt_kernel(u_ref, wb_ref, wc_ref, lamp_ref, dskip_ref, wglu_ref, bglu_ref,
                      o_ref, hre_ref, him_ref, carry_ref, *, chunk):
    c = pl.program_id(1)

    @pl.when(c == 0)
    def _():
        carry_ref[...] = jnp.zeros_like(carry_ref)

    u = u_ref[0]
    seg_pos = lax.broadcasted_iota(jnp.int32, (chunk, LANES), 0) % SCAN_ROWS
    y_blocks = []
    for kb in range(SSM_WIDTH // LANES):
        ukb = u[:, kb * LANES:(kb + 1) * LANES]
        lagged = [ukb] + [jnp.where(seg_pos >= k, pltpu.roll(ukb, k, 0), 0.0)
                          for k in range(1, S5_WINDOW)]
        uwin = jnp.concatenate(lagged, axis=1).astype(BF16)
        res4 = _dot(uwin, wb_ref[kb])
        states = []
        for j in range(4):
            s = 4 * kb + j
            res = res4[:, 2 * j * LANES:2 * (j + 1) * LANES]
            lam = lamp_ref[s]
            prev = carry_ref[s]
            cr, ci = prev[7:8, :LANES], prev[7:8, LANES:]
            lr, li = lam[:S5_WINDOW, :LANES], lam[:S5_WINDOW, LANES:]
            parts = []
            for h in range(chunk // SCAN_ROWS):
                rs = slice(h * SCAN_ROWS, (h + 1) * SCAN_ROWS)
                xr, xi = res[rs, :LANES], res[rs, LANES:]
                xr = jnp.concatenate([xr[:S5_WINDOW] + (lr * cr - li * ci), xr[S5_WINDOW:]], axis=0)
                xi = jnp.concatenate([xi[:S5_WINDOW] + (lr * ci + li * cr), xi[S5_WINDOW:]], axis=0)
                xr, xi = _scan_rows(xr, xi, lam)
                cr, ci = xr[SCAN_ROWS - 1:], xi[SCAN_ROWS - 1:]
                parts.append(jnp.concatenate([xr, xi], axis=1).astype(BF16))
            carry_ref[s] = jnp.concatenate([xr[SCAN_ROWS - 8:], xi[SCAN_ROWS - 8:]], axis=1)
            states.append(jnp.concatenate(parts, axis=0))
        y_blocks.append(_dot(jnp.concatenate(states, axis=1), wc_ref[kb]))
    y = jnp.concatenate(y_blocks, axis=1) + dskip_ref[...] * u
    o_ref[0] = _glu_out(y, wglu_ref, bglu_ref)

    @pl.when(c == pl.num_programs(1) - 1)
    def _():
        for s in range(N_STRIPS):
            last = carry_ref[s]
            hre_ref[0, :, s * LANES:(s + 1) * LANES] = last[7:8, :LANES]
            him_ref[0, :, s * LANES:(s + 1) * LANES] = last[7:8, LANES:]


def _s5_prompt(u, wb, wc, lamp, dskip, wglu, bglu, chunk=256):
    b, t, w = u.shape
    ch = SSM_CH
    return pl.pallas_call(
        functools.partial(_s5_prompt_kernel, chunk=chunk),
        grid=(b, t // chunk),
        in_specs=[pl.BlockSpec((1, chunk, w), lambda bi, c: (bi, c, 0))]
        + [_full(a.shape) for a in (wb, wc, lamp, dskip, wglu, bglu)],
        out_specs=[
            pl.BlockSpec((1, chunk, w), lambda bi, c: (bi, c, 0)),
            pl.BlockSpec((1, 1, ch), lambda bi, c: (bi, 0, 0)),
            pl.BlockSpec((1, 1, ch), lambda bi, c: (bi, 0, 0)),
        ],
        out_shape=[
            jax.ShapeDtypeStruct((b, t, w), F32),
            jax.ShapeDtypeStruct((b, 1, ch), F32),
            jax.ShapeDtypeStruct((b, 1, ch), F32),
        ],
        scratch_shapes=[pltpu.VMEM((N_STRIPS, 8, 2 * LANES), F32)],
        compiler_params=_cparams(2),
        name="s5_prompt",
    )(u, wb, wc, lamp, dskip, wglu, bglu)


def _s5_step_kernel(u_ref, h0r_ref, h0i_ref, wb_ref, wc_ref, lamp_ref, dskip_ref, wglu_ref,
                    bglu_ref, o_ref, hre_ref, him_ref):
    u = u_ref[...]
    ub = u.astype(BF16)
    y_blocks = []
    for kb in range(SSM_WIDTH // LANES):
        res4 = _dot(ub[:, kb * LANES:(kb + 1) * LANES], wb_ref[kb, 0:LANES, :])
        states = []
        for j in range(4):
            s = 4 * kb + j
            sl = slice(s * LANES, (s + 1) * LANES)
            res = res4[:, 2 * j * LANES:2 * (j + 1) * LANES]
            lam = lamp_ref[s]
            lr, li = lam[0:1, :LANES], lam[0:1, LANES:]
            h0r, h0i = h0r_ref[:, sl], h0i_ref[:, sl]
            xr = res[:, :LANES] + lr * h0r - li * h0i
            xi = res[:, LANES:] + lr * h0i + li * h0r
            hre_ref[:, sl] = xr
            him_ref[:, sl] = xi
            states.append(jnp.concatenate([xr, xi], axis=1).astype(BF16))
        y_blocks.append(_dot(jnp.concatenate(states, axis=1), wc_ref[kb]))
    y = jnp.concatenate(y_blocks, axis=1) + dskip_ref[...] * u
    o_ref[...] = _glu_out(y, wglu_ref, bglu_ref)


def _s5_step(u, h0r, h0i, wb, wc, lamp, dskip, wglu, bglu):
    n, w = u.shape
    return pl.pallas_call(
        _s5_step_kernel,
        out_shape=[
            jax.ShapeDtypeStruct((n, w), F32),
            jax.ShapeDtypeStruct((n, SSM_CH), F32),
            jax.ShapeDtypeStruct((n, SSM_CH), F32),
        ],
        compiler_params=pltpu.CompilerParams(vmem_limit_bytes=VMEM_LIMIT),
        name="s5_step",
    )(u, h0r, h0i, wb, wc, lamp, dskip, wglu, bglu)


def _mix_out_kernel(x_ref, a_ref, b_ref, ga_ref, gb_ref, wa_ref, wb_ref, o_ref):
    ha = _rms(a_ref[...], ga_ref[...]).astype(BF16)
    hb = _rms(b_ref[...], gb_ref[...]).astype(BF16)
    o_ref[...] = x_ref[...] + _dot(ha, wa_ref[...]) + _dot(hb, wb_ref[...])


def _mix_out(x, a, b, ga, gb, wa, wb, tm):
    m, d = x.shape
    w = a.shape[1]
    return pl.pallas_call(
        _mix_out_kernel,
        grid=(m // tm,),
        in_specs=[
            pl.BlockSpec((tm, d), lambda i: (i, 0)),
            pl.BlockSpec((tm, w), lambda i: (i, 0)),
            pl.BlockSpec((tm, w), lambda i: (i, 0)),
        ] + [_full(t.shape) for t in (ga, gb, wa, wb)],
        out_specs=pl.BlockSpec((tm, d), lambda i: (i, 0)),
        out_shape=jax.ShapeDtypeStruct((m, d), F32),
        compiler_params=_cparams(1),
        name="mix_out",
    )(x, a, b, ga, gb, wa, wb)


def _mem_prompt_kernel(x0_ref, a_ref, b_ref, ga_ref, gb_ref, wa_ref, wb_ref,
                       g_ref, wq_ref, mk_ref, mv_ref, wo_ref, o_ref, ob_ref):
    ha = _rms(a_ref[0], ga_ref[...]).astype(BF16)
    hb = _rms(b_ref[0], gb_ref[...]).astype(BF16)
    x = x0_ref[0] + _dot(ha, wa_ref[...]) + _dot(hb, wb_ref[...])
    q = _dot(_rms(x, g_ref[...]).astype(BF16), wq_ref[...])
    qb = (q * (1.0 / math.sqrt(MEM_HEAD_DIM))).astype(BF16)
    for h in range(MEM_HEADS):
        sl = slice(h * MEM_HEAD_DIM, (h + 1) * MEM_HEAD_DIM)
        s = lax.dot_general(qb[:, sl], mk_ref[0, :, sl], _NT, preferred_element_type=F32)
        e = jnp.exp(s - jnp.max(s, axis=-1, keepdims=True))
        o = _dot(e.astype(BF16), mv_ref[0, :, sl]) / jnp.sum(e, axis=-1, keepdims=True)
        ob_ref[:, sl] = o.astype(BF16)
    o_ref[0] = x + _dot(ob_ref[...], wo_ref[...])


def _mem_prompt(x, a, b_, ga, gb, wa, wb, g, wq, mk, mv, wo, tm):
    b, t, d = x.shape
    w = a.shape[2]
    nk = mk.shape[1]
    tok = lambda width: pl.BlockSpec((1, tm, width), lambda bi, i: (bi, i, 0))
    mem = pl.BlockSpec((1, nk, d), lambda bi, i: (bi, 0, 0))
    return pl.pallas_call(
        _mem_prompt_kernel,
        grid=(b, t // tm),
        in_specs=[tok(d), tok(w), tok(w)] + [_full(p.shape) for p in (ga, gb, wa, wb, g, wq)]
        + [mem, mem, _full(wo.shape)],
        out_specs=tok(d),
        out_shape=jax.ShapeDtypeStruct((b, t, d), F32),
        scratch_shapes=[pltpu.VMEM((tm, d), BF16)],
        compiler_params=_cparams(2),
        name="mem_prompt",
    )(x, a, b_, ga, gb, wa, wb, g, wq, mk, mv, wo)


MEM_SEQS_PER_STEP = 4


def _mem_decode_kernel(q_ref, k_ref, v_ref, o_ref, *, n_mem):
    def tree(fn, xs):
        while len(xs) > 1:
            xs = [fn(xs[j], xs[j + 1]) for j in range(0, len(xs) - 1, 2)] + xs[len(xs) & ~1:]
        return xs[0]

    for i in range(MEM_SEQS_PER_STEP):
        q = q_ref[i]
        s = [jnp.sum(k_ref[0, i, t] * q, axis=-1, keepdims=True) for t in range(n_mem)]
        m = tree(jnp.maximum, s)
        e = [jnp.exp(st - m) for st in s]
        denom = tree(jnp.add, e)
        acc = tree(jnp.add, [e[t] * v_ref[0, i, t] for t in range(n_mem)])
        o_ref[i] = acc / denom


def _mem_decode(q, mem_k, mem_v, layer):
    _, n, nk, nh, dh = mem_k.shape
    ns = MEM_SEQS_PER_STEP
    kv_spec = pl.BlockSpec((1, ns, nk, nh, dh), lambda b: (layer, b, 0, 0, 0))
    return pl.pallas_call(
        functools.partial(_mem_decode_kernel, n_mem=nk),
        grid=(n // ns,),
        in_specs=[pl.BlockSpec((ns, nh, dh), lambda b: (b, 0, 0)), kv_spec, kv_spec],
        out_specs=pl.BlockSpec((ns, nh, dh), lambda b: (b, 0, 0)),
        out_shape=jax.ShapeDtypeStruct((n, nh, dh), F32),
        compiler_params=_cparams(1),
        name="mem_decode",
    )(q, mem_k, mem_v)


def _proj_residual_kernel(x_ref, a_ref, w_ref, o_ref):
    o_ref[...] = x_ref[...] + _dot(a_ref[...].astype(BF16), w_ref[...])


def _proj_residual(x, a, w):
    return pl.pallas_call(
        _proj_residual_kernel,
        out_shape=jax.ShapeDtypeStruct(x.shape, F32),
        compiler_params=pltpu.CompilerParams(vmem_limit_bytes=VMEM_LIMIT),
        name="proj_residual",
    )(x, a, w)


FF_CHUNK = 256


DEC_PER_STEP = 4


def _ffn_prompt_kernel(pt_ref, x_ref, g_ref, wg_ref, wu_ref, cw_ref, cb_ref, wd_ref, gf_ref,
                       qd_ref, bd_ref, kc_ref, vc_ref,
                       y_ref, cs_ref, od_ref, act_ref, carry_ref, kbuf, vbuf, sem, *,
                       tm, layer, n_pages, page):
    step = pl.program_id(0) * pl.num_programs(1) + pl.program_id(1)
    n_steps = pl.num_programs(0) * pl.num_programs(1)

    def start_fetch(seq, slot):
        for p in range(n_pages):
            pg = pt_ref[seq * n_pages + p]
            pltpu.make_async_copy(kc_ref.at[layer, pg], kbuf.at[slot, p], sem.at[0, slot]).start()
            pltpu.make_async_copy(vc_ref.at[layer, pg], vbuf.at[slot, p], sem.at[1, slot]).start()

    def wait_fetch(slot):
        for p in range(n_pages):
            pltpu.make_async_copy(kc_ref.at[layer, 0], kbuf.at[slot, p], sem.at[0, slot]).wait()
            pltpu.make_async_copy(vc_ref.at[layer, 0], vbuf.at[slot, p], sem.at[1, slot]).wait()

    @pl.when(step == 0)
    def _():
        start_fetch(0, 0)
        start_fetch(1, 1)

    @pl.when(pl.program_id(1) == 0)
    def _():
        carry_ref[...] = jnp.zeros_like(carry_ref)

    x = x_ref[0]
    h = _rms(x, g_ref[...]).astype(BF16)
    row = lax.broadcasted_iota(jnp.int32, (tm, FF_CHUNK), 0)
    d_ff = wg_ref.shape[1]
    n_chunks = d_ff // FF_CHUNK

    def ff_chunk(c):
        sl = slice(c * FF_CHUNK, (c + 1) * FF_CHUNK)
        g = _dot(h, wg_ref[:, sl])
        up = _dot(h, wu_ref[:, sl])
        prev = carry_ref[:, sl]
        p1, p2 = prev[7:8], prev[6:7]
        g1 = jnp.where(row == 0, p1, pltpu.roll(g, 1, 0))
        g2 = jnp.where(row == 0, p2, jnp.where(row == 1, p1, pltpu.roll(g, 2, 0)))
        conv = cb_ref[:, sl] + cw_ref[0:1, sl] * g2 + cw_ref[1:2, sl] * g1 + cw_ref[2:3, sl] * g
        act_ref[:, sl] = (_gelu(conv) * up).astype(BF16)
        carry_ref[:, sl] = g[tm - 8:]
        cs_ref[0, :, sl] = g[tm - 2:]

    per_phase = -(-n_chunks // (DEC_PER_STEP - 1)) if DEC_PER_STEP > 1 else n_chunks
    for u in range(DEC_PER_STEP):
        slot = u % 2
        wait_fetch(slot)
        od_ref[u] = _sb_decode_one(qd_ref[u], bd_ref[...], kbuf.at[slot], vbuf.at[slot],
                                   n_pages, page)
        for c in range(u * per_phase, min((u + 1) * per_phase, n_chunks)):
            ff_chunk(c)
        if u == DEC_PER_STEP - 1:
            x3 = x + _dot(act_ref[...], wd_ref[...])
            y_ref[0] = _rms(x3, gf_ref[...])
        if u + 2 < DEC_PER_STEP:
            start_fetch(step * DEC_PER_STEP + u + 2, slot)
        else:
            @pl.when(step + 1 < n_steps)
            def _():
                start_fetch((step + 1) * DEC_PER_STEP + u + 2 - DEC_PER_STEP, slot)


def _ffn_prompt(x, g, wg, wu, cw, cb, wd, gf, q_dec, bias_dec, cache_k, cache_v, page_table,
                layer, tm):
    b, t, d = x.shape
    d_ff = wg.shape[1]
    n_seq, n_pages = page_table.shape
    page = cache_k.shape[2]
    nh, dh = SB_HEADS, SB_HEAD_DIM
    n_tiles = t // tm
    assert n_seq == b * n_tiles * DEC_PER_STEP and DEC_PER_STEP % 2 == 0
    kc = jnp.transpose(cache_k, (0, 1, 3, 4, 2))
    vc = jnp.transpose(cache_v, (0, 1, 3, 4, 2))
    qt = jnp.transpose(q_dec.reshape(n_seq, nh, dh), (0, 2, 1))
    bias_lanes = jnp.broadcast_to(bias_dec[:, None], (nh, page))
    dec = pl.BlockSpec((DEC_PER_STEP, dh, nh), lambda bi, i, pt: (bi * n_tiles + i, 0, 0))
    grid_spec = pltpu.PrefetchScalarGridSpec(
        num_scalar_prefetch=1,
        grid=(b, n_tiles),
        in_specs=[pl.BlockSpec((1, tm, d), lambda bi, i, pt: (bi, i, 0))]
        + [_full(a.shape) for a in (g, wg, wu, cw, cb, wd, gf)]
        + [dec, _full(bias_lanes.shape),
           pl.BlockSpec(memory_space=pl.ANY), pl.BlockSpec(memory_space=pl.ANY)],
        out_specs=[
            pl.BlockSpec((1, tm, d), lambda bi, i, pt: (bi, i, 0)),
            pl.BlockSpec((1, 2, d_ff), lambda bi, i, pt: (bi, 0, 0)),
            dec,
        ],
        scratch_shapes=[
            pltpu.VMEM((tm, d_ff), BF16),
            pltpu.VMEM((8, d_ff), F32),
            pltpu.VMEM((2, n_pages, nh, dh, page), F32),
            pltpu.VMEM((2, n_pages, nh, dh, page), F32),
            pltpu.SemaphoreType.DMA((2, 2)),
        ],
    )
    y, cs, od = pl.pallas_call(
        functools.partial(_ffn_prompt_kernel, tm=tm, layer=layer, n_pages=n_pages, page=page),
        grid_spec=grid_spec,
        out_shape=[
            jax.ShapeDtypeStruct((b, t, d), F32),
            jax.ShapeDtypeStruct((b, 2, d_ff), F32),
            jax.ShapeDtypeStruct((n_seq, dh, nh), F32),
        ],
        compiler_params=_cparams(2, VMEM_LIMIT_FFN),
        name="ffn_prompt",
    )(page_table.reshape(-1), x, g, wg, wu, cw, cb, wd, gf, qt, bias_lanes, kc, vc)
    return y, cs, jnp.transpose(od, (0, 2, 1)).reshape(n_seq, nh * dh)


def _ffn_step_kernel(x_ref, g_ref, wg_ref, wu_ref, cw_ref, cb_ref, wd_ref, gf_ref, p0_ref, p1_ref,
                     y_ref, gate_ref, act_ref):
    x = x_ref[...]
    h = _rms(x, g_ref[...]).astype(BF16)
    d_ff = wg_ref.shape[1]
    for c in range(d_ff // FF_CHUNK):
        sl = slice(c * FF_CHUNK, (c + 1) * FF_CHUNK)
        g = _dot(h, wg_ref[:, sl])
        up = _dot(h, wu_ref[:, sl])
        conv = (cb_ref[:, sl] + cw_ref[0:1, sl] * p0_ref[:, sl] + cw_ref[1:2, sl] * p1_ref[:, sl]
                + cw_ref[2:3, sl] * g)
        act_ref[:, sl] = (_gelu(conv) * up).astype(BF16)
        gate_ref[:, sl] = g
    x3 = x + _dot(act_ref[...], wd_ref[...])
    y_ref[...] = _rms(x3, gf_ref[...])


def _ffn_step(x, g, wg, wu, cw, cb, wd, gf, p0, p1):
    n, d = x.shape
    d_ff = wg.shape[1]
    return pl.pallas_call(
        _ffn_step_kernel,
        out_shape=[jax.ShapeDtypeStruct((n, d), F32), jax.ShapeDtypeStruct((n, d_ff), F32)],
        scratch_shapes=[pltpu.VMEM((n, d_ff), BF16)],
        compiler_params=pltpu.CompilerParams(vmem_limit_bytes=VMEM_LIMIT),
        name="ffn_step",
    )(x, g, wg, wu, cw, cb, wd, gf, p0, p1)


def kernel(x_prompt, x_sample, cache_sb_k, cache_sb_v, page_table, state_ssm_re, state_ssm_im, state_conv, cache_mem_k, cache_mem_v, mem_prompt, g_mix, w_in, sb_bias, lam_re, lam_im, log_dt, b_re, b_im, c_re, c_im, d_skip, w_glu, b_glu, g_sb_out, g_ssm_out, w_out, g_mem_q, g_mem_kv, w_mq, w_mk, w_mv, w_mo, g_ffn, w_gate, w_up, conv_w, conv_b, w_down, g_final):
    depth = w_in.shape[0]
    n_p, t_p, d = x_prompt.shape
    n_s = x_sample.shape[0]
    assert x_sample.shape[1] == 1
    tm = 512
    q_scale = 1.0 / math.sqrt(SB_HEAD_DIM)
    row = lambda a: a.reshape(1, -1)
    gf = row(g_final)

    yp = x_prompt.reshape(n_p * t_p, d)
    ys = x_sample.reshape(n_s, d)
    outs = {k: [] for k in ("pk", "pv", "pre", "pim", "pconv", "pmk", "pmv",
                            "sk", "sv", "sre", "sim", "sconv")}
    y_prompt = y_sample = None
    for l in range(depth):
        w_in_b = w_in[l].astype(BF16)
        w_q, w_k, w_v, w_u = (w_in_b[:, j * SB_WIDTH:(j + 1) * SB_WIDTH] for j in range(4))
        w_out_b = w_out[l].astype(BF16)
        wo_a, wo_b = w_out_b[:SB_WIDTH], w_out_b[SB_WIDTH:]
        wglu_b = w_glu[l].astype(BF16)
        wmq, wmk, wmv, wmo = (w[l].astype(BF16) for w in (w_mq, w_mk, w_mv, w_mo))
        wg, wu, wd = (w[l].astype(BF16) for w in (w_gate, w_up, w_down))
        wb, wc, lamp = _s5_prepare(lam_re[l], lam_im[l], log_dt[l], b_re[l], b_im[l], c_re[l], c_im[l])
        s5_w = (wb, wc, lamp, row(d_skip[l]), wglu_b, row(b_glu[l]))
        ffn_w = (row(g_ffn[l]), wg, wu, conv_w[l], row(conv_b[l]), wd, gf)

        q_b, kt_f, kt_b, v_b, vt_f, u_f = _in_proj_prompt(
            yp.reshape(n_p, t_p, d), row(g_mix[l]), w_q, w_k, w_v, w_u, q_scale, tm)
        o_sb = _sb_prompt(q_b, kt_b, v_b, sb_bias[l])
        o_ssm, hre, him = _s5_prompt(u_f, *s5_w)
        n_mem = mem_prompt.shape[1]
        mk_f, mk_b, mv_f, mv_b = _norm_proj(
            mem_prompt.reshape(n_p * n_mem, d), row(g_mem_kv[l]), [wmk, wmv],
            [(True, True, 1.0), (True, True, 1.0)], n_mem)
        x2 = _mem_prompt(yp.reshape(n_p, t_p, d), o_sb, o_ssm, row(g_sb_out[l]), row(g_ssm_out[l]),
                         wo_a, wo_b, row(g_mem_q[l]), wmq,
                         mk_b.reshape(n_p, n_mem, d), mv_b.reshape(n_p, n_mem, d), wmo, tm)
        qs_f, ks_f, vs_f, us_f = _norm_proj(
            ys, row(g_mix[l]), [w_q, w_k, w_v, w_u],
            [(True, False, q_scale), (True, False, 1.0), (True, False, 1.0), (True, False, 1.0)], n_s)
        y3, cs_p, os_sb = _ffn_prompt(x2, *ffn_w, qs_f, sb_bias[l], cache_sb_k, cache_sb_v,
                                      page_table, l, tm)
        if l + 1 < depth:
            raise NotImplementedError("final norm is fused into the last layer's FFN")
        y_prompt = y3
        to_cache = lambda a: jnp.transpose(a.reshape(n_p, SB_HEADS, SB_HEAD_DIM, t_p), (0, 3, 1, 2))
        outs["pk"].append(to_cache(kt_f))
        outs["pv"].append(to_cache(vt_f))
        outs["pre"].append(hre.reshape(n_p, SSM_GROUPS, SSM_STATE))
        outs["pim"].append(him.reshape(n_p, SSM_GROUPS, SSM_STATE))
        outs["pconv"].append(cs_p)
        outs["pmk"].append(mk_f.reshape(n_p, n_mem, MEM_HEADS, MEM_HEAD_DIM))
        outs["pmv"].append(mv_f.reshape(n_p, n_mem, MEM_HEADS, MEM_HEAD_DIM))

        os_ssm, hsr, hsi = _s5_step(us_f, state_ssm_re[l].reshape(n_s, SSM_CH),
                                    state_ssm_im[l].reshape(n_s, SSM_CH), *s5_w)
        x1s = _mix_out(ys, os_sb, os_ssm, row(g_sb_out[l]), row(g_ssm_out[l]), wo_a, wo_b, n_s)
        (qm_f,) = _norm_proj(x1s, row(g_mem_q[l]), [wmq],
                             [(True, False, 1.0 / math.sqrt(MEM_HEAD_DIM))], n_s)
        om = _mem_decode(qm_f.reshape(n_s, MEM_HEADS, MEM_HEAD_DIM), cache_mem_k, cache_mem_v,
                         l).reshape(n_s, d)
        x2s = _proj_residual(x1s, om, wmo)
        y3s, gate_s = _ffn_step(x2s, *ffn_w, state_conv[l][:, 0], state_conv[l][:, 1])
        y_sample = y3s
        outs["sk"].append(ks_f.reshape(n_s, 1, SB_HEADS, SB_HEAD_DIM))
        outs["sv"].append(vs_f.reshape(n_s, 1, SB_HEADS, SB_HEAD_DIM))
        outs["sre"].append(hsr.reshape(n_s, SSM_GROUPS, SSM_STATE))
        outs["sim"].append(hsi.reshape(n_s, SSM_GROUPS, SSM_STATE))
        outs["sconv"].append(jnp.stack([state_conv[l][:, 1], gate_s], axis=1))

    st = lambda k: jnp.stack(outs[k])
    return (y_prompt, y_sample.reshape(n_s, 1, d),
            st("pk"), st("pv"), st("pre"), st("pim"), st("pconv"), st("pmk"), st("pmv"),
            st("sk"), st("sv"), st("sre"), st("sim"), st("sconv"))
```

```python
import functools
import math

import numpy as np
import jax
import jax.numpy as jnp
from jax import lax
from jax.experimental import pallas as pl
from jax.experimental.pallas import tpu as pltpu

F32 = jnp.float32
BF16 = jnp.bfloat16

EPS = 1e-6
SB_HEADS = 8
SB_HEAD_DIM = 64
SB_WIDTH = SB_HEADS * SB_HEAD_DIM
SSM_GROUPS = 32
SSM_GROUP = 16
SSM_STATE = 64
SSM_WIDTH = SSM_GROUPS * SSM_GROUP
SSM_CH = SSM_GROUPS * SSM_STATE
MEM_HEADS = 4
MEM_HEAD_DIM = 256
LANES = 128
N_STRIPS = SSM_CH // LANES
SCAN_ROWS = 128
S5_WINDOW = 8
S5_POW_ROWS = 16
VMEM_LIMIT = 48 * 1024 * 1024
VMEM_LIMIT_FFN = 56 * 1024 * 1024

_NT = (((1,), (1,)), ((), ()))


def _cparams(n_axes, vmem_limit=VMEM_LIMIT):
    return pltpu.CompilerParams(
        dimension_semantics=("arbitrary",) * n_axes, vmem_limit_bytes=vmem_limit)


def _rms(x, g):
    ms = jnp.mean(x * x, axis=-1, keepdims=True)
    return x * lax.rsqrt(ms + EPS) * g


def _gelu(x):
    c = math.sqrt(2.0 / math.pi)
    return x * (0.5 * (1.0 + jnp.tanh(c * (x + 0.044715 * (x * x * x)))))


def _dot(a, b):
    return jnp.dot(a, b, preferred_element_type=F32)


def _full(shape):
    n = len(shape)
    return pl.BlockSpec(shape, lambda *_: (0,) * n)


def _norm_proj_kernel(x_ref, g_ref, *refs, out_kinds):
    n_w = len(out_kinds)
    w_refs, out_refs = refs[:n_w], refs[n_w:]
    h = _rms(x_ref[...], g_ref[...]).astype(BF16)
    oi = 0
    for w_ref, (want_f32, want_bf16, scale) in zip(w_refs, out_kinds):
        r = _dot(h, w_ref[...])
        if scale != 1.0:
            r = r * scale
        if want_f32:
            out_refs[oi][...] = r
            oi += 1
        if want_bf16:
            out_refs[oi][...] = r.astype(BF16)
            oi += 1


def _norm_proj(x, g, ws, out_kinds, tm):
    m, d = x.shape
    out_shapes, out_specs = [], []
    for w, (want_f32, want_bf16, _) in zip(ws, out_kinds):
        n = w.shape[1]
        for want, dt in ((want_f32, F32), (want_bf16, BF16)):
            if want:
                out_shapes.append(jax.ShapeDtypeStruct((m, n), dt))
                out_specs.append(pl.BlockSpec((tm, n), lambda i: (i, 0)))
    return pl.pallas_call(
        functools.partial(_norm_proj_kernel, out_kinds=tuple(out_kinds)),
        grid=(m // tm,),
        in_specs=[pl.BlockSpec((tm, d), lambda i: (i, 0)), _full(g.shape)]
        + [_full(w.shape) for w in ws],
        out_specs=out_specs,
        out_shape=out_shapes,
        compiler_params=_cparams(1),
        name="norm_proj",
    )(x, g, *ws)


def _in_proj_prompt_kernel(x_ref, g_ref, wq_ref, wkt_ref, wv_ref, wvt_ref, wu_ref,
                           q_ref, kt_ref, ktb_ref, vb_ref, vt_ref, u_ref, *, q_scale):
    h = _rms(x_ref[0], g_ref[...]).astype(BF16)
    q_ref[0] = (_dot(h, wq_ref[...]) * q_scale).astype(BF16)
    kt = lax.dot_general(wkt_ref[...], h, _NT, preferred_element_type=F32)
    kt_ref[0] = kt
    ktb_ref[0] = kt.astype(BF16)
    vb_ref[0] = _dot(h, wv_ref[...]).astype(BF16)
    vt_ref[0] = lax.dot_general(wvt_ref[...], h, _NT, preferred_element_type=F32)
    u_ref[0] = _dot(h, wu_ref[...])


def _in_proj_prompt(x, g, wq, wk, wv, wu, q_scale, tm):
    b, t, d = x.shape
    w = wq.shape[1]
    tok = pl.BlockSpec((1, tm, w), lambda bi, i: (bi, i, 0))
    tr = pl.BlockSpec((1, w, tm), lambda bi, i: (bi, 0, i))
    ws = (wq, wk.T, wv, wv.T, wu)
    return pl.pallas_call(
        functools.partial(_in_proj_prompt_kernel, q_scale=q_scale),
        grid=(b, t // tm),
        in_specs=[pl.BlockSpec((1, tm, d), lambda bi, i: (bi, i, 0)), _full(g.shape)]
        + [_full(a.shape) for a in ws],
        out_specs=[tok, tr, tr, tok, tr, tok],
        out_shape=[
            jax.ShapeDtypeStruct((b, t, w), BF16),
            jax.ShapeDtypeStruct((b, w, t), F32),
            jax.ShapeDtypeStruct((b, w, t), BF16),
            jax.ShapeDtypeStruct((b, t, w), BF16),
            jax.ShapeDtypeStruct((b, w, t), F32),
            jax.ShapeDtypeStruct((b, t, w), F32),
        ],
        compiler_params=_cparams(2),
        name="in_proj_prompt",
    )(x, g, *ws)


SB_SUB = 256
LOG2E = 1.4426950408889634


def _softplus_and_logsig(z):
    e = jnp.exp2(jnp.abs(z) * -LOG2E)
    sp = jnp.maximum(z, 0.0) + jnp.log(1.0 + e)
    return sp, z - sp


def _bf16_pieces(x, n):
    out = []
    for _ in range(n):
        p = x.astype(BF16).astype(F32)
        out.append(p)
        x = x - p
    return out


def _sb_prompt_kernel(bias_ref, q_ref, k_ref, v_ref, o_ref, acc_ref, r_ref, *, tq, groups):
    i = pl.program_id(2)
    lane = lax.broadcasted_iota(jnp.int32, (tq, LANES), 1)
    left = lane < SB_HEAD_DIM
    nsub = tq // SB_SUB
    qqs = []
    for g in range(groups):
        q = q_ref[0, :, g * LANES:(g + 1) * LANES].astype(F32)
        bias_rows = jnp.concatenate([jnp.broadcast_to(bias_ref[g, 0:1, :], (tq, LANES)),
                                     jnp.broadcast_to(bias_ref[g, 1:2, :], (tq, LANES))], axis=0)
        qq = jnp.concatenate([jnp.where(left, q, 0.0), jnp.where(left, 0.0, q)], axis=0)
        qqs.append(jnp.concatenate([qq, bias_rows], axis=1).astype(BF16))
    k_ones = jnp.ones((LANES, tq), BF16)
    rr = lax.broadcasted_iota(jnp.int32, (SB_SUB, SB_SUB), 0)
    cc = lax.broadcasted_iota(jnp.int32, (SB_SUB, SB_SUB), 1)
    later = jnp.where(rr > cc, 1.0, 0.0).astype(BF16)
    qrow = lax.broadcasted_iota(jnp.int32, (tq, SB_SUB), 0)
    kcol = lax.broadcasted_iota(jnp.int32, (tq, SB_SUB), 1)
    acc_ref[...] = jnp.zeros_like(acc_ref)
    r_ref[...] = jnp.zeros_like(r_ref)

    def both(fn, x):
        return jnp.concatenate([fn(x[:tq]), fn(x[tq:])], axis=0)

    def run(j, masked):
        off = pl.multiple_of(j * tq, tq)
        for g in range(groups):
            gl = slice(g * LANES, (g + 1) * LANES)
            kb = jnp.concatenate([k_ref[0, gl, pl.ds(off, tq)], k_ones], axis=0)
            vb = v_ref[0, pl.ds(off, tq), gl]
            z = _dot(qqs[g], kb)
            sp, lsig = _softplus_and_logsig(z)
            r = r_ref[g]
            parts = [None] * nsub
            for n in reversed(range(nsub)):
                sl = slice(n * SB_SUB, (n + 1) * SB_SUB)
                spn = sp[:, sl]
                if masked:
                    causal = (kcol + n * SB_SUB) < qrow
                    spn = both(lambda x: jnp.where(causal, x, 0.0), spn)
                cs = _dot(spn.astype(BF16), later)
                a = jnp.exp((lsig[:, sl] - jnp.concatenate([r] * (SB_SUB // LANES), axis=1)) - cs)
                if masked:
                    a = both(lambda x: jnp.where(causal, x, 0.0), a)
                parts[n] = a.astype(BF16)
                r = r + (cs[:, 0:1] + spn[:, 0:1])
            r_ref[g] = r
            pv = _dot(jnp.concatenate(parts, axis=1), vb)
            acc_ref[:, gl] += jnp.where(left, pv[:tq], pv[tq:])

    run(i, True)

    def body(jj, carry):
        run(i - jj, False)
        return carry

    lax.fori_loop(1, i + 1, body, 0)
    o_ref[0] = acc_ref[...]


def _sb_prompt(q, k, v, bias, tq=2 * SB_SUB, groups=4):
    b, t, w = q.shape
    gw = groups * LANES
    pieces = jnp.stack(_bf16_pieces(bias, 3), axis=-1)
    bias = jnp.pad(pieces, ((0, 0), (0, LANES - 3))).reshape(w // LANES, 2, LANES)
    return pl.pallas_call(
        functools.partial(_sb_prompt_kernel, tq=tq, groups=groups),
        grid=(b, w // gw, t // tq),
        in_specs=[
            pl.BlockSpec((groups, 2, LANES), lambda bi, h, i: (h, 0, 0)),
            pl.BlockSpec((1, tq, gw), lambda bi, h, i: (bi, i, h)),
            pl.BlockSpec((1, gw, t), lambda bi, h, i: (bi, h, 0)),
            pl.BlockSpec((1, t, gw), lambda bi, h, i: (bi, 0, h)),
        ],
        out_specs=pl.BlockSpec((1, tq, gw), lambda bi, h, i: (bi, i, h)),
        out_shape=jax.ShapeDtypeStruct((b, t, w), F32),
        scratch_shapes=[pltpu.VMEM((tq, gw), F32), pltpu.VMEM((groups, 2 * tq, LANES), F32)],
        compiler_params=_cparams(3),
        name="sb_prompt",
    )(bias, q, k, v)


def _sb_decode_one(qt, bias, k_pages, v_pages, n_pages, page):
    nh, dh = SB_HEADS, SB_HEAD_DIM
    past = n_pages * page
    qb = [jnp.broadcast_to(qt[:, h:h + 1], (dh, page)) for h in range(nh)]
    zpages = []
    for p in range(n_pages):
        rows = [jnp.sum(k_pages[p, h] * qb[h], axis=0, keepdims=True) for h in range(nh)]
        zpages.append(jnp.concatenate(rows, axis=0))
    z = jnp.concatenate(zpages, axis=1) + jnp.concatenate([bias] * n_pages, axis=1)
    sp, lsig = _softplus_and_logsig(z)

    lane = lax.broadcasted_iota(jnp.int32, (nh, past), 1)
    incl = sp
    step = 1
    while step < past:
        incl = incl + jnp.where(lane < past - step, pltpu.roll(incl, past - step, 1), 0.0)
        step *= 2
    a = jnp.exp(lsig - (incl - sp))

    accs = [jnp.zeros((dh, page), F32) for _ in range(nh)]
    for p in range(n_pages):
        for h in range(nh):
            arow = jnp.broadcast_to(a[h:h + 1, p * page:(p + 1) * page], (dh, page))
            accs[h] = accs[h] + arow * v_pages[p, h]
    return jnp.concatenate([jnp.sum(acc, axis=1, keepdims=True) for acc in accs], axis=1)


def _s5_prep_kernel(lre_ref, lim_ref, ldt_ref, btr_ref, bti_ref, pre_ref, pim_ref, bbr_ref, bbi_ref):
    lre, lim = lre_ref[...], lim_ref[...]
    dt = jnp.exp(ldt_ref[...])
    mag = jnp.exp(lre * dt)
    br = mag * jnp.cos(lim * dt)
    bi = mag * jnp.sin(lim * dt)
    den = lre * lre + lim * lim
    nr, ni = br - 1.0, bi
    cr = (nr * lre + ni * lim) / den
    ci = (ni * lre - nr * lim) / den
    btr, bti = btr_ref[...], bti_ref[...]
    for k in range(S5_WINDOW):
        bbr_ref[k * SSM_GROUP:(k + 1) * SSM_GROUP, :] = cr * btr - ci * bti
        bbi_ref[k * SSM_GROUP:(k + 1) * SSM_GROUP, :] = cr * bti + ci * btr
        cr, ci = cr * br - ci * bi, cr * bi + ci * br
    pr, pi_ = br, bi
    for k in range(S5_WINDOW):
        pre_ref[k:k + 1, :] = pr
        pim_ref[k:k + 1, :] = pi_
        if k + 1 < S5_WINDOW:
            pr, pi_ = pr * br - pi_ * bi, pr * bi + pi_ * br
    for k in range(S5_WINDOW, S5_POW_ROWS):
        pr, pi_ = pr * pr - pi_ * pi_, 2.0 * pr * pi_
        pre_ref[k:k + 1, :] = pr
        pim_ref[k:k + 1, :] = pi_


def _s5_prepare(lam_re, lam_im, log_dt, b_re, b_im, c_re, c_im):
    ch = SSM_CH
    lre = lam_re.reshape(1, ch)
    lim = lam_im.reshape(1, ch)
    ldt = jnp.repeat(log_dt, SSM_STATE).reshape(1, ch)
    btr = b_re.reshape(ch, SSM_GROUP).T
    bti = b_im.reshape(ch, SSM_GROUP).T
    pre, pim, bbr, bbi = pl.pallas_call(
        _s5_prep_kernel,
        out_shape=[jax.ShapeDtypeStruct((S5_POW_ROWS, ch), F32)] * 2
        + [jax.ShapeDtypeStruct((S5_WINDOW * SSM_GROUP, ch), F32)] * 2,
        name="s5_prep",
    )(lre, lim, ldt, btr, bti)

    s_idx = np.arange(N_STRIPS)[:, None, None]
    j_idx = np.arange(LANES)[None, :, None]
    c_idx = np.arange(LANES)[None, None, :]
    grp_of_ch = (LANES * (s_idx // 4) + j_idx) // SSM_GROUP
    grp_of_state = (LANES * s_idx + c_idx) // SSM_STATE
    mask = jnp.asarray(grp_of_ch == grp_of_state, F32)

    def b_strips(bb):
        t = bb.reshape(SSM_GROUP, N_STRIPS, LANES).transpose(1, 0, 2)
        return jnp.tile(t, (1, LANES // SSM_GROUP, 1)) * mask

    def b_windows(bb):
        per_k = [b_strips(bb[k * SSM_GROUP:(k + 1) * SSM_GROUP]) for k in range(S5_WINDOW)]
        return jnp.concatenate(per_k, axis=1)

    wb = jnp.concatenate([b_windows(bbr), b_windows(bbi)], axis=2).astype(BF16)

    def c_strips(c):
        t = c.transpose(0, 2, 1).reshape(N_STRIPS, LANES, SSM_GROUP)
        return jnp.tile(t, (1, 1, LANES // SSM_GROUP)) * mask.transpose(0, 2, 1)

    wc = jnp.concatenate([c_strips(c_re), -c_strips(c_im)], axis=1).astype(BF16)

    def pw(p):
        return p.reshape(S5_POW_ROWS, N_STRIPS, LANES).transpose(1, 0, 2)

    lamp = jnp.concatenate([pw(pre), pw(pim)], axis=2)
    nb = SSM_WIDTH // LANES
    wb = wb.reshape(nb, 4, S5_WINDOW * LANES, 2 * LANES).transpose(0, 2, 1, 3)
    wb = wb.reshape(nb, S5_WINDOW * LANES, 8 * LANES)
    wc = wc.reshape(nb, 8 * LANES, LANES)
    return wb, wc, lamp


def _scan_rows(xr, xi, lam):
    s, k = S5_WINDOW, S5_WINDOW - 1
    while s < SCAN_ROWS:
        ar, ai = lam[k:k + 1, :LANES], lam[k:k + 1, LANES:]
        pr, pi_ = xr[:-s], xi[:-s]
        nr = xr[s:] + ar * pr - ai * pi_
        ni = xi[s:] + ar * pi_ + ai * pr
        xr = jnp.concatenate([xr[:s], nr], axis=0)
        xi = jnp.concatenate([xi[:s], ni], axis=0)
        s, k = 2 * s, k + 1
    return xr, xi


def _glu_out(y, wglu_ref, bglu_ref):
    y = _gelu(y)
    return y * jax.nn.sigmoid(_dot(y.astype(BF16), wglu_ref[...]) + bglu_ref[...])


def _s5_prompt_kernel(u_ref, wb_ref, wc_ref, lamp_ref, dskip_ref, wglu_ref, bglu_ref,
                      o_ref, hre_ref, him_ref, carry_ref, *, chunk):
    c = pl.program_id(1)

    @pl.when(c == 0)
    def _():
        carry_ref[...] = jnp.zeros_like(carry_ref)

    u = u_ref[0]
    seg_pos = lax.broadcasted_iota(jnp.int32, (chunk, LANES), 0) % SCAN_ROWS
    y_blocks = []
    for kb in range(SSM_WIDTH // LANES):
        ukb = u[:, kb * LANES:(kb + 1) * LANES]
        lagged = [ukb] + [jnp.where(seg_pos >= k, pltpu.roll(ukb, k, 0), 0.0)
                          for k in range(1, S5_WINDOW)]
        uwin = jnp.concatenate(lagged, axis=1).astype(BF16)
        res4 = _dot(uwin, wb_ref[kb])
        states = []
        for j in range(4):
            s = 4 * kb + j
            res = res4[:, 2 * j * LANES:2 * (j + 1) * LANES]
            lam = lamp_ref[s]
            prev = carry_ref[s]
            cr, ci = prev[7:8, :LANES], prev[7:8, LANES:]
            lr, li = lam[:S5_WINDOW, :LANES], lam[:S5_WINDOW, LANES:]
            parts = []
            for h in range(chunk // SCAN_ROWS):
                rs = slice(h * SCAN_ROWS, (h + 1) * SCAN_ROWS)
                xr, xi = res[rs, :LANES], res[rs, LANES:]
                xr = jnp.concatenate([xr[:S5_WINDOW] + (lr * cr - li * ci), xr[S5_WINDOW:]], axis=0)
                xi = jnp.concatenate([xi[:S5_WINDOW] + (lr * ci + li * cr), xi[S5_WINDOW:]], axis=0)
                xr, xi = _scan_rows(xr, xi, lam)
                cr, ci = xr[SCAN_ROWS - 1:], xi[SCAN_ROWS - 1:]
                parts.append(jnp.concatenate([xr, xi], axis=1).astype(BF16))
            carry_ref[s] = jnp.concatenate([xr[SCAN_ROWS - 8:], xi[SCAN_ROWS - 8:]], axis=1)
            states.append(jnp.concatenate(parts, axis=0))
        y_blocks.append(_dot(jnp.concatenate(states, axis=1), wc_ref[kb]))
    y = jnp.concatenate(y_blocks, axis=1) + dskip_ref[...] * u
    o_ref[0] = _glu_out(y, wglu_ref, bglu_ref)

    @pl.when(c == pl.num_programs(1) - 1)
    def _():
        for s in range(N_STRIPS):
            last = carry_ref[s]
            hre_ref[0, :, s * LANES:(s + 1) * LANES] = last[7:8, :LANES]
            him_ref[0, :, s * LANES:(s + 1) * LANES] = last[7:8, LANES:]


def _s5_prompt(u, wb, wc, lamp, dskip, wglu, bglu, chunk=512):
    b, t, w = u.shape
    ch = SSM_CH
    return pl.pallas_call(
        functools.partial(_s5_prompt_kernel, chunk=chunk),
        grid=(b, t // chunk),
        in_specs=[pl.BlockSpec((1, chunk, w), lambda bi, c: (bi, c, 0))]
        + [_full(a.shape) for a in (wb, wc, lamp, dskip, wglu, bglu)],
        out_specs=[
            pl.BlockSpec((1, chunk, w), lambda bi, c: (bi, c, 0)),
            pl.BlockSpec((1, 1, ch), lambda bi, c: (bi, 0, 0)),
            pl.BlockSpec((1, 1, ch), lambda bi, c: (bi, 0, 0)),
        ],
        out_shape=[
            jax.ShapeDtypeStruct((b, t, w), F32),
            jax.ShapeDtypeStruct((b, 1, ch), F32),
            jax.ShapeDtypeStruct((b, 1, ch), F32),
        ],
        scratch_shapes=[pltpu.VMEM((N_STRIPS, 8, 2 * LANES), F32)],
        compiler_params=_cparams(2),
        name="s5_prompt",
    )(u, wb, wc, lamp, dskip, wglu, bglu)


def _s5_step_kernel(u_ref, h0r_ref, h0i_ref, wb_ref, wc_ref, lamp_ref, dskip_ref, wglu_ref,
                    bglu_ref, o_ref, hre_ref, him_ref):
    u = u_ref[...]
    ub = u.astype(BF16)
    y_blocks = []
    for kb in range(SSM_WIDTH // LANES):
        res4 = _dot(ub[:, kb * LANES:(kb + 1) * LANES], wb_ref[kb, 0:LANES, :])
        states = []
        for j in range(4):
            s = 4 * kb + j
            sl = slice(s * LANES, (s + 1) * LANES)
            res = res4[:, 2 * j * LANES:2 * (j + 1) * LANES]
            lam = lamp_ref[s]
            lr, li = lam[0:1, :LANES], lam[0:1, LANES:]
            h0r, h0i = h0r_ref[:, sl], h0i_ref[:, sl]
            xr = res[:, :LANES] + lr * h0r - li * h0i
            xi = res[:, LANES:] + lr * h0i + li * h0r
            hre_ref[:, sl] = xr
            him_ref[:, sl] = xi
            states.append(jnp.concatenate([xr, xi], axis=1).astype(BF16))
        y_blocks.append(_dot(jnp.concatenate(states, axis=1), wc_ref[kb]))
    y = jnp.concatenate(y_blocks, axis=1) + dskip_ref[...] * u
    o_ref[...] = _glu_out(y, wglu_ref, bglu_ref)


def _s5_step(u, h0r, h0i, wb, wc, lamp, dskip, wglu, bglu):
    n, w = u.shape
    return pl.pallas_call(
        _s5_step_kernel,
        out_shape=[
            jax.ShapeDtypeStruct((n, w), F32),
            jax.ShapeDtypeStruct((n, SSM_CH), F32),
            jax.ShapeDtypeStruct((n, SSM_CH), F32),
        ],
        compiler_params=pltpu.CompilerParams(vmem_limit_bytes=VMEM_LIMIT),
        name="s5_step",
    )(u, h0r, h0i, wb, wc, lamp, dskip, wglu, bglu)


def _mix_out_kernel(x_ref, a_ref, b_ref, ga_ref, gb_ref, wa_ref, wb_ref, o_ref):
    ha = _rms(a_ref[...], ga_ref[...]).astype(BF16)
    hb = _rms(b_ref[...], gb_ref[...]).astype(BF16)
    o_ref[...] = x_ref[...] + _dot(ha, wa_ref[...]) + _dot(hb, wb_ref[...])


def _mix_out(x, a, b, ga, gb, wa, wb, tm):
    m, d = x.shape
    w = a.shape[1]
    return pl.pallas_call(
        _mix_out_kernel,
        grid=(m // tm,),
        in_specs=[
            pl.BlockSpec((tm, d), lambda i: (i, 0)),
            pl.BlockSpec((tm, w), lambda i: (i, 0)),
            pl.BlockSpec((tm, w), lambda i: (i, 0)),
        ] + [_full(t.shape) for t in (ga, gb, wa, wb)],
        out_specs=pl.BlockSpec((tm, d), lambda i: (i, 0)),
        out_shape=jax.ShapeDtypeStruct((m, d), F32),
        compiler_params=_cparams(1),
        name="mix_out",
    )(x, a, b, ga, gb, wa, wb)


def _mem_prompt_kernel(x0_ref, a_ref, b_ref, ga_ref, gb_ref, wa_ref, wb_ref,
                       g_ref, wq_ref, mk_ref, mv_ref, wo_ref, o_ref, ob_ref):
    ha = _rms(a_ref[0], ga_ref[...]).astype(BF16)
    hb = _rms(b_ref[0], gb_ref[...]).astype(BF16)
    x = x0_ref[0] + _dot(ha, wa_ref[...]) + _dot(hb, wb_ref[...])
    q = _dot(_rms(x, g_ref[...]).astype(BF16), wq_ref[...])
    qb = (q * (1.0 / math.sqrt(MEM_HEAD_DIM))).astype(BF16)
    for h in range(MEM_HEADS):
        sl = slice(h * MEM_HEAD_DIM, (h + 1) * MEM_HEAD_DIM)
        s = lax.dot_general(qb[:, sl], mk_ref[0, :, sl], _NT, preferred_element_type=F32)
        e = jnp.exp(s - jnp.max(s, axis=-1, keepdims=True))
        o = _dot(e.astype(BF16), mv_ref[0, :, sl]) / jnp.sum(e, axis=-1, keepdims=True)
        ob_ref[:, sl] = o.astype(BF16)
    o_ref[0] = x + _dot(ob_ref[...], wo_ref[...])


def _mem_prompt(x, a, b_, ga, gb, wa, wb, g, wq, mk, mv, wo, tm):
    b, t, d = x.shape
    w = a.shape[2]
    nk = mk.shape[1]
    tok = lambda width: pl.BlockSpec((1, tm, width), lambda bi, i: (bi, i, 0))
    mem = pl.BlockSpec((1, nk, d), lambda bi, i: (bi, 0, 0))
    return pl.pallas_call(
        _mem_prompt_kernel,
        grid=(b, t // tm),
        in_specs=[tok(d), tok(w), tok(w)] + [_full(p.shape) for p in (ga, gb, wa, wb, g, wq)]
        + [mem, mem, _full(wo.shape)],
        out_specs=tok(d),
        out_shape=jax.ShapeDtypeStruct((b, t, d), F32),
        scratch_shapes=[pltpu.VMEM((tm, d), BF16)],
        compiler_params=_cparams(2),
        name="mem_prompt",
    )(x, a, b_, ga, gb, wa, wb, g, wq, mk, mv, wo)


MEM_SEQS_PER_STEP = 4


def _mem_decode_kernel(q_ref, k_ref, v_ref, o_ref, *, n_mem):
    def tree(fn, xs):
        while len(xs) > 1:
            xs = [fn(xs[j], xs[j + 1]) for j in range(0, len(xs) - 1, 2)] + xs[len(xs) & ~1:]
        return xs[0]

    for i in range(MEM_SEQS_PER_STEP):
        q = q_ref[i]
        s = [jnp.sum(k_ref[0, i, t] * q, axis=-1, keepdims=True) for t in range(n_mem)]
        m = tree(jnp.maximum, s)
        e = [jnp.exp(st - m) for st in s]
        denom = tree(jnp.add, e)
        acc = tree(jnp.add, [e[t] * v_ref[0, i, t] for t in range(n_mem)])
        o_ref[i] = acc / denom


def _mem_decode(q, mem_k, mem_v, layer):
    _, n, nk, nh, dh = mem_k.shape
    ns = MEM_SEQS_PER_STEP
    kv_spec = pl.BlockSpec((1, ns, nk, nh, dh), lambda b: (layer, b, 0, 0, 0))
    return pl.pallas_call(
        functools.partial(_mem_decode_kernel, n_mem=nk),
        grid=(n // ns,),
        in_specs=[pl.BlockSpec((ns, nh, dh), lambda b: (b, 0, 0)), kv_spec, kv_spec],
        out_specs=pl.BlockSpec((ns, nh, dh), lambda b: (b, 0, 0)),
        out_shape=jax.ShapeDtypeStruct((n, nh, dh), F32),
        compiler_params=_cparams(1),
        name="mem_decode",
    )(q, mem_k, mem_v)


def _proj_residual_kernel(x_ref, a_ref, w_ref, o_ref):
    o_ref[...] = x_ref[...] + _dot(a_ref[...].astype(BF16), w_ref[...])


def _proj_residual(x, a, w):
    return pl.pallas_call(
        _proj_residual_kernel,
        out_shape=jax.ShapeDtypeStruct(x.shape, F32),
        compiler_params=pltpu.CompilerParams(vmem_limit_bytes=VMEM_LIMIT),
        name="proj_residual",
    )(x, a, w)


FF_CHUNK = 256


DEC_PER_STEP = 4


def _ffn_prompt_kernel(pt_ref, x_ref, g_ref, wg_ref, wu_ref, cw_ref, cb_ref, wd_ref, gf_ref,
                       qd_ref, bd_ref, kc_ref, vc_ref,
                       y_ref, cs_ref, od_ref, act_ref, carry_ref, kbuf, vbuf, sem, *,
                       tm, layer, n_pages, page):
    step = pl.program_id(0) * pl.num_programs(1) + pl.program_id(1)
    n_steps = pl.num_programs(0) * pl.num_programs(1)

    def start_fetch(seq, slot):
        for p in range(n_pages):
            pg = pt_ref[seq * n_pages + p]
            pltpu.make_async_copy(kc_ref.at[layer, pg], kbuf.at[slot, p], sem.at[0, slot]).start()
            pltpu.make_async_copy(vc_ref.at[layer, pg], vbuf.at[slot, p], sem.at[1, slot]).start()

    def wait_fetch(slot):
        for p in range(n_pages):
            pltpu.make_async_copy(kc_ref.at[layer, 0], kbuf.at[slot, p], sem.at[0, slot]).wait()
            pltpu.make_async_copy(vc_ref.at[layer, 0], vbuf.at[slot, p], sem.at[1, slot]).wait()

    @pl.when(step == 0)
    def _():
        start_fetch(0, 0)
        start_fetch(1, 1)

    @pl.when(pl.program_id(1) == 0)
    def _():
        carry_ref[...] = jnp.zeros_like(carry_ref)

    x = x_ref[0]
    h = _rms(x, g_ref[...]).astype(BF16)
    row = lax.broadcasted_iota(jnp.int32, (8, FF_CHUNK), 0)
    d_ff = wg_ref.shape[1]
    n_chunks = d_ff // FF_CHUNK

    def ff_chunk(c):
        sl = slice(c * FF_CHUNK, (c + 1) * FF_CHUNK)
        g = _dot(h, wg_ref[:, sl])
        up = _dot(h, wu_ref[:, sl])
        prev = carry_ref[:, sl]
        p1, p2 = prev[7:8], prev[6:7]
        r1, r2 = pltpu.roll(g, 1, 0), pltpu.roll(g, 2, 0)
        g1 = jnp.concatenate([jnp.where(row == 0, p1, r1[:8]), r1[8:]], axis=0)
        g2 = jnp.concatenate(
            [jnp.where(row == 0, p2, jnp.where(row == 1, p1, r2[:8])), r2[8:]], axis=0)
        conv = cb_ref[:, sl] + cw_ref[0:1, sl] * g2 + cw_ref[1:2, sl] * g1 + cw_ref[2:3, sl] * g
        act_ref[:, sl] = (_gelu(conv) * up).astype(BF16)
        carry_ref[:, sl] = g[tm - 8:]
        cs_ref[0, :, sl] = g[tm - 2:]

    bounds = [(n_chunks * u + DEC_PER_STEP // 2) // DEC_PER_STEP for u in range(DEC_PER_STEP + 1)]
    for u in range(DEC_PER_STEP):
        slot = u % 2
        wait_fetch(slot)
        od_ref[u] = _sb_decode_one(qd_ref[u], bd_ref[...], kbuf.at[slot], vbuf.at[slot],
                                   n_pages, page)
        for c in range(bounds[u], bounds[u + 1]):
            ff_chunk(c)
        if u == DEC_PER_STEP - 1:
            x3 = x + _dot(act_ref[...], wd_ref[...])
            y_ref[0] = _rms(x3, gf_ref[...])
        if u + 2 < DEC_PER_STEP:
            start_fetch(step * DEC_PER_STEP + u + 2, slot)
        else:
            @pl.when(step + 1 < n_steps)
            def _():
                start_fetch((step + 1) * DEC_PER_STEP + u + 2 - DEC_PER_STEP, slot)


def _ffn_prompt(x, g, wg, wu, cw, cb, wd, gf, q_dec, bias_dec, cache_k, cache_v, page_table,
                layer, tm):
    b, t, d = x.shape
    d_ff = wg.shape[1]
    n_seq, n_pages = page_table.shape
    page = cache_k.shape[2]
    nh, dh = SB_HEADS, SB_HEAD_DIM
    n_tiles = t // tm
    assert n_seq == b * n_tiles * DEC_PER_STEP and DEC_PER_STEP % 2 == 0
    kc = jnp.transpose(cache_k, (0, 1, 3, 4, 2))
    vc = jnp.transpose(cache_v, (0, 1, 3, 4, 2))
    qt = jnp.transpose(q_dec.reshape(n_seq, nh, dh), (0, 2, 1))
    bias_lanes = jnp.broadcast_to(bias_dec[:, None], (nh, page))
    dec = pl.BlockSpec((DEC_PER_STEP, dh, nh), lambda bi, i, pt: (bi * n_tiles + i, 0, 0))
    grid_spec = pltpu.PrefetchScalarGridSpec(
        num_scalar_prefetch=1,
        grid=(b, n_tiles),
        in_specs=[pl.BlockSpec((1, tm, d), lambda bi, i, pt: (bi, i, 0))]
        + [_full(a.shape) for a in (g, wg, wu, cw, cb, wd, gf)]
        + [dec, _full(bias_lanes.shape),
           pl.BlockSpec(memory_space=pl.ANY), pl.BlockSpec(memory_space=pl.ANY)],
        out_specs=[
            pl.BlockSpec((1, tm, d), lambda bi, i, pt: (bi, i, 0)),
            pl.BlockSpec((1, 2, d_ff), lambda bi, i, pt: (bi, 0, 0)),
            dec,
        ],
        scratch_shapes=[
            pltpu.VMEM((tm, d_ff), BF16),
            pltpu.VMEM((8, d_ff), F32),
            pltpu.VMEM((2, n_pages, nh, dh, page), F32),
            pltpu.VMEM((2, n_pages, nh, dh, page), F32),
            pltpu.SemaphoreType.DMA((2, 2)),
        ],
    )
    y, cs, od = pl.pallas_call(
        functools.partial(_ffn_prompt_kernel, tm=tm, layer=layer, n_pages=n_pages, page=page),
        grid_spec=grid_spec,
        out_shape=[
            jax.ShapeDtypeStruct((b, t, d), F32),
            jax.ShapeDtypeStruct((b, 2, d_ff), F32),
            jax.ShapeDtypeStruct((n_seq, dh, nh), F32),
        ],
        compiler_params=_cparams(2, VMEM_LIMIT_FFN),
        name="ffn_prompt",
    )(page_table.reshape(-1), x, g, wg, wu, cw, cb, wd, gf, qt, bias_lanes, kc, vc)
    return y, cs, jnp.transpose(od, (0, 2, 1)).reshape(n_seq, nh * dh)


def _ffn_step_kernel(x_ref, g_ref, wg_ref, wu_ref, cw_ref, cb_ref, wd_ref, gf_ref, p0_ref, p1_ref,
                     y_ref, gate_ref, act_ref):
    x = x_ref[...]
    h = _rms(x, g_ref[...]).astype(BF16)
    d_ff = wg_ref.shape[1]
    for c in range(d_ff // FF_CHUNK):
        sl = slice(c * FF_CHUNK, (c + 1) * FF_CHUNK)
        g = _dot(h, wg_ref[:, sl])
        up = _dot(h, wu_ref[:, sl])
        conv = (cb_ref[:, sl] + cw_ref[0:1, sl] * p0_ref[:, sl] + cw_ref[1:2, sl] * p1_ref[:, sl]
                + cw_ref[2:3, sl] * g)
        act_ref[:, sl] = (_gelu(conv) * up).astype(BF16)
        gate_ref[:, sl] = g
    x3 = x + _dot(act_ref[...], wd_ref[...])
    y_ref[...] = _rms(x3, gf_ref[...])


def _ffn_step(x, g, wg, wu, cw, cb, wd, gf, p0, p1):
    n, d = x.shape
    d_ff = wg.shape[1]
    return pl.pallas_call(
        _ffn_step_kernel,
        out_shape=[jax.ShapeDtypeStruct((n, d), F32), jax.ShapeDtypeStruct((n, d_ff), F32)],
        scratch_shapes=[pltpu.VMEM((n, d_ff), BF16)],
        compiler_params=pltpu.CompilerParams(vmem_limit_bytes=VMEM_LIMIT),
        name="ffn_step",
    )(x, g, wg, wu, cw, cb, wd, gf, p0, p1)


def kernel(x_prompt, x_sample, cache_sb_k, cache_sb_v, page_table, state_ssm_re, state_ssm_im, state_conv, cache_mem_k, cache_mem_v, mem_prompt, g_mix, w_in, sb_bias, lam_re, lam_im, log_dt, b_re, b_im, c_re, c_im, d_skip, w_glu, b_glu, g_sb_out, g_ssm_out, w_out, g_mem_q, g_mem_kv, w_mq, w_mk, w_mv, w_mo, g_ffn, w_gate, w_up, conv_w, conv_b, w_down, g_final):
    depth = w_in.shape[0]
    n_p, t_p, d = x_prompt.shape
    n_s = x_sample.shape[0]
    assert x_sample.shape[1] == 1
    tm = 512
    q_scale = 1.0 / math.sqrt(SB_HEAD_DIM)
    row = lambda a: a.reshape(1, -1)
    gf = row(g_final)

    yp = x_prompt.reshape(n_p * t_p, d)
    ys = x_sample.reshape(n_s, d)
    outs = {k: [] for k in ("pk", "pv", "pre", "pim", "pconv", "pmk", "pmv",
                            "sk", "sv", "sre", "sim", "sconv")}
    y_prompt = y_sample = None
    for l in range(depth):
        w_in_b = w_in[l].astype(BF16)
        w_q, w_k, w_v, w_u = (w_in_b[:, j * SB_WIDTH:(j + 1) * SB_WIDTH] for j in range(4))
        w_out_b = w_out[l].astype(BF16)
        wo_a, wo_b = w_out_b[:SB_WIDTH], w_out_b[SB_WIDTH:]
        wglu_b = w_glu[l].astype(BF16)
        wmq, wmk, wmv, wmo = (w[l].astype(BF16) for w in (w_mq, w_mk, w_mv, w_mo))
        wg, wu, wd = (w[l].astype(BF16) for w in (w_gate, w_up, w_down))
        wb, wc, lamp = _s5_prepare(lam_re[l], lam_im[l], log_dt[l], b_re[l], b_im[l], c_re[l], c_im[l])
        s5_w = (wb, wc, lamp, row(d_skip[l]), wglu_b, row(b_glu[l]))
        ffn_w = (row(g_ffn[l]), wg, wu, conv_w[l], row(conv_b[l]), wd, gf)

        q_b, kt_f, kt_b, v_b, vt_f, u_f = _in_proj_prompt(
            yp.reshape(n_p, t_p, d), row(g_mix[l]), w_q, w_k, w_v, w_u, q_scale, tm)
        o_sb = _sb_prompt(q_b, kt_b, v_b, sb_bias[l])
        o_ssm, hre, him = _s5_prompt(u_f, *s5_w)
        n_mem = mem_prompt.shape[1]
        mk_f, mk_b, mv_f, mv_b = _norm_proj(
            mem_prompt.reshape(n_p * n_mem, d), row(g_mem_kv[l]), [wmk, wmv],
            [(True, True, 1.0), (True, True, 1.0)], n_mem)
        x2 = _mem_prompt(yp.reshape(n_p, t_p, d), o_sb, o_ssm, row(g_sb_out[l]), row(g_ssm_out[l]),
                         wo_a, wo_b, row(g_mem_q[l]), wmq,
                         mk_b.reshape(n_p, n_mem, d), mv_b.reshape(n_p, n_mem, d), wmo, tm)
        qs_f, ks_f, vs_f, us_f = _norm_proj(
            ys, row(g_mix[l]), [w_q, w_k, w_v, w_u],
            [(True, False, q_scale), (True, False, 1.0), (True, False, 1.0), (True, False, 1.0)], n_s)
        y3, cs_p, os_sb = _ffn_prompt(x2, *ffn_w, qs_f, sb_bias[l], cache_sb_k, cache_sb_v,
                                      page_table, l, tm)
        if l + 1 < depth:
            raise NotImplementedError("final norm is fused into the last layer's FFN")
        y_prompt = y3
        to_cache = lambda a: jnp.transpose(a.reshape(n_p, SB_HEADS, SB_HEAD_DIM, t_p), (0, 3, 1, 2))
        outs["pk"].append(to_cache(kt_f))
        outs["pv"].append(to_cache(vt_f))
        outs["pre"].append(hre.reshape(n_p, SSM_GROUPS, SSM_STATE))
        outs["pim"].append(him.reshape(n_p, SSM_GROUPS, SSM_STATE))
        outs["pconv"].append(cs_p)
        outs["pmk"].append(mk_f.reshape(n_p, n_mem, MEM_HEADS, MEM_HEAD_DIM))
        outs["pmv"].append(mv_f.reshape(n_p, n_mem, MEM_HEADS, MEM_HEAD_DIM))

        os_ssm, hsr, hsi = _s5_step(us_f, state_ssm_re[l].reshape(n_s, SSM_CH),
                                    state_ssm_im[l].reshape(n_s, SSM_CH), *s5_w)
        x1s = _mix_out(ys, os_sb, os_ssm, row(g_sb_out[l]), row(g_ssm_out[l]), wo_a, wo_b, n_s)
        (qm_f,) = _norm_proj(x1s, row(g_mem_q[l]), [wmq],
                             [(True, False, 1.0 / math.sqrt(MEM_HEAD_DIM))], n_s)
        om = _mem_decode(qm_f.reshape(n_s, MEM_HEADS, MEM_HEAD_DIM), cache_mem_k, cache_mem_v,
                         l).reshape(n_s, d)
        x2s = _proj_residual(x1s, om, wmo)
        y3s, gate_s = _ffn_step(x2s, *ffn_w, state_conv[l][:, 0], state_conv[l][:, 1])
        y_sample = y3s
        outs["sk"].append(ks_f.reshape(n_s, 1, SB_HEADS, SB_HEAD_DIM))
        outs["sv"].append(vs_f.reshape(n_s, 1, SB_HEADS, SB_HEAD_DIM))
        outs["sre"].append(hsr.reshape(n_s, SSM_GROUPS, SSM_STATE))
        outs["sim"].append(hsi.reshape(n_s, SSM_GROUPS, SSM_STATE))
        outs["sconv"].append(jnp.stack([state_conv[l][:, 1], gate_s], axis=1))

    st = lambda k: jnp.stack(outs[k])
    return (y_prompt, y_sample.reshape(n_s, 1, d),
            st("pk"), st("pv"), st("pre"), st("pim"), st("pconv"), st("pmk"), st("pmv"),
            st("sk"), st("sv"), st("sre"), st("sim"), st("sconv"))
```

```python
import functools
import math

import numpy as np
import jax
import jax.numpy as jnp
from jax import lax
from jax.experimental import pallas as pl
from jax.experimental.pallas import tpu as pltpu

F32 = jnp.float32
BF16 = jnp.bfloat16

EPS = 1e-6
SB_HEADS = 8
SB_HEAD_DIM = 64
SB_WIDTH = SB_HEADS * SB_HEAD_DIM
SSM_GROUPS = 32
SSM_GROUP = 16
SSM_STATE = 64
SSM_WIDTH = SSM_GROUPS * SSM_GROUP
SSM_CH = SSM_GROUPS * SSM_STATE
MEM_HEADS = 4
MEM_HEAD_DIM = 256
LANES = 128
N_STRIPS = SSM_CH // LANES
SCAN_ROWS = 128
S5_WINDOW = 8
S5_POW_ROWS = 16
VMEM_LIMIT = 48 * 1024 * 1024
VMEM_LIMIT_FFN = 56 * 1024 * 1024

_NT = (((1,), (1,)), ((), ()))


def _cparams(n_axes, vmem_limit=VMEM_LIMIT):
    return pltpu.CompilerParams(
        dimension_semantics=("arbitrary",) * n_axes, vmem_limit_bytes=vmem_limit)


def _rms(x, g):
    ms = jnp.mean(x * x, axis=-1, keepdims=True)
    return x * lax.rsqrt(ms + EPS) * g


def _gelu(x):
    c = math.sqrt(2.0 / math.pi)
    return x * (0.5 * (1.0 + jnp.tanh(c * (x + 0.044715 * (x * x * x)))))


def _dot(a, b):
    return jnp.dot(a, b, preferred_element_type=F32)


def _full(shape):
    n = len(shape)
    return pl.BlockSpec(shape, lambda *_: (0,) * n)


def _norm_proj_kernel(x_ref, g_ref, *refs, out_kinds):
    n_w = len(out_kinds)
    w_refs, out_refs = refs[:n_w], refs[n_w:]
    h = _rms(x_ref[...], g_ref[...]).astype(BF16)
    oi = 0
    for w_ref, (want_f32, want_bf16, scale) in zip(w_refs, out_kinds):
        r = _dot(h, w_ref[...])
        if scale != 1.0:
            r = r * scale
        if want_f32:
            out_refs[oi][...] = r
            oi += 1
        if want_bf16:
            out_refs[oi][...] = r.astype(BF16)
            oi += 1


def _norm_proj(x, g, ws, out_kinds, tm):
    m, d = x.shape
    out_shapes, out_specs = [], []
    for w, (want_f32, want_bf16, _) in zip(ws, out_kinds):
        n = w.shape[1]
        for want, dt in ((want_f32, F32), (want_bf16, BF16)):
            if want:
                out_shapes.append(jax.ShapeDtypeStruct((m, n), dt))
                out_specs.append(pl.BlockSpec((tm, n), lambda i: (i, 0)))
    return pl.pallas_call(
        functools.partial(_norm_proj_kernel, out_kinds=tuple(out_kinds)),
        grid=(m // tm,),
        in_specs=[pl.BlockSpec((tm, d), lambda i: (i, 0)), _full(g.shape)]
        + [_full(w.shape) for w in ws],
        out_specs=out_specs,
        out_shape=out_shapes,
        compiler_params=_cparams(1),
        name="norm_proj",
    )(x, g, *ws)


def _in_proj_prompt_kernel(x_ref, g_ref, wq_ref, wkt_ref, wv_ref, wvt_ref, wu_ref,
                           q_ref, kt_ref, ktb_ref, vb_ref, vt_ref, u_ref, *, q_scale):
    h = _rms(x_ref[0], g_ref[...]).astype(BF16)
    q_ref[0] = (_dot(h, wq_ref[...]) * q_scale).astype(BF16)
    kt = lax.dot_general(wkt_ref[...], h, _NT, preferred_element_type=F32)
    kt_ref[0] = kt
    ktb_ref[0] = kt.astype(BF16)
    vb_ref[0] = _dot(h, wv_ref[...]).astype(BF16)
    vt_ref[0] = lax.dot_general(wvt_ref[...], h, _NT, preferred_element_type=F32)
    u_ref[0] = _dot(h, wu_ref[...])


def _in_proj_prompt(x, g, wq, wk, wv, wu, q_scale, tm):
    b, t, d = x.shape
    w = wq.shape[1]
    tok = pl.BlockSpec((1, tm, w), lambda bi, i: (bi, i, 0))
    tr = pl.BlockSpec((1, w, tm), lambda bi, i: (bi, 0, i))
    ws = (wq, wk.T, wv, wv.T, wu)
    return pl.pallas_call(
        functools.partial(_in_proj_prompt_kernel, q_scale=q_scale),
        grid=(b, t // tm),
        in_specs=[pl.BlockSpec((1, tm, d), lambda bi, i: (bi, i, 0)), _full(g.shape)]
        + [_full(a.shape) for a in ws],
        out_specs=[tok, tr, tr, tok, tr, tok],
        out_shape=[
            jax.ShapeDtypeStruct((b, t, w), BF16),
            jax.ShapeDtypeStruct((b, w, t), F32),
            jax.ShapeDtypeStruct((b, w, t), BF16),
            jax.ShapeDtypeStruct((b, t, w), BF16),
            jax.ShapeDtypeStruct((b, w, t), F32),
            jax.ShapeDtypeStruct((b, t, w), F32),
        ],
        compiler_params=_cparams(2),
        name="in_proj_prompt",
    )(x, g, *ws)


SB_SUB = 256
LOG2E = 1.4426950408889634


def _softplus_and_logsig(z):
    e = jnp.exp2(jnp.abs(z) * -LOG2E)
    sp = jnp.maximum(z, 0.0) + jnp.log(1.0 + e)
    return sp, z - sp


def _bf16_pieces(x, n):
    out = []
    for _ in range(n):
        p = x.astype(BF16).astype(F32)
        out.append(p)
        x = x - p
    return out


def _sb_prompt_kernel(bias_ref, q_ref, k_ref, v_ref, o_ref, acc_ref, r_ref, *, tq, groups):
    i = pl.program_id(2)
    lane = lax.broadcasted_iota(jnp.int32, (tq, LANES), 1)
    left = lane < SB_HEAD_DIM
    nsub = tq // SB_SUB
    qqs = []
    for g in range(groups):
        q = q_ref[0, :, g * LANES:(g + 1) * LANES].astype(F32)
        bias_rows = jnp.concatenate([jnp.broadcast_to(bias_ref[g, 0:1, :], (tq, LANES)),
                                     jnp.broadcast_to(bias_ref[g, 1:2, :], (tq, LANES))], axis=0)
        qq = jnp.concatenate([jnp.where(left, q, 0.0), jnp.where(left, 0.0, q)], axis=0)
        qqs.append(jnp.concatenate([qq, bias_rows], axis=1).astype(BF16))
    k_ones = jnp.ones((LANES, tq), BF16)
    rr = lax.broadcasted_iota(jnp.int32, (SB_SUB, SB_SUB), 0)
    cc = lax.broadcasted_iota(jnp.int32, (SB_SUB, SB_SUB), 1)
    later = jnp.where(rr > cc, 1.0, 0.0).astype(BF16)
    qrow = lax.broadcasted_iota(jnp.int32, (tq, SB_SUB), 0)
    kcol = lax.broadcasted_iota(jnp.int32, (tq, SB_SUB), 1)
    acc_ref[...] = jnp.zeros_like(acc_ref)
    r_ref[...] = jnp.zeros_like(r_ref)

    def both(fn, x):
        return jnp.concatenate([fn(x[:tq]), fn(x[tq:])], axis=0)

    def run(j, masked):
        off = pl.multiple_of(j * tq, tq)
        for g in range(groups):
            gl = slice(g * LANES, (g + 1) * LANES)
            kb = jnp.concatenate([k_ref[0, gl, pl.ds(off, tq)], k_ones], axis=0)
            vb = v_ref[0, pl.ds(off, tq), gl]
            z = _dot(qqs[g], kb)
            sp, lsig = _softplus_and_logsig(z)
            r = r_ref[g]
            parts = [None] * nsub
            for n in reversed(range(nsub)):
                sl = slice(n * SB_SUB, (n + 1) * SB_SUB)
                spn = sp[:, sl]
                if masked:
                    causal = (kcol + n * SB_SUB) < qrow
                    spn = both(lambda x: jnp.where(causal, x, 0.0), spn)
                cs = _dot(spn.astype(BF16), later)
                a = jnp.exp((lsig[:, sl] - jnp.concatenate([r] * (SB_SUB // LANES), axis=1)) - cs)
                if masked:
                    a = both(lambda x: jnp.where(causal, x, 0.0), a)
                parts[n] = a.astype(BF16)
                r = r + (cs[:, 0:1] + spn[:, 0:1])
            r_ref[g] = r
            pv = _dot(jnp.concatenate(parts, axis=1), vb)
            acc_ref[:, gl] += jnp.where(left, pv[:tq], pv[tq:])

    run(i, True)

    def body(jj, carry):
        run(i - jj, False)
        return carry

    lax.fori_loop(1, i + 1, body, 0)
    o_ref[0] = acc_ref[...]


def _sb_prompt(q, k, v, bias, tq=2 * SB_SUB, groups=4):
    b, t, w = q.shape
    gw = groups * LANES
    pieces = jnp.stack(_bf16_pieces(bias, 3), axis=-1)
    bias = jnp.pad(pieces, ((0, 0), (0, LANES - 3))).reshape(w // LANES, 2, LANES)
    return pl.pallas_call(
        functools.partial(_sb_prompt_kernel, tq=tq, groups=groups),
        grid=(b, w // gw, t // tq),
        in_specs=[
            pl.BlockSpec((groups, 2, LANES), lambda bi, h, i: (h, 0, 0)),
            pl.BlockSpec((1, tq, gw), lambda bi, h, i: (bi, i, h)),
            pl.BlockSpec((1, gw, t), lambda bi, h, i: (bi, h, 0)),
            pl.BlockSpec((1, t, gw), lambda bi, h, i: (bi, 0, h)),
        ],
        out_specs=pl.BlockSpec((1, tq, gw), lambda bi, h, i: (bi, i, h)),
        out_shape=jax.ShapeDtypeStruct((b, t, w), F32),
        scratch_shapes=[pltpu.VMEM((tq, gw), F32), pltpu.VMEM((groups, 2 * tq, LANES), F32)],
        compiler_params=_cparams(3),
        name="sb_prompt",
    )(bias, q, k, v)


def _sb_decode_one(qt, bias, k_pages, v_pages, n_pages, page):
    nh, dh = SB_HEADS, SB_HEAD_DIM
    past = n_pages * page
    qb = [jnp.broadcast_to(qt[:, h:h + 1], (dh, page)) for h in range(nh)]
    zpages = []
    for p in range(n_pages):
        rows = [jnp.sum(k_pages[p, h] * qb[h], axis=0, keepdims=True) for h in range(nh)]
        zpages.append(jnp.concatenate(rows, axis=0))
    z = jnp.concatenate(zpages, axis=1) + jnp.concatenate([bias] * n_pages, axis=1)
    sp, lsig = _softplus_and_logsig(z)

    lane = lax.broadcasted_iota(jnp.int32, (nh, past), 1)
    incl = sp
    step = 1
    while step < past:
        incl = incl + jnp.where(lane < past - step, pltpu.roll(incl, past - step, 1), 0.0)
        step *= 2
    a = jnp.exp(lsig - (incl - sp))

    accs = [jnp.zeros((dh, page), F32) for _ in range(nh)]
    for p in range(n_pages):
        for h in range(nh):
            arow = jnp.broadcast_to(a[h:h + 1, p * page:(p + 1) * page], (dh, page))
            accs[h] = accs[h] + arow * v_pages[p, h]
    return jnp.concatenate([jnp.sum(acc, axis=1, keepdims=True) for acc in accs], axis=1)


def _s5_prep_kernel(lre_ref, lim_ref, ldt_ref, btr_ref, bti_ref, pre_ref, pim_ref, bbr_ref, bbi_ref):
    lre, lim = lre_ref[...], lim_ref[...]
    dt = jnp.exp(ldt_ref[...])
    mag = jnp.exp(lre * dt)
    br = mag * jnp.cos(lim * dt)
    bi = mag * jnp.sin(lim * dt)
    den = lre * lre + lim * lim
    nr, ni = br - 1.0, bi
    cr = (nr * lre + ni * lim) / den
    ci = (ni * lre - nr * lim) / den
    btr, bti = btr_ref[...], bti_ref[...]
    for k in range(S5_WINDOW):
        bbr_ref[k * SSM_GROUP:(k + 1) * SSM_GROUP, :] = cr * btr - ci * bti
        bbi_ref[k * SSM_GROUP:(k + 1) * SSM_GROUP, :] = cr * bti + ci * btr
        cr, ci = cr * br - ci * bi, cr * bi + ci * br
    pr, pi_ = br, bi
    for k in range(S5_WINDOW):
        pre_ref[k:k + 1, :] = pr
        pim_ref[k:k + 1, :] = pi_
        if k + 1 < S5_WINDOW:
            pr, pi_ = pr * br - pi_ * bi, pr * bi + pi_ * br
    for k in range(S5_WINDOW, S5_POW_ROWS):
        pr, pi_ = pr * pr - pi_ * pi_, 2.0 * pr * pi_
        pre_ref[k:k + 1, :] = pr
        pim_ref[k:k + 1, :] = pi_


def _s5_prepare(lam_re, lam_im, log_dt, b_re, b_im, c_re, c_im):
    ch = SSM_CH
    lre = lam_re.reshape(1, ch)
    lim = lam_im.reshape(1, ch)
    ldt = jnp.repeat(log_dt, SSM_STATE).reshape(1, ch)
    btr = b_re.reshape(ch, SSM_GROUP).T
    bti = b_im.reshape(ch, SSM_GROUP).T
    pre, pim, bbr, bbi = pl.pallas_call(
        _s5_prep_kernel,
        out_shape=[jax.ShapeDtypeStruct((S5_POW_ROWS, ch), F32)] * 2
        + [jax.ShapeDtypeStruct((S5_WINDOW * SSM_GROUP, ch), F32)] * 2,
        name="s5_prep",
    )(lre, lim, ldt, btr, bti)

    s_idx = np.arange(N_STRIPS)[:, None, None]
    j_idx = np.arange(LANES)[None, :, None]
    c_idx = np.arange(LANES)[None, None, :]
    grp_of_ch = (LANES * (s_idx // 4) + j_idx) // SSM_GROUP
    grp_of_state = (LANES * s_idx + c_idx) // SSM_STATE
    mask = jnp.asarray(grp_of_ch == grp_of_state, F32)

    def b_strips(bb):
        t = bb.reshape(SSM_GROUP, N_STRIPS, LANES).transpose(1, 0, 2)
        return jnp.tile(t, (1, LANES // SSM_GROUP, 1)) * mask

    def b_windows(bb):
        per_k = [b_strips(bb[k * SSM_GROUP:(k + 1) * SSM_GROUP]) for k in range(S5_WINDOW)]
        return jnp.concatenate(per_k, axis=1)

    wb = jnp.concatenate([b_windows(bbr), b_windows(bbi)], axis=2).astype(BF16)

    def c_strips(c):
        t = c.transpose(0, 2, 1).reshape(N_STRIPS, LANES, SSM_GROUP)
        return jnp.tile(t, (1, 1, LANES // SSM_GROUP)) * mask.transpose(0, 2, 1)

    wc = jnp.concatenate([c_strips(c_re), -c_strips(c_im)], axis=1).astype(BF16)

    def pw(p):
        return p.reshape(S5_POW_ROWS, N_STRIPS, LANES).transpose(1, 0, 2)

    lamp = jnp.concatenate([pw(pre), pw(pim)], axis=2)
    nb = SSM_WIDTH // LANES
    wb = wb.reshape(nb, 4, S5_WINDOW * LANES, 2 * LANES).transpose(0, 2, 1, 3)
    wb = wb.reshape(nb, S5_WINDOW * LANES, 8 * LANES)
    wc = wc.reshape(nb, 8 * LANES, LANES)
    return wb, wc, lamp


def _scan_rows(xr, xi, lam):
    s, k = S5_WINDOW, S5_WINDOW - 1
    while s < SCAN_ROWS:
        ar, ai = lam[k:k + 1, :LANES], lam[k:k + 1, LANES:]
        pr, pi_ = xr[:-s], xi[:-s]
        nr = xr[s:] + ar * pr - ai * pi_
        ni = xi[s:] + ar * pi_ + ai * pr
        xr = jnp.concatenate([xr[:s], nr], axis=0)
        xi = jnp.concatenate([xi[:s], ni], axis=0)
        s, k = 2 * s, k + 1
    return xr, xi


def _glu_out(y, wglu_ref, bglu_ref):
    y = _gelu(y)
    return y * jax.nn.sigmoid(_dot(y.astype(BF16), wglu_ref[...]) + bglu_ref[...])


def _s5_prompt_kernel(u_ref, wb_ref, wc_ref, lamp_ref, dskip_ref, wglu_ref, bglu_ref,
                      o_ref, hre_ref, him_ref, carry_ref, *, chunk):
    c = pl.program_id(1)

    @pl.when(c == 0)
    def _():
        carry_ref[...] = jnp.zeros_like(carry_ref)

    u = u_ref[0]
    seg_pos = lax.broadcasted_iota(jnp.int32, (chunk, LANES), 0) % SCAN_ROWS
    y_blocks = []
    for kb in range(SSM_WIDTH // LANES):
        ukb = u[:, kb * LANES:(kb + 1) * LANES]
        lagged = [ukb] + [jnp.where(seg_pos >= k, pltpu.roll(ukb, k, 0), 0.0)
                          for k in range(1, S5_WINDOW)]
        uwin = jnp.concatenate(lagged, axis=1).astype(BF16)
        res4 = _dot(uwin, wb_ref[kb])
        states = []
        for j in range(4):
            s = 4 * kb + j
            res = res4[:, 2 * j * LANES:2 * (j + 1) * LANES]
            lam = lamp_ref[s]
            prev = carry_ref[s]
            cr, ci = prev[7:8, :LANES], prev[7:8, LANES:]
            lr, li = lam[:S5_WINDOW, :LANES], lam[:S5_WINDOW, LANES:]
            parts = []
            for h in range(chunk // SCAN_ROWS):
                rs = slice(h * SCAN_ROWS, (h + 1) * SCAN_ROWS)
                xr, xi = res[rs, :LANES], res[rs, LANES:]
                xr = jnp.concatenate([xr[:S5_WINDOW] + (lr * cr - li * ci), xr[S5_WINDOW:]], axis=0)
                xi = jnp.concatenate([xi[:S5_WINDOW] + (lr * ci + li * cr), xi[S5_WINDOW:]], axis=0)
                xr, xi = _scan_rows(xr, xi, lam)
                cr, ci = xr[SCAN_ROWS - 1:], xi[SCAN_ROWS - 1:]
                parts.append(jnp.concatenate([xr, xi], axis=1).astype(BF16))
            carry_ref[s] = jnp.concatenate([xr[SCAN_ROWS - 8:], xi[SCAN_ROWS - 8:]], axis=1)
            states.append(jnp.concatenate(parts, axis=0))
        y_blocks.append(_dot(jnp.concatenate(states, axis=1), wc_ref[kb]))
    y = jnp.concatenate(y_blocks, axis=1) + dskip_ref[...] * u
    o_ref[0] = _glu_out(y, wglu_ref, bglu_ref)

    @pl.when(c == pl.num_programs(1) - 1)
    def _():
        for s in range(N_STRIPS):
            last = carry_ref[s]
            hre_ref[0, :, s * LANES:(s + 1) * LANES] = last[7:8, :LANES]
            him_ref[0, :, s * LANES:(s + 1) * LANES] = last[7:8, LANES:]


def _s5_prompt(u, wb, wc, lamp, dskip, wglu, bglu, chunk=512):
    b, t, w = u.shape
    ch = SSM_CH
    return pl.pallas_call(
        functools.partial(_s5_prompt_kernel, chunk=chunk),
        grid=(b, t // chunk),
        in_specs=[pl.BlockSpec((1, chunk, w), lambda bi, c: (bi, c, 0))]
        + [_full(a.shape) for a in (wb, wc, lamp, dskip, wglu, bglu)],
        out_specs=[
            pl.BlockSpec((1, chunk, w), lambda bi, c: (bi, c, 0)),
            pl.BlockSpec((1, 1, ch), lambda bi, c: (bi, 0, 0)),
            pl.BlockSpec((1, 1, ch), lambda bi, c: (bi, 0, 0)),
        ],
        out_shape=[
            jax.ShapeDtypeStruct((b, t, w), F32),
            jax.ShapeDtypeStruct((b, 1, ch), F32),
            jax.ShapeDtypeStruct((b, 1, ch), F32),
        ],
        scratch_shapes=[pltpu.VMEM((N_STRIPS, 8, 2 * LANES), F32)],
        compiler_params=_cparams(2),
        name="s5_prompt",
    )(u, wb, wc, lamp, dskip, wglu, bglu)


def _s5_step_kernel(u_ref, h0r_ref, h0i_ref, wb_ref, wc_ref, lamp_ref, dskip_ref, wglu_ref,
                    bglu_ref, o_ref, hre_ref, him_ref):
    u = u_ref[...]
    ub = u.astype(BF16)
    y_blocks = []
    for kb in range(SSM_WIDTH // LANES):
        res4 = _dot(ub[:, kb * LANES:(kb + 1) * LANES], wb_ref[kb, 0:LANES, :])
        states = []
        for j in range(4):
            s = 4 * kb + j
            sl = slice(s * LANES, (s + 1) * LANES)
            res = res4[:, 2 * j * LANES:2 * (j + 1) * LANES]
            lam = lamp_ref[s]
            lr, li = lam[0:1, :LANES], lam[0:1, LANES:]
            h0r, h0i = h0r_ref[:, sl], h0i_ref[:, sl]
            xr = res[:, :LANES] + lr * h0r - li * h0i
            xi = res[:, LANES:] + lr * h0i + li * h0r
            hre_ref[:, sl] = xr
            him_ref[:, sl] = xi
            states.append(jnp.concatenate([xr, xi], axis=1).astype(BF16))
        y_blocks.append(_dot(jnp.concatenate(states, axis=1), wc_ref[kb]))
    y = jnp.concatenate(y_blocks, axis=1) + dskip_ref[...] * u
    o_ref[...] = _glu_out(y, wglu_ref, bglu_ref)


def _s5_step(u, h0r, h0i, wb, wc, lamp, dskip, wglu, bglu):
    n, w = u.shape
    return pl.pallas_call(
        _s5_step_kernel,
        out_shape=[
            jax.ShapeDtypeStruct((n, w), F32),
            jax.ShapeDtypeStruct((n, SSM_CH), F32),
            jax.ShapeDtypeStruct((n, SSM_CH), F32),
        ],
        compiler_params=pltpu.CompilerParams(vmem_limit_bytes=VMEM_LIMIT),
        name="s5_step",
    )(u, h0r, h0i, wb, wc, lamp, dskip, wglu, bglu)


def _mix_out_kernel(x_ref, a_ref, b_ref, ga_ref, gb_ref, wa_ref, wb_ref, o_ref):
    ha = _rms(a_ref[...], ga_ref[...]).astype(BF16)
    hb = _rms(b_ref[...], gb_ref[...]).astype(BF16)
    o_ref[...] = x_ref[...] + _dot(ha, wa_ref[...]) + _dot(hb, wb_ref[...])


def _mix_out(x, a, b, ga, gb, wa, wb, tm):
    m, d = x.shape
    w = a.shape[1]
    return pl.pallas_call(
        _mix_out_kernel,
        grid=(m // tm,),
        in_specs=[
            pl.BlockSpec((tm, d), lambda i: (i, 0)),
            pl.BlockSpec((tm, w), lambda i: (i, 0)),
            pl.BlockSpec((tm, w), lambda i: (i, 0)),
        ] + [_full(t.shape) for t in (ga, gb, wa, wb)],
        out_specs=pl.BlockSpec((tm, d), lambda i: (i, 0)),
        out_shape=jax.ShapeDtypeStruct((m, d), F32),
        compiler_params=_cparams(1),
        name="mix_out",
    )(x, a, b, ga, gb, wa, wb)


def _mem_prompt_kernel(x0_ref, a_ref, b_ref, ga_ref, gb_ref, wa_ref, wb_ref,
                       g_ref, wq_ref, mk_ref, mv_ref, wo_ref, o_ref, ob_ref):
    ha = _rms(a_ref[0], ga_ref[...]).astype(BF16)
    hb = _rms(b_ref[0], gb_ref[...]).astype(BF16)
    x = x0_ref[0] + _dot(ha, wa_ref[...]) + _dot(hb, wb_ref[...])
    q = _dot(_rms(x, g_ref[...]).astype(BF16), wq_ref[...])
    qb = (q * (1.0 / math.sqrt(MEM_HEAD_DIM))).astype(BF16)
    for h in range(MEM_HEADS):
        sl = slice(h * MEM_HEAD_DIM, (h + 1) * MEM_HEAD_DIM)
        s = lax.dot_general(qb[:, sl], mk_ref[0, :, sl], _NT, preferred_element_type=F32)
        e = jnp.exp(s - jnp.max(s, axis=-1, keepdims=True))
        o = _dot(e.astype(BF16), mv_ref[0, :, sl]) / jnp.sum(e, axis=-1, keepdims=True)
        ob_ref[:, sl] = o.astype(BF16)
    o_ref[0] = x + _dot(ob_ref[...], wo_ref[...])


def _mem_prompt(x, a, b_, ga, gb, wa, wb, g, wq, mk, mv, wo, tm):
    b, t, d = x.shape
    w = a.shape[2]
    nk = mk.shape[1]
    tok = lambda width: pl.BlockSpec((1, tm, width), lambda bi, i: (bi, i, 0))
    mem = pl.BlockSpec((1, nk, d), lambda bi, i: (bi, 0, 0))
    return pl.pallas_call(
        _mem_prompt_kernel,
        grid=(b, t // tm),
        in_specs=[tok(d), tok(w), tok(w)] + [_full(p.shape) for p in (ga, gb, wa, wb, g, wq)]
        + [mem, mem, _full(wo.shape)],
        out_specs=tok(d),
        out_shape=jax.ShapeDtypeStruct((b, t, d), F32),
        scratch_shapes=[pltpu.VMEM((tm, d), BF16)],
        compiler_params=_cparams(2),
        name="mem_prompt",
    )(x, a, b_, ga, gb, wa, wb, g, wq, mk, mv, wo)


MEM_SEQS_PER_STEP = 4


def _mem_decode_kernel(q_ref, k_ref, v_ref, o_ref, *, n_mem):
    def tree(fn, xs):
        while len(xs) > 1:
            xs = [fn(xs[j], xs[j + 1]) for j in range(0, len(xs) - 1, 2)] + xs[len(xs) & ~1:]
        return xs[0]

    nh, dh = MEM_HEADS, MEM_HEAD_DIM
    for i in range(MEM_SEQS_PER_STEP):
        q = q_ref[i]
        q2 = jnp.concatenate([q, q], axis=0)
        pair = lambda ref, t: ref[0, i, pl.ds(2 * t, 2)].reshape(2 * nh, dh)
        s = [jnp.sum(pair(k_ref, t) * q2, axis=-1, keepdims=True) for t in range(n_mem // 2)]
        m = tree(jnp.maximum, s)
        m = jnp.maximum(m[:nh], m[nh:])
        m = jnp.concatenate([m, m], axis=0)
        e = [jnp.exp(st - m) for st in s]
        denom = tree(jnp.add, e)
        acc = tree(jnp.add, [e[t] * pair(v_ref, t) for t in range(n_mem // 2)])
        o_ref[i] = (acc[:nh] + acc[nh:]) / (denom[:nh] + denom[nh:])


def _mem_decode(q, mem_k, mem_v, layer):
    _, n, nk, nh, dh = mem_k.shape
    ns = MEM_SEQS_PER_STEP
    kv_spec = pl.BlockSpec((1, ns, nk, nh, dh), lambda b: (layer, b, 0, 0, 0))
    return pl.pallas_call(
        functools.partial(_mem_decode_kernel, n_mem=nk),
        grid=(n // ns,),
        in_specs=[pl.BlockSpec((ns, nh, dh), lambda b: (b, 0, 0)), kv_spec, kv_spec],
        out_specs=pl.BlockSpec((ns, nh, dh), lambda b: (b, 0, 0)),
        out_shape=jax.ShapeDtypeStruct((n, nh, dh), F32),
        compiler_params=_cparams(1),
        name="mem_decode",
    )(q, mem_k, mem_v)


def _proj_residual_kernel(x_ref, a_ref, w_ref, o_ref):
    o_ref[...] = x_ref[...] + _dot(a_ref[...].astype(BF16), w_ref[...])


def _proj_residual(x, a, w):
    return pl.pallas_call(
        _proj_residual_kernel,
        out_shape=jax.ShapeDtypeStruct(x.shape, F32),
        compiler_params=pltpu.CompilerParams(vmem_limit_bytes=VMEM_LIMIT),
        name="proj_residual",
    )(x, a, w)


FF_CHUNK = 256


DEC_PER_STEP = 4


def _ffn_prompt_kernel(pt_ref, x_ref, g_ref, wg_ref, wu_ref, cw_ref, cb_ref, wd_ref, gf_ref,
                       qd_ref, bd_ref, kc_ref, vc_ref,
                       y_ref, cs_ref, od_ref, act_ref, carry_ref, kbuf, vbuf, sem, *,
                       tm, layer, n_pages, page):
    step = pl.program_id(0) * pl.num_programs(1) + pl.program_id(1)
    n_steps = pl.num_programs(0) * pl.num_programs(1)

    def start_fetch(seq, slot):
        for p in range(n_pages):
            pg = pt_ref[seq * n_pages + p]
            pltpu.make_async_copy(kc_ref.at[layer, pg], kbuf.at[slot, p], sem.at[0, slot]).start()
            pltpu.make_async_copy(vc_ref.at[layer, pg], vbuf.at[slot, p], sem.at[1, slot]).start()

    def wait_fetch(slot):
        for p in range(n_pages):
            pltpu.make_async_copy(kc_ref.at[layer, 0], kbuf.at[slot, p], sem.at[0, slot]).wait()
            pltpu.make_async_copy(vc_ref.at[layer, 0], vbuf.at[slot, p], sem.at[1, slot]).wait()

    @pl.when(step == 0)
    def _():
        start_fetch(0, 0)
        start_fetch(1, 1)

    @pl.when(pl.program_id(1) == 0)
    def _():
        carry_ref[...] = jnp.zeros_like(carry_ref)

    x = x_ref[0]
    h = _rms(x, g_ref[...]).astype(BF16)
    row = lax.broadcasted_iota(jnp.int32, (8, FF_CHUNK), 0)
    d_ff = wg_ref.shape[1]
    n_chunks = d_ff // FF_CHUNK

    def ff_chunk(c):
        sl = slice(c * FF_CHUNK, (c + 1) * FF_CHUNK)
        g = _dot(h, wg_ref[:, sl])
        up = _dot(h, wu_ref[:, sl])
        prev = carry_ref[:, sl]
        p1, p2 = prev[7:8], prev[6:7]
        r1, r2 = pltpu.roll(g, 1, 0), pltpu.roll(g, 2, 0)
        g1 = jnp.concatenate([jnp.where(row == 0, p1, r1[:8]), r1[8:]], axis=0)
        g2 = jnp.concatenate(
            [jnp.where(row == 0, p2, jnp.where(row == 1, p1, r2[:8])), r2[8:]], axis=0)
        conv = cb_ref[:, sl] + cw_ref[0:1, sl] * g2 + cw_ref[1:2, sl] * g1 + cw_ref[2:3, sl] * g
        act_ref[:, sl] = (_gelu(conv) * up).astype(BF16)
        carry_ref[:, sl] = g[tm - 8:]
        cs_ref[0, :, sl] = g[tm - 2:]

    bounds = [(n_chunks * u + DEC_PER_STEP // 2) // DEC_PER_STEP for u in range(DEC_PER_STEP + 1)]
    for u in range(DEC_PER_STEP):
        slot = u % 2
        wait_fetch(slot)
        od_ref[u] = _sb_decode_one(qd_ref[u], bd_ref[...], kbuf.at[slot], vbuf.at[slot],
                                   n_pages, page)
        for c in range(bounds[u], bounds[u + 1]):
            ff_chunk(c)
        if u == DEC_PER_STEP - 1:
            x3 = x + _dot(act_ref[...], wd_ref[...])
            y_ref[0] = _rms(x3, gf_ref[...])
        if u + 2 < DEC_PER_STEP:
            start_fetch(step * DEC_PER_STEP + u + 2, slot)
        else:
            @pl.when(step + 1 < n_steps)
            def _():
                start_fetch((step + 1) * DEC_PER_STEP + u + 2 - DEC_PER_STEP, slot)


def _ffn_prompt(x, g, wg, wu, cw, cb, wd, gf, q_dec, bias_dec, cache_k, cache_v, page_table,
                layer, tm):
    b, t, d = x.shape
    d_ff = wg.shape[1]
    n_seq, n_pages = page_table.shape
    page = cache_k.shape[2]
    nh, dh = SB_HEADS, SB_HEAD_DIM
    n_tiles = t // tm
    assert n_seq == b * n_tiles * DEC_PER_STEP and DEC_PER_STEP % 2 == 0
    kc = jnp.transpose(cache_k, (0, 1, 3, 4, 2))
    vc = jnp.transpose(cache_v, (0, 1, 3, 4, 2))
    qt = jnp.transpose(q_dec.reshape(n_seq, nh, dh), (0, 2, 1))
    bias_lanes = jnp.broadcast_to(bias_dec[:, None], (nh, page))
    dec = pl.BlockSpec((DEC_PER_STEP, dh, nh), lambda bi, i, pt: (bi * n_tiles + i, 0, 0))
    grid_spec = pltpu.PrefetchScalarGridSpec(
        num_scalar_prefetch=1,
        grid=(b, n_tiles),
        in_specs=[pl.BlockSpec((1, tm, d), lambda bi, i, pt: (bi, i, 0))]
        + [_full(a.shape) for a in (g, wg, wu, cw, cb, wd, gf)]
        + [dec, _full(bias_lanes.shape),
           pl.BlockSpec(memory_space=pl.ANY), pl.BlockSpec(memory_space=pl.ANY)],
        out_specs=[
            pl.BlockSpec((1, tm, d), lambda bi, i, pt: (bi, i, 0)),
            pl.BlockSpec((1, 2, d_ff), lambda bi, i, pt: (bi, 0, 0)),
            dec,
        ],
        scratch_shapes=[
            pltpu.VMEM((tm, d_ff), BF16),
            pltpu.VMEM((8, d_ff), F32),
            pltpu.VMEM((2, n_pages, nh, dh, page), F32),
            pltpu.VMEM((2, n_pages, nh, dh, page), F32),
            pltpu.SemaphoreType.DMA((2, 2)),
        ],
    )
    y, cs, od = pl.pallas_call(
        functools.partial(_ffn_prompt_kernel, tm=tm, layer=layer, n_pages=n_pages, page=page),
        grid_spec=grid_spec,
        out_shape=[
            jax.ShapeDtypeStruct((b, t, d), F32),
            jax.ShapeDtypeStruct((b, 2, d_ff), F32),
            jax.ShapeDtypeStruct((n_seq, dh, nh), F32),
        ],
        compiler_params=_cparams(2, VMEM_LIMIT_FFN),
        name="ffn_prompt",
    )(page_table.reshape(-1), x, g, wg, wu, cw, cb, wd, gf, qt, bias_lanes, kc, vc)
    return y, cs, jnp.transpose(od, (0, 2, 1)).reshape(n_seq, nh * dh)


def _ffn_step_kernel(x_ref, g_ref, wg_ref, wu_ref, cw_ref, cb_ref, wd_ref, gf_ref, p0_ref, p1_ref,
                     y_ref, gate_ref, act_ref):
    x = x_ref[...]
    h = _rms(x, g_ref[...]).astype(BF16)
    d_ff = wg_ref.shape[1]
    for c in range(d_ff // FF_CHUNK):
        sl = slice(c * FF_CHUNK, (c + 1) * FF_CHUNK)
        g = _dot(h, wg_ref[:, sl])
        up = _dot(h, wu_ref[:, sl])
        conv = (cb_ref[:, sl] + cw_ref[0:1, sl] * p0_ref[:, sl] + cw_ref[1:2, sl] * p1_ref[:, sl]
                + cw_ref[2:3, sl] * g)
        act_ref[:, sl] = (_gelu(conv) * up).astype(BF16)
        gate_ref[:, sl] = g
    x3 = x + _dot(act_ref[...], wd_ref[...])
    y_ref[...] = _rms(x3, gf_ref[...])


def _ffn_step(x, g, wg, wu, cw, cb, wd, gf, p0, p1):
    n, d = x.shape
    d_ff = wg.shape[1]
    return pl.pallas_call(
        _ffn_step_kernel,
        out_shape=[jax.ShapeDtypeStruct((n, d), F32), jax.ShapeDtypeStruct((n, d_ff), F32)],
        scratch_shapes=[pltpu.VMEM((n, d_ff), BF16)],
        compiler_params=pltpu.CompilerParams(vmem_limit_bytes=VMEM_LIMIT),
        name="ffn_step",
    )(x, g, wg, wu, cw, cb, wd, gf, p0, p1)


def kernel(x_prompt, x_sample, cache_sb_k, cache_sb_v, page_table, state_ssm_re, state_ssm_im, state_conv, cache_mem_k, cache_mem_v, mem_prompt, g_mix, w_in, sb_bias, lam_re, lam_im, log_dt, b_re, b_im, c_re, c_im, d_skip, w_glu, b_glu, g_sb_out, g_ssm_out, w_out, g_mem_q, g_mem_kv, w_mq, w_mk, w_mv, w_mo, g_ffn, w_gate, w_up, conv_w, conv_b, w_down, g_final):
    depth = w_in.shape[0]
    n_p, t_p, d = x_prompt.shape
    n_s = x_sample.shape[0]
    assert x_sample.shape[1] == 1
    tm = 512
    q_scale = 1.0 / math.sqrt(SB_HEAD_DIM)
    row = lambda a: a.reshape(1, -1)
    gf = row(g_final)

    yp = x_prompt.reshape(n_p * t_p, d)
    ys = x_sample.reshape(n_s, d)
    outs = {k: [] for k in ("pk", "pv", "pre", "pim", "pconv", "pmk", "pmv",
                            "sk", "sv", "sre", "sim", "sconv")}
    y_prompt = y_sample = None
    for l in range(depth):
        w_in_b = w_in[l].astype(BF16)
        w_q, w_k, w_v, w_u = (w_in_b[:, j * SB_WIDTH:(j + 1) * SB_WIDTH] for j in range(4))
        w_out_b = w_out[l].astype(BF16)
        wo_a, wo_b = w_out_b[:SB_WIDTH], w_out_b[SB_WIDTH:]
        wglu_b = w_glu[l].astype(BF16)
        wmq, wmk, wmv, wmo = (w[l].astype(BF16) for w in (w_mq, w_mk, w_mv, w_mo))
        wg, wu, wd = (w[l].astype(BF16) for w in (w_gate, w_up, w_down))
        wb, wc, lamp = _s5_prepare(lam_re[l], lam_im[l], log_dt[l], b_re[l], b_im[l], c_re[l], c_im[l])
        s5_w = (wb, wc, lamp, row(d_skip[l]), wglu_b, row(b_glu[l]))
        ffn_w = (row(g_ffn[l]), wg, wu, conv_w[l], row(conv_b[l]), wd, gf)

        q_b, kt_f, kt_b, v_b, vt_f, u_f = _in_proj_prompt(
            yp.reshape(n_p, t_p, d), row(g_mix[l]), w_q, w_k, w_v, w_u, q_scale, tm)
        o_sb = _sb_prompt(q_b, kt_b, v_b, sb_bias[l])
        o_ssm, hre, him = _s5_prompt(u_f, *s5_w)
        n_mem = mem_prompt.shape[1]
        mk_f, mk_b, mv_f, mv_b = _norm_proj(
            mem_prompt.reshape(n_p * n_mem, d), row(g_mem_kv[l]), [wmk, wmv],
            [(True, True, 1.0), (True, True, 1.0)], n_mem)
        x2 = _mem_prompt(yp.reshape(n_p, t_p, d), o_sb, o_ssm, row(g_sb_out[l]), row(g_ssm_out[l]),
                         wo_a, wo_b, row(g_mem_q[l]), wmq,
                         mk_b.reshape(n_p, n_mem, d), mv_b.reshape(n_p, n_mem, d), wmo, tm)
        qs_f, ks_f, vs_f, us_f = _norm_proj(
            ys, row(g_mix[l]), [w_q, w_k, w_v, w_u],
            [(True, False, q_scale), (True, False, 1.0), (True, False, 1.0), (True, False, 1.0)], n_s)
        y3, cs_p, os_sb = _ffn_prompt(x2, *ffn_w, qs_f, sb_bias[l], cache_sb_k, cache_sb_v,
                                      page_table, l, tm)
        if l + 1 < depth:
            raise NotImplementedError("final norm is fused into the last layer's FFN")
        y_prompt = y3
        to_cache = lambda a: jnp.transpose(a.reshape(n_p, SB_HEADS, SB_HEAD_DIM, t_p), (0, 3, 1, 2))
        outs["pk"].append(to_cache(kt_f))
        outs["pv"].append(to_cache(vt_f))
        outs["pre"].append(hre.reshape(n_p, SSM_GROUPS, SSM_STATE))
        outs["pim"].append(him.reshape(n_p, SSM_GROUPS, SSM_STATE))
        outs["pconv"].append(cs_p)
        outs["pmk"].append(mk_f.reshape(n_p, n_mem, MEM_HEADS, MEM_HEAD_DIM))
        outs["pmv"].append(mv_f.reshape(n_p, n_mem, MEM_HEADS, MEM_HEAD_DIM))

        os_ssm, hsr, hsi = _s5_step(us_f, state_ssm_re[l].reshape(n_s, SSM_CH),
                                    state_ssm_im[l].reshape(n_s, SSM_CH), *s5_w)
        x1s = _mix_out(ys, os_sb, os_ssm, row(g_sb_out[l]), row(g_ssm_out[l]), wo_a, wo_b, n_s)
        (qm_f,) = _norm_proj(x1s, row(g_mem_q[l]), [wmq],
                             [(True, False, 1.0 / math.sqrt(MEM_HEAD_DIM))], n_s)
        om = _mem_decode(qm_f.reshape(n_s, MEM_HEADS, MEM_HEAD_DIM), cache_mem_k, cache_mem_v,
                         l).reshape(n_s, d)
        x2s = _proj_residual(x1s, om, wmo)
        y3s, gate_s = _ffn_step(x2s, *ffn_w, state_conv[l][:, 0], state_conv[l][:, 1])
        y_sample = y3s
        outs["sk"].append(ks_f.reshape(n_s, 1, SB_HEADS, SB_HEAD_DIM))
        outs["sv"].append(vs_f.reshape(n_s, 1, SB_HEADS, SB_HEAD_DIM))
        outs["sre"].append(hsr.reshape(n_s, SSM_GROUPS, SSM_STATE))
        outs["sim"].append(hsi.reshape(n_s, SSM_GROUPS, SSM_STATE))
        outs["sconv"].append(jnp.stack([state_conv[l][:, 1], gate_s], axis=1))

    st = lambda k: jnp.stack(outs[k])
    return (y_prompt, y_sample.reshape(n_s, 1, d),
            st("pk"), st("pv"), st("pre"), st("pim"), st("pconv"), st("pmk"), st("pmv"),
            st("sk"), st("sv"), st("sre"), st("sim"), st("sconv"))
```

```python
import functools
import math

import numpy as np
import jax
import jax.numpy as jnp
from jax import lax
from jax.experimental import pallas as pl
from jax.experimental.pallas import tpu as pltpu

F32 = jnp.float32
BF16 = jnp.bfloat16

EPS = 1e-6
SB_HEADS = 8
SB_HEAD_DIM = 64
SB_WIDTH = SB_HEADS * SB_HEAD_DIM
SSM_GROUPS = 32
SSM_GROUP = 16
SSM_STATE = 64
SSM_WIDTH = SSM_GROUPS * SSM_GROUP
SSM_CH = SSM_GROUPS * SSM_STATE
MEM_HEADS = 4
MEM_HEAD_DIM = 256
LANES = 128
N_STRIPS = SSM_CH // LANES
SCAN_ROWS = 128
S5_WINDOW = 8
S5_POW_ROWS = 16
VMEM_LIMIT = 48 * 1024 * 1024
VMEM_LIMIT_FFN = 56 * 1024 * 1024

_NT = (((1,), (1,)), ((), ()))


def _cparams(n_axes, vmem_limit=VMEM_LIMIT):
    return pltpu.CompilerParams(
        dimension_semantics=("arbitrary",) * n_axes, vmem_limit_bytes=vmem_limit)


def _rms(x, g):
    ms = jnp.mean(x * x, axis=-1, keepdims=True)
    return x * lax.rsqrt(ms + EPS) * g


def _gelu(x):
    c = math.sqrt(2.0 / math.pi)
    return x * (0.5 + 0.5 * jnp.tanh(x * (c + (c * 0.044715) * (x * x))))


def _dot(a, b):
    return jnp.dot(a, b, preferred_element_type=F32)


def _full(shape):
    n = len(shape)
    return pl.BlockSpec(shape, lambda *_: (0,) * n)


def _norm_proj_kernel(x_ref, g_ref, *refs, out_kinds):
    n_w = len(out_kinds)
    w_refs, out_refs = refs[:n_w], refs[n_w:]
    h = _rms(x_ref[...], g_ref[...]).astype(BF16)
    oi = 0
    for w_ref, (want_f32, want_bf16, scale) in zip(w_refs, out_kinds):
        r = _dot(h, w_ref[...])
        if scale != 1.0:
            r = r * scale
        if want_f32:
            out_refs[oi][...] = r
            oi += 1
        if want_bf16:
            out_refs[oi][...] = r.astype(BF16)
            oi += 1


def _norm_proj(x, g, ws, out_kinds, tm):
    m, d = x.shape
    out_shapes, out_specs = [], []
    for w, (want_f32, want_bf16, _) in zip(ws, out_kinds):
        n = w.shape[1]
        for want, dt in ((want_f32, F32), (want_bf16, BF16)):
            if want:
                out_shapes.append(jax.ShapeDtypeStruct((m, n), dt))
                out_specs.append(pl.BlockSpec((tm, n), lambda i: (i, 0)))
    return pl.pallas_call(
        functools.partial(_norm_proj_kernel, out_kinds=tuple(out_kinds)),
        grid=(m // tm,),
        in_specs=[pl.BlockSpec((tm, d), lambda i: (i, 0)), _full(g.shape)]
        + [_full(w.shape) for w in ws],
        out_specs=out_specs,
        out_shape=out_shapes,
        compiler_params=_cparams(1),
        name="norm_proj",
    )(x, g, *ws)


def _in_proj_prompt_kernel(x_ref, g_ref, wq_ref, wkt_ref, wv_ref, wvt_ref, wu_ref,
                           q_ref, kt_ref, ktb_ref, vb_ref, vt_ref, u_ref, *, q_scale):
    h = _rms(x_ref[0], g_ref[...]).astype(BF16)
    q_ref[0] = (_dot(h, wq_ref[...]) * q_scale).astype(BF16)
    kt = lax.dot_general(wkt_ref[...], h, _NT, preferred_element_type=F32)
    kt_ref[0] = kt
    ktb_ref[0] = kt.astype(BF16)
    vb_ref[0] = _dot(h, wv_ref[...]).astype(BF16)
    vt_ref[0] = lax.dot_general(wvt_ref[...], h, _NT, preferred_element_type=F32)
    u_ref[0] = _dot(h, wu_ref[...])


def _in_proj_prompt(x, g, wq, wk, wv, wu, q_scale, tm):
    b, t, d = x.shape
    w = wq.shape[1]
    tok = pl.BlockSpec((1, tm, w), lambda bi, i: (bi, i, 0))
    tr = pl.BlockSpec((1, w, tm), lambda bi, i: (bi, 0, i))
    ws = (wq, wk.T, wv, wv.T, wu)
    return pl.pallas_call(
        functools.partial(_in_proj_prompt_kernel, q_scale=q_scale),
        grid=(b, t // tm),
        in_specs=[pl.BlockSpec((1, tm, d), lambda bi, i: (bi, i, 0)), _full(g.shape)]
        + [_full(a.shape) for a in ws],
        out_specs=[tok, tr, tr, tok, tr, tok],
        out_shape=[
            jax.ShapeDtypeStruct((b, t, w), BF16),
            jax.ShapeDtypeStruct((b, w, t), F32),
            jax.ShapeDtypeStruct((b, w, t), BF16),
            jax.ShapeDtypeStruct((b, t, w), BF16),
            jax.ShapeDtypeStruct((b, w, t), F32),
            jax.ShapeDtypeStruct((b, t, w), F32),
        ],
        compiler_params=_cparams(2),
        name="in_proj_prompt",
    )(x, g, *ws)


SB_SUB = 256
LOG2E = 1.4426950408889634


def _softplus_and_logsig(z):
    e = jnp.exp2(jnp.abs(z) * -LOG2E)
    sp = jnp.maximum(z, 0.0) + jnp.log(1.0 + e)
    return sp, z - sp


def _bf16_pieces(x, n):
    out = []
    for _ in range(n):
        p = x.astype(BF16).astype(F32)
        out.append(p)
        x = x - p
    return out


def _sb_prompt_kernel(bias_ref, q_ref, k_ref, v_ref, o_ref, acc_ref, r_ref, *, tq, groups):
    i = pl.program_id(2)
    lane = lax.broadcasted_iota(jnp.int32, (tq, LANES), 1)
    left = lane < SB_HEAD_DIM
    nsub = tq // SB_SUB
    qqs = []
    for g in range(groups):
        q = q_ref[0, :, g * LANES:(g + 1) * LANES].astype(F32)
        bias_rows = jnp.concatenate([jnp.broadcast_to(bias_ref[g, 0:1, :], (tq, LANES)),
                                     jnp.broadcast_to(bias_ref[g, 1:2, :], (tq, LANES))], axis=0)
        qq = jnp.concatenate([jnp.where(left, q, 0.0), jnp.where(left, 0.0, q)], axis=0)
        qqs.append(jnp.concatenate([qq, bias_rows], axis=1).astype(BF16))
    k_ones = jnp.ones((LANES, tq), BF16)
    rr = lax.broadcasted_iota(jnp.int32, (SB_SUB, SB_SUB), 0)
    cc = lax.broadcasted_iota(jnp.int32, (SB_SUB, SB_SUB), 1)
    later = jnp.where(rr > cc, 1.0, 0.0).astype(BF16)
    qrow = lax.broadcasted_iota(jnp.int32, (tq, SB_SUB), 0)
    kcol = lax.broadcasted_iota(jnp.int32, (tq, SB_SUB), 1)
    acc_ref[...] = jnp.zeros_like(acc_ref)
    r_ref[...] = jnp.zeros_like(r_ref)

    def both(fn, x):
        return jnp.concatenate([fn(x[:tq]), fn(x[tq:])], axis=0)

    def run(j, masked):
        off = pl.multiple_of(j * tq, tq)
        for g in range(groups):
            gl = slice(g * LANES, (g + 1) * LANES)
            kb = jnp.concatenate([k_ref[0, gl, pl.ds(off, tq)], k_ones], axis=0)
            vb = v_ref[0, pl.ds(off, tq), gl]
            z = _dot(qqs[g], kb)
            sp, lsig = _softplus_and_logsig(z)
            r = r_ref[g]
            parts = [None] * nsub
            for n in reversed(range(nsub)):
                sl = slice(n * SB_SUB, (n + 1) * SB_SUB)
                spn = sp[:, sl]
                if masked:
                    causal = (kcol + n * SB_SUB) < qrow
                    spn = both(lambda x: jnp.where(causal, x, 0.0), spn)
                cs = _dot(spn.astype(BF16), later)
                a = jnp.exp((lsig[:, sl] - jnp.concatenate([r] * (SB_SUB // LANES), axis=1)) - cs)
                if masked:
                    a = both(lambda x: jnp.where(causal, x, 0.0), a)
                parts[n] = a.astype(BF16)
                r = r + (cs[:, 0:1] + spn[:, 0:1])
            r_ref[g] = r
            pv = _dot(jnp.concatenate(parts, axis=1), vb)
            acc_ref[:, gl] += jnp.where(left, pv[:tq], pv[tq:])

    run(i, True)

    def body(jj, carry):
        run(i - jj, False)
        return carry

    lax.fori_loop(1, i + 1, body, 0)
    o_ref[0] = acc_ref[...]


def _sb_prompt(q, k, v, bias, tq=2 * SB_SUB, groups=4):
    b, t, w = q.shape
    gw = groups * LANES
    pieces = jnp.stack(_bf16_pieces(bias, 3), axis=-1)
    bias = jnp.pad(pieces, ((0, 0), (0, LANES - 3))).reshape(w // LANES, 2, LANES)
    return pl.pallas_call(
        functools.partial(_sb_prompt_kernel, tq=tq, groups=groups),
        grid=(b, w // gw, t // tq),
        in_specs=[
            pl.BlockSpec((groups, 2, LANES), lambda bi, h, i: (h, 0, 0)),
            pl.BlockSpec((1, tq, gw), lambda bi, h, i: (bi, i, h)),
            pl.BlockSpec((1, gw, t), lambda bi, h, i: (bi, h, 0)),
            pl.BlockSpec((1, t, gw), lambda bi, h, i: (bi, 0, h)),
        ],
        out_specs=pl.BlockSpec((1, tq, gw), lambda bi, h, i: (bi, i, h)),
        out_shape=jax.ShapeDtypeStruct((b, t, w), F32),
        scratch_shapes=[pltpu.VMEM((tq, gw), F32), pltpu.VMEM((groups, 2 * tq, LANES), F32)],
        compiler_params=_cparams(3),
        name="sb_prompt",
    )(bias, q, k, v)


def _sb_decode_one(qt, bias, k_pages, v_pages, n_pages, page):
    nh, dh = SB_HEADS, SB_HEAD_DIM
    past = n_pages * page
    qb = [jnp.broadcast_to(qt[:, h:h + 1], (dh, page)) for h in range(nh)]
    zpages = []
    for p in range(n_pages):
        rows = [jnp.sum(k_pages[p, h] * qb[h], axis=0, keepdims=True) for h in range(nh)]
        zpages.append(jnp.concatenate(rows, axis=0))
    z = jnp.concatenate(zpages, axis=1) + jnp.concatenate([bias] * n_pages, axis=1)
    sp, lsig = _softplus_and_logsig(z)

    lane = lax.broadcasted_iota(jnp.int32, (nh, past), 1)
    incl = sp
    step = 1
    while step < past:
        incl = incl + jnp.where(lane < past - step, pltpu.roll(incl, past - step, 1), 0.0)
        step *= 2
    a = jnp.exp(lsig - (incl - sp))

    accs = [jnp.zeros((dh, page), F32) for _ in range(nh)]
    for p in range(n_pages):
        for h in range(nh):
            arow = jnp.broadcast_to(a[h:h + 1, p * page:(p + 1) * page], (dh, page))
            accs[h] = accs[h] + arow * v_pages[p, h]
    return jnp.concatenate([jnp.sum(acc, axis=1, keepdims=True) for acc in accs], axis=1)


def _s5_prep_kernel(lre_ref, lim_ref, ldt_ref, btr_ref, bti_ref, pre_ref, pim_ref, bbr_ref, bbi_ref):
    lre, lim = lre_ref[...], lim_ref[...]
    dt = jnp.exp(ldt_ref[...])
    mag = jnp.exp(lre * dt)
    br = mag * jnp.cos(lim * dt)
    bi = mag * jnp.sin(lim * dt)
    den = lre * lre + lim * lim
    nr, ni = br - 1.0, bi
    cr = (nr * lre + ni * lim) / den
    ci = (ni * lre - nr * lim) / den
    btr, bti = btr_ref[...], bti_ref[...]
    for k in range(S5_WINDOW):
        bbr_ref[k * SSM_GROUP:(k + 1) * SSM_GROUP, :] = cr * btr - ci * bti
        bbi_ref[k * SSM_GROUP:(k + 1) * SSM_GROUP, :] = cr * bti + ci * btr
        cr, ci = cr * br - ci * bi, cr * bi + ci * br
    pr, pi_ = br, bi
    for k in range(S5_WINDOW):
        pre_ref[k:k + 1, :] = pr
        pim_ref[k:k + 1, :] = pi_
        if k + 1 < S5_WINDOW:
            pr, pi_ = pr * br - pi_ * bi, pr * bi + pi_ * br
    for k in range(S5_WINDOW, S5_POW_ROWS):
        pr, pi_ = pr * pr - pi_ * pi_, 2.0 * pr * pi_
        pre_ref[k:k + 1, :] = pr
        pim_ref[k:k + 1, :] = pi_


def _s5_prepare(lam_re, lam_im, log_dt, b_re, b_im, c_re, c_im):
    ch = SSM_CH
    lre = lam_re.reshape(1, ch)
    lim = lam_im.reshape(1, ch)
    ldt = jnp.repeat(log_dt, SSM_STATE).reshape(1, ch)
    btr = b_re.reshape(ch, SSM_GROUP).T
    bti = b_im.reshape(ch, SSM_GROUP).T
    pre, pim, bbr, bbi = pl.pallas_call(
        _s5_prep_kernel,
        out_shape=[jax.ShapeDtypeStruct((S5_POW_ROWS, ch), F32)] * 2
        + [jax.ShapeDtypeStruct((S5_WINDOW * SSM_GROUP, ch), F32)] * 2,
        name="s5_prep",
    )(lre, lim, ldt, btr, bti)

    s_idx = np.arange(N_STRIPS)[:, None, None]
    j_idx = np.arange(LANES)[None, :, None]
    c_idx = np.arange(LANES)[None, None, :]
    grp_of_ch = (LANES * (s_idx // 4) + j_idx) // SSM_GROUP
    grp_of_state = (LANES * s_idx + c_idx) // SSM_STATE
    mask = jnp.asarray(grp_of_ch == grp_of_state, F32)

    def b_strips(bb):
        t = bb.reshape(SSM_GROUP, N_STRIPS, LANES).transpose(1, 0, 2)
        return jnp.tile(t, (1, LANES // SSM_GROUP, 1)) * mask

    def b_windows(bb):
        per_k = [b_strips(bb[k * SSM_GROUP:(k + 1) * SSM_GROUP]) for k in range(S5_WINDOW)]
        return jnp.concatenate(per_k, axis=1)

    wb = jnp.concatenate([b_windows(bbr), b_windows(bbi)], axis=2).astype(BF16)

    def c_strips(c):
        t = c.transpose(0, 2, 1).reshape(N_STRIPS, LANES, SSM_GROUP)
        return jnp.tile(t, (1, 1, LANES // SSM_GROUP)) * mask.transpose(0, 2, 1)

    wc = jnp.concatenate([c_strips(c_re), -c_strips(c_im)], axis=1).astype(BF16)

    def pw(p):
        return p.reshape(S5_POW_ROWS, N_STRIPS, LANES).transpose(1, 0, 2)

    lamp = jnp.concatenate([pw(pre), pw(pim)], axis=2)
    nb = SSM_WIDTH // LANES
    wb = wb.reshape(nb, 4, S5_WINDOW * LANES, 2 * LANES).transpose(0, 2, 1, 3)
    wb = wb.reshape(nb, S5_WINDOW * LANES, 8 * LANES)
    wc = wc.reshape(nb, 8 * LANES, LANES)
    return wb, wc, lamp


def _scan_rows(xr, xi, lam):
    s, k = S5_WINDOW, S5_WINDOW - 1
    while s < SCAN_ROWS:
        ar, ai = lam[k:k + 1, :LANES], lam[k:k + 1, LANES:]
        pr, pi_ = xr[:-s], xi[:-s]
        nr = xr[s:] + ar * pr - ai * pi_
        ni = xi[s:] + ar * pi_ + ai * pr
        xr = jnp.concatenate([xr[:s], nr], axis=0)
        xi = jnp.concatenate([xi[:s], ni], axis=0)
        s, k = 2 * s, k + 1
    return xr, xi


def _glu_out(y, wglu_ref, bglu_ref):
    y = _gelu(y)
    return y * jax.nn.sigmoid(_dot(y.astype(BF16), wglu_ref[...]) + bglu_ref[...])


def _s5_prompt_kernel(u_ref, wb_ref, wc_ref, lamp_ref, dskip_ref, wglu_ref, bglu_ref,
                      o_ref, hre_ref, him_ref, carry_ref, *, chunk):
    c = pl.program_id(1)

    @pl.when(c == 0)
    def _():
        carry_ref[...] = jnp.zeros_like(carry_ref)

    u = u_ref[0]
    seg_pos = lax.broadcasted_iota(jnp.int32, (chunk, LANES), 0) % SCAN_ROWS
    y_blocks = []
    for kb in range(SSM_WIDTH // LANES):
        ukb = u[:, kb * LANES:(kb + 1) * LANES]
        lagged = [ukb] + [jnp.where(seg_pos >= k, pltpu.roll(ukb, k, 0), 0.0)
                          for k in range(1, S5_WINDOW)]
        uwin = jnp.concatenate(lagged, axis=1).astype(BF16)
        res4 = _dot(uwin, wb_ref[kb])
        states = []
        for j in range(4):
            s = 4 * kb + j
            res = res4[:, 2 * j * LANES:2 * (j + 1) * LANES]
            lam = lamp_ref[s]
            prev = carry_ref[s]
            cr, ci = prev[7:8, :LANES], prev[7:8, LANES:]
            lr, li = lam[:S5_WINDOW, :LANES], lam[:S5_WINDOW, LANES:]
            parts = []
            for h in range(chunk // SCAN_ROWS):
                rs = slice(h * SCAN_ROWS, (h + 1) * SCAN_ROWS)
                xr, xi = res[rs, :LANES], res[rs, LANES:]
                xr = jnp.concatenate([xr[:S5_WINDOW] + (lr * cr - li * ci), xr[S5_WINDOW:]], axis=0)
                xi = jnp.concatenate([xi[:S5_WINDOW] + (lr * ci + li * cr), xi[S5_WINDOW:]], axis=0)
                xr, xi = _scan_rows(xr, xi, lam)
                cr, ci = xr[SCAN_ROWS - 1:], xi[SCAN_ROWS - 1:]
                parts.append(jnp.concatenate([xr, xi], axis=1).astype(BF16))
            carry_ref[s] = jnp.concatenate([xr[SCAN_ROWS - 8:], xi[SCAN_ROWS - 8:]], axis=1)
            states.append(jnp.concatenate(parts, axis=0))
        y_blocks.append(_dot(jnp.concatenate(states, axis=1), wc_ref[kb]))
    y = jnp.concatenate(y_blocks, axis=1) + dskip_ref[...] * u
    o_ref[0] = _glu_out(y, wglu_ref, bglu_ref)

    @pl.when(c == pl.num_programs(1) - 1)
    def _():
        for s in range(N_STRIPS):
            last = carry_ref[s]
            hre_ref[0, :, s * LANES:(s + 1) * LANES] = last[7:8, :LANES]
            him_ref[0, :, s * LANES:(s + 1) * LANES] = last[7:8, LANES:]


def _s5_prompt(u, wb, wc, lamp, dskip, wglu, bglu, chunk=512):
    b, t, w = u.shape
    ch = SSM_CH
    return pl.pallas_call(
        functools.partial(_s5_prompt_kernel, chunk=chunk),
        grid=(b, t // chunk),
        in_specs=[pl.BlockSpec((1, chunk, w), lambda bi, c: (bi, c, 0))]
        + [_full(a.shape) for a in (wb, wc, lamp, dskip, wglu, bglu)],
        out_specs=[
            pl.BlockSpec((1, chunk, w), lambda bi, c: (bi, c, 0)),
            pl.BlockSpec((1, 1, ch), lambda bi, c: (bi, 0, 0)),
            pl.BlockSpec((1, 1, ch), lambda bi, c: (bi, 0, 0)),
        ],
        out_shape=[
            jax.ShapeDtypeStruct((b, t, w), F32),
            jax.ShapeDtypeStruct((b, 1, ch), F32),
            jax.ShapeDtypeStruct((b, 1, ch), F32),
        ],
        scratch_shapes=[pltpu.VMEM((N_STRIPS, 8, 2 * LANES), F32)],
        compiler_params=_cparams(2),
        name="s5_prompt",
    )(u, wb, wc, lamp, dskip, wglu, bglu)


def _s5_step_kernel(u_ref, h0r_ref, h0i_ref, wb_ref, wc_ref, lamp_ref, dskip_ref, wglu_ref,
                    bglu_ref, o_ref, hre_ref, him_ref):
    u = u_ref[...]
    ub = u.astype(BF16)
    y_blocks = []
    for kb in range(SSM_WIDTH // LANES):
        res4 = _dot(ub[:, kb * LANES:(kb + 1) * LANES], wb_ref[kb, 0:LANES, :])
        states = []
        for j in range(4):
            s = 4 * kb + j
            sl = slice(s * LANES, (s + 1) * LANES)
            res = res4[:, 2 * j * LANES:2 * (j + 1) * LANES]
            lam = lamp_ref[s]
            lr, li = lam[0:1, :LANES], lam[0:1, LANES:]
            h0r, h0i = h0r_ref[:, sl], h0i_ref[:, sl]
            xr = res[:, :LANES] + lr * h0r - li * h0i
            xi = res[:, LANES:] + lr * h0i + li * h0r
            hre_ref[:, sl] = xr
            him_ref[:, sl] = xi
            states.append(jnp.concatenate([xr, xi], axis=1).astype(BF16))
        y_blocks.append(_dot(jnp.concatenate(states, axis=1), wc_ref[kb]))
    y = jnp.concatenate(y_blocks, axis=1) + dskip_ref[...] * u
    o_ref[...] = _glu_out(y, wglu_ref, bglu_ref)


def _s5_step(u, h0r, h0i, wb, wc, lamp, dskip, wglu, bglu):
    n, w = u.shape
    return pl.pallas_call(
        _s5_step_kernel,
        out_shape=[
            jax.ShapeDtypeStruct((n, w), F32),
            jax.ShapeDtypeStruct((n, SSM_CH), F32),
            jax.ShapeDtypeStruct((n, SSM_CH), F32),
        ],
        compiler_params=pltpu.CompilerParams(vmem_limit_bytes=VMEM_LIMIT),
        name="s5_step",
    )(u, h0r, h0i, wb, wc, lamp, dskip, wglu, bglu)


def _mix_out_kernel(x_ref, a_ref, b_ref, ga_ref, gb_ref, wa_ref, wb_ref, o_ref):
    ha = _rms(a_ref[...], ga_ref[...]).astype(BF16)
    hb = _rms(b_ref[...], gb_ref[...]).astype(BF16)
    o_ref[...] = x_ref[...] + _dot(ha, wa_ref[...]) + _dot(hb, wb_ref[...])


def _mix_out(x, a, b, ga, gb, wa, wb, tm):
    m, d = x.shape
    w = a.shape[1]
    return pl.pallas_call(
        _mix_out_kernel,
        grid=(m // tm,),
        in_specs=[
            pl.BlockSpec((tm, d), lambda i: (i, 0)),
            pl.BlockSpec((tm, w), lambda i: (i, 0)),
            pl.BlockSpec((tm, w), lambda i: (i, 0)),
        ] + [_full(t.shape) for t in (ga, gb, wa, wb)],
        out_specs=pl.BlockSpec((tm, d), lambda i: (i, 0)),
        out_shape=jax.ShapeDtypeStruct((m, d), F32),
        compiler_params=_cparams(1),
        name="mix_out",
    )(x, a, b, ga, gb, wa, wb)


def _mem_prompt_kernel(x0_ref, a_ref, b_ref, ga_ref, gb_ref, wa_ref, wb_ref,
                       g_ref, wq_ref, mk_ref, mv_ref, wo_ref, o_ref, ob_ref):
    ha = _rms(a_ref[0], ga_ref[...]).astype(BF16)
    hb = _rms(b_ref[0], gb_ref[...]).astype(BF16)
    x = x0_ref[0] + _dot(ha, wa_ref[...]) + _dot(hb, wb_ref[...])
    q = _dot(_rms(x, g_ref[...]).astype(BF16), wq_ref[...])
    qb = (q * (1.0 / math.sqrt(MEM_HEAD_DIM))).astype(BF16)
    for h in range(MEM_HEADS):
        sl = slice(h * MEM_HEAD_DIM, (h + 1) * MEM_HEAD_DIM)
        s = lax.dot_general(qb[:, sl], mk_ref[0, :, sl], _NT, preferred_element_type=F32)
        e = jnp.exp(s - jnp.max(s, axis=-1, keepdims=True))
        o = _dot(e.astype(BF16), mv_ref[0, :, sl]) / jnp.sum(e, axis=-1, keepdims=True)
        ob_ref[:, sl] = o.astype(BF16)
    o_ref[0] = x + _dot(ob_ref[...], wo_ref[...])


def _mem_prompt(x, a, b_, ga, gb, wa, wb, g, wq, mk, mv, wo, tm):
    b, t, d = x.shape
    w = a.shape[2]
    nk = mk.shape[1]
    tok = lambda width: pl.BlockSpec((1, tm, width), lambda bi, i: (bi, i, 0))
    mem = pl.BlockSpec((1, nk, d), lambda bi, i: (bi, 0, 0))
    return pl.pallas_call(
        _mem_prompt_kernel,
        grid=(b, t // tm),
        in_specs=[tok(d), tok(w), tok(w)] + [_full(p.shape) for p in (ga, gb, wa, wb, g, wq)]
        + [mem, mem, _full(wo.shape)],
        out_specs=tok(d),
        out_shape=jax.ShapeDtypeStruct((b, t, d), F32),
        scratch_shapes=[pltpu.VMEM((tm, d), BF16)],
        compiler_params=_cparams(2),
        name="mem_prompt",
    )(x, a, b_, ga, gb, wa, wb, g, wq, mk, mv, wo)


MEM_SEQS_PER_STEP = 8


def _mem_decode_kernel(q_ref, k_ref, v_ref, o_ref, *, n_mem):
    def tree(fn, xs):
        while len(xs) > 1:
            xs = [fn(xs[j], xs[j + 1]) for j in range(0, len(xs) - 1, 2)] + xs[len(xs) & ~1:]
        return xs[0]

    nh, dh = MEM_HEADS, MEM_HEAD_DIM
    for i in range(MEM_SEQS_PER_STEP):
        q = q_ref[i]
        q2 = jnp.concatenate([q, q], axis=0)
        pair = lambda ref, t: ref[0, i, pl.ds(2 * t, 2)].reshape(2 * nh, dh)
        s = [jnp.sum(pair(k_ref, t) * q2, axis=-1, keepdims=True) for t in range(n_mem // 2)]
        m = tree(jnp.maximum, s)
        m = jnp.maximum(m[:nh], m[nh:])
        m = jnp.concatenate([m, m], axis=0)
        e = [jnp.exp(st - m) for st in s]
        denom = tree(jnp.add, e)
        acc = tree(jnp.add, [e[t] * pair(v_ref, t) for t in range(n_mem // 2)])
        o_ref[i] = (acc[:nh] + acc[nh:]) / (denom[:nh] + denom[nh:])


def _mem_decode(q, mem_k, mem_v, layer):
    _, n, nk, nh, dh = mem_k.shape
    ns = MEM_SEQS_PER_STEP
    kv_spec = pl.BlockSpec((1, ns, nk, nh, dh), lambda b: (layer, b, 0, 0, 0))
    return pl.pallas_call(
        functools.partial(_mem_decode_kernel, n_mem=nk),
        grid=(n // ns,),
        in_specs=[pl.BlockSpec((ns, nh, dh), lambda b: (b, 0, 0)), kv_spec, kv_spec],
        out_specs=pl.BlockSpec((ns, nh, dh), lambda b: (b, 0, 0)),
        out_shape=jax.ShapeDtypeStruct((n, nh, dh), F32),
        compiler_params=_cparams(1),
        name="mem_decode",
    )(q, mem_k, mem_v)


def _proj_residual_kernel(x_ref, a_ref, w_ref, o_ref):
    o_ref[...] = x_ref[...] + _dot(a_ref[...].astype(BF16), w_ref[...])


def _proj_residual(x, a, w):
    return pl.pallas_call(
        _proj_residual_kernel,
        out_shape=jax.ShapeDtypeStruct(x.shape, F32),
        compiler_params=pltpu.CompilerParams(vmem_limit_bytes=VMEM_LIMIT),
        name="proj_residual",
    )(x, a, w)


FF_CHUNK = 256


DEC_PER_STEP = 4


def _ffn_prompt_kernel(pt_ref, x_ref, g_ref, wg_ref, wu_ref, cw_ref, cb_ref, wd_ref, gf_ref,
                       qd_ref, bd_ref, kc_ref, vc_ref,
                       y_ref, cs_ref, od_ref, act_ref, carry_ref, kbuf, vbuf, sem, *,
                       tm, layer, n_pages, page):
    step = pl.program_id(0) * pl.num_programs(1) + pl.program_id(1)
    n_steps = pl.num_programs(0) * pl.num_programs(1)

    def start_fetch(seq, slot):
        for p in range(n_pages):
            pg = pt_ref[seq * n_pages + p]
            pltpu.make_async_copy(kc_ref.at[layer, pg], kbuf.at[slot, p], sem.at[0, slot]).start()
            pltpu.make_async_copy(vc_ref.at[layer, pg], vbuf.at[slot, p], sem.at[1, slot]).start()

    def wait_fetch(slot):
        for p in range(n_pages):
            pltpu.make_async_copy(kc_ref.at[layer, 0], kbuf.at[slot, p], sem.at[0, slot]).wait()
            pltpu.make_async_copy(vc_ref.at[layer, 0], vbuf.at[slot, p], sem.at[1, slot]).wait()

    @pl.when(step == 0)
    def _():
        start_fetch(0, 0)
        start_fetch(1, 1)

    @pl.when(pl.program_id(1) == 0)
    def _():
        carry_ref[...] = jnp.zeros_like(carry_ref)

    x = x_ref[0]
    h = _rms(x, g_ref[...]).astype(BF16)
    row = lax.broadcasted_iota(jnp.int32, (8, FF_CHUNK), 0)
    d_ff = wg_ref.shape[1]
    n_chunks = d_ff // FF_CHUNK

    def ff_chunk(c):
        sl = slice(c * FF_CHUNK, (c + 1) * FF_CHUNK)
        g = _dot(h, wg_ref[:, sl])
        up = _dot(h, wu_ref[:, sl])
        prev = carry_ref[:, sl]
        p1, p2 = prev[7:8], prev[6:7]
        r1, r2 = pltpu.roll(g, 1, 0), pltpu.roll(g, 2, 0)
        g1 = jnp.concatenate([jnp.where(row == 0, p1, r1[:8]), r1[8:]], axis=0)
        g2 = jnp.concatenate(
            [jnp.where(row == 0, p2, jnp.where(row == 1, p1, r2[:8])), r2[8:]], axis=0)
        conv = cb_ref[:, sl] + cw_ref[0:1, sl] * g2 + cw_ref[1:2, sl] * g1 + cw_ref[2:3, sl] * g
        act_ref[:, sl] = (_gelu(conv) * up).astype(BF16)
        carry_ref[:, sl] = g[tm - 8:]
        cs_ref[0, :, sl] = g[tm - 2:]

    bounds = [(n_chunks * u + DEC_PER_STEP // 2) // DEC_PER_STEP for u in range(DEC_PER_STEP + 1)]
    for u in range(DEC_PER_STEP):
        slot = u % 2
        wait_fetch(slot)
        od_ref[u] = _sb_decode_one(qd_ref[u], bd_ref[...], kbuf.at[slot], vbuf.at[slot],
                                   n_pages, page)
        for c in range(bounds[u], bounds[u + 1]):
            ff_chunk(c)
        if u == DEC_PER_STEP - 1:
            x3 = x + _dot(act_ref[...], wd_ref[...])
            y_ref[0] = _rms(x3, gf_ref[...])
        if u + 2 < DEC_PER_STEP:
            start_fetch(step * DEC_PER_STEP + u + 2, slot)
        else:
            @pl.when(step + 1 < n_steps)
            def _():
                start_fetch((step + 1) * DEC_PER_STEP + u + 2 - DEC_PER_STEP, slot)


def _ffn_prompt(x, g, wg, wu, cw, cb, wd, gf, q_dec, bias_dec, cache_k, cache_v, page_table,
                layer, tm):
    b, t, d = x.shape
    d_ff = wg.shape[1]
    n_seq, n_pages = page_table.shape
    page = cache_k.shape[2]
    nh, dh = SB_HEADS, SB_HEAD_DIM
    n_tiles = t // tm
    assert n_seq == b * n_tiles * DEC_PER_STEP and DEC_PER_STEP % 2 == 0
    kc = jnp.transpose(cache_k, (0, 1, 3, 4, 2))
    vc = jnp.transpose(cache_v, (0, 1, 3, 4, 2))
    qt = jnp.transpose(q_dec.reshape(n_seq, nh, dh), (0, 2, 1))
    bias_lanes = jnp.broadcast_to(bias_dec[:, None], (nh, page))
    dec = pl.BlockSpec((DEC_PER_STEP, dh, nh), lambda bi, i, pt: (bi * n_tiles + i, 0, 0))
    grid_spec = pltpu.PrefetchScalarGridSpec(
        num_scalar_prefetch=1,
        grid=(b, n_tiles),
        in_specs=[pl.BlockSpec((1, tm, d), lambda bi, i, pt: (bi, i, 0))]
        + [_full(a.shape) for a in (g, wg, wu, cw, cb, wd, gf)]
        + [dec, _full(bias_lanes.shape),
           pl.BlockSpec(memory_space=pl.ANY), pl.BlockSpec(memory_space=pl.ANY)],
        out_specs=[
            pl.BlockSpec((1, tm, d), lambda bi, i, pt: (bi, i, 0)),
            pl.BlockSpec((1, 2, d_ff), lambda bi, i, pt: (bi, 0, 0)),
            dec,
        ],
        scratch_shapes=[
            pltpu.VMEM((tm, d_ff), BF16),
            pltpu.VMEM((8, d_ff), F32),
            pltpu.VMEM((2, n_pages, nh, dh, page), F32),
            pltpu.VMEM((2, n_pages, nh, dh, page), F32),
            pltpu.SemaphoreType.DMA((2, 2)),
        ],
    )
    y, cs, od = pl.pallas_call(
        functools.partial(_ffn_prompt_kernel, tm=tm, layer=layer, n_pages=n_pages, page=page),
        grid_spec=grid_spec,
        out_shape=[
            jax.ShapeDtypeStruct((b, t, d), F32),
            jax.ShapeDtypeStruct((b, 2, d_ff), F32),
            jax.ShapeDtypeStruct((n_seq, dh, nh), F32),
        ],
        compiler_params=_cparams(2, VMEM_LIMIT_FFN),
        name="ffn_prompt",
    )(page_table.reshape(-1), x, g, wg, wu, cw, cb, wd, gf, qt, bias_lanes, kc, vc)
    return y, cs, jnp.transpose(od, (0, 2, 1)).reshape(n_seq, nh * dh)


def _ffn_step_kernel(x_ref, g_ref, wg_ref, wu_ref, cw_ref, cb_ref, wd_ref, gf_ref, p0_ref, p1_ref,
                     y_ref, gate_ref, act_ref):
    x = x_ref[...]
    h = _rms(x, g_ref[...]).astype(BF16)
    d_ff = wg_ref.shape[1]
    for c in range(d_ff // FF_CHUNK):
        sl = slice(c * FF_CHUNK, (c + 1) * FF_CHUNK)
        g = _dot(h, wg_ref[:, sl])
        up = _dot(h, wu_ref[:, sl])
        conv = (cb_ref[:, sl] + cw_ref[0:1, sl] * p0_ref[:, sl] + cw_ref[1:2, sl] * p1_ref[:, sl]
                + cw_ref[2:3, sl] * g)
        act_ref[:, sl] = (_gelu(conv) * up).astype(BF16)
        gate_ref[:, sl] = g
    x3 = x + _dot(act_ref[...], wd_ref[...])
    y_ref[...] = _rms(x3, gf_ref[...])


def _ffn_step(x, g, wg, wu, cw, cb, wd, gf, p0, p1):
    n, d = x.shape
    d_ff = wg.shape[1]
    return pl.pallas_call(
        _ffn_step_kernel,
        out_shape=[jax.ShapeDtypeStruct((n, d), F32), jax.ShapeDtypeStruct((n, d_ff), F32)],
        scratch_shapes=[pltpu.VMEM((n, d_ff), BF16)],
        compiler_params=pltpu.CompilerParams(vmem_limit_bytes=VMEM_LIMIT),
        name="ffn_step",
    )(x, g, wg, wu, cw, cb, wd, gf, p0, p1)


def kernel(x_prompt, x_sample, cache_sb_k, cache_sb_v, page_table, state_ssm_re, state_ssm_im, state_conv, cache_mem_k, cache_mem_v, mem_prompt, g_mix, w_in, sb_bias, lam_re, lam_im, log_dt, b_re, b_im, c_re, c_im, d_skip, w_glu, b_glu, g_sb_out, g_ssm_out, w_out, g_mem_q, g_mem_kv, w_mq, w_mk, w_mv, w_mo, g_ffn, w_gate, w_up, conv_w, conv_b, w_down, g_final):
    depth = w_in.shape[0]
    n_p, t_p, d = x_prompt.shape
    n_s = x_sample.shape[0]
    assert x_sample.shape[1] == 1
    tm = 512
    q_scale = 1.0 / math.sqrt(SB_HEAD_DIM)
    row = lambda a: a.reshape(1, -1)
    gf = row(g_final)

    yp = x_prompt.reshape(n_p * t_p, d)
    ys = x_sample.reshape(n_s, d)
    outs = {k: [] for k in ("pk", "pv", "pre", "pim", "pconv", "pmk", "pmv",
                            "sk", "sv", "sre", "sim", "sconv")}
    y_prompt = y_sample = None
    for l in range(depth):
        w_in_b = w_in[l].astype(BF16)
        w_q, w_k, w_v, w_u = (w_in_b[:, j * SB_WIDTH:(j + 1) * SB_WIDTH] for j in range(4))
        w_out_b = w_out[l].astype(BF16)
        wo_a, wo_b = w_out_b[:SB_WIDTH], w_out_b[SB_WIDTH:]
        wglu_b = w_glu[l].astype(BF16)
        wmq, wmk, wmv, wmo = (w[l].astype(BF16) for w in (w_mq, w_mk, w_mv, w_mo))
        wg, wu, wd = (w[l].astype(BF16) for w in (w_gate, w_up, w_down))
        wb, wc, lamp = _s5_prepare(lam_re[l], lam_im[l], log_dt[l], b_re[l], b_im[l], c_re[l], c_im[l])
        s5_w = (wb, wc, lamp, row(d_skip[l]), wglu_b, row(b_glu[l]))
        ffn_w = (row(g_ffn[l]), wg, wu, conv_w[l], row(conv_b[l]), wd, gf)

        q_b, kt_f, kt_b, v_b, vt_f, u_f = _in_proj_prompt(
            yp.reshape(n_p, t_p, d), row(g_mix[l]), w_q, w_k, w_v, w_u, q_scale, tm)
        o_sb = _sb_prompt(q_b, kt_b, v_b, sb_bias[l])
        o_ssm, hre, him = _s5_prompt(u_f, *s5_w)
        n_mem = mem_prompt.shape[1]
        mk_f, mk_b, mv_f, mv_b = _norm_proj(
            mem_prompt.reshape(n_p * n_mem, d), row(g_mem_kv[l]), [wmk, wmv],
            [(True, True, 1.0), (True, True, 1.0)], n_mem)
        x2 = _mem_prompt(yp.reshape(n_p, t_p, d), o_sb, o_ssm, row(g_sb_out[l]), row(g_ssm_out[l]),
                         wo_a, wo_b, row(g_mem_q[l]), wmq,
                         mk_b.reshape(n_p, n_mem, d), mv_b.reshape(n_p, n_mem, d), wmo, tm)
        qs_f, ks_f, vs_f, us_f = _norm_proj(
            ys, row(g_mix[l]), [w_q, w_k, w_v, w_u],
            [(True, False, q_scale), (True, False, 1.0), (True, False, 1.0), (True, False, 1.0)], n_s)
        y3, cs_p, os_sb = _ffn_prompt(x2, *ffn_w, qs_f, sb_bias[l], cache_sb_k, cache_sb_v,
                                      page_table, l, tm)
        if l + 1 < depth:
            raise NotImplementedError("final norm is fused into the last layer's FFN")
        y_prompt = y3
        to_cache = lambda a: jnp.transpose(a.reshape(n_p, SB_HEADS, SB_HEAD_DIM, t_p), (0, 3, 1, 2))
        outs["pk"].append(to_cache(kt_f))
        outs["pv"].append(to_cache(vt_f))
        outs["pre"].append(hre.reshape(n_p, SSM_GROUPS, SSM_STATE))
        outs["pim"].append(him.reshape(n_p, SSM_GROUPS, SSM_STATE))
        outs["pconv"].append(cs_p)
        outs["pmk"].append(mk_f.reshape(n_p, n_mem, MEM_HEADS, MEM_HEAD_DIM))
        outs["pmv"].append(mv_f.reshape(n_p, n_mem, MEM_HEADS, MEM_HEAD_DIM))

        os_ssm, hsr, hsi = _s5_step(us_f, state_ssm_re[l].reshape(n_s, SSM_CH),
                                    state_ssm_im[l].reshape(n_s, SSM_CH), *s5_w)
        x1s = _mix_out(ys, os_sb, os_ssm, row(g_sb_out[l]), row(g_ssm_out[l]), wo_a, wo_b, n_s)
        (qm_f,) = _norm_proj(x1s, row(g_mem_q[l]), [wmq],
                             [(True, False, 1.0 / math.sqrt(MEM_HEAD_DIM))], n_s)
        om = _mem_decode(qm_f.reshape(n_s, MEM_HEADS, MEM_HEAD_DIM), cache_mem_k, cache_mem_v,
                         l).reshape(n_s, d)
        x2s = _proj_residual(x1s, om, wmo)
        y3s, gate_s = _ffn_step(x2s, *ffn_w, state_conv[l][:, 0], state_conv[l][:, 1])
        y_sample = y3s
        outs["sk"].append(ks_f.reshape(n_s, 1, SB_HEADS, SB_HEAD_DIM))
        outs["sv"].append(vs_f.reshape(n_s, 1, SB_HEADS, SB_HEAD_DIM))
        outs["sre"].append(hsr.reshape(n_s, SSM_GROUPS, SSM_STATE))
        outs["sim"].append(hsi.reshape(n_s, SSM_GROUPS, SSM_STATE))
        outs["sconv"].append(jnp.stack([state_conv[l][:, 1], gate_s], axis=1))

    st = lambda k: jnp.stack(outs[k])
    return (y_prompt, y_sample.reshape(n_s, 1, d),
            st("pk"), st("pv"), st("pre"), st("pim"), st("pconv"), st("pmk"), st("pmv"),
            st("sk"), st("sv"), st("sre"), st("sim"), st("sconv"))
```

```python
import functools
import math

import numpy as np
import jax
import jax.numpy as jnp
from jax import lax
from jax.experimental import pallas as pl
from jax.experimental.pallas import tpu as pltpu

F32 = jnp.float32
BF16 = jnp.bfloat16

EPS = 1e-6
SB_HEADS = 8
SB_HEAD_DIM = 64
SB_WIDTH = SB_HEADS * SB_HEAD_DIM
SSM_GROUPS = 32
SSM_GROUP = 16
SSM_STATE = 64
SSM_WIDTH = SSM_GROUPS * SSM_GROUP
SSM_CH = SSM_GROUPS * SSM_STATE
MEM_HEADS = 4
MEM_HEAD_DIM = 256
LANES = 128
N_STRIPS = SSM_CH // LANES
SCAN_ROWS = 128
S5_WINDOW = 8
S5_POW_ROWS = 16
VMEM_LIMIT = 48 * 1024 * 1024
VMEM_LIMIT_FFN = 56 * 1024 * 1024

_NT = (((1,), (1,)), ((), ()))


def _cparams(n_axes, vmem_limit=VMEM_LIMIT):
    return pltpu.CompilerParams(
        dimension_semantics=("arbitrary",) * n_axes, vmem_limit_bytes=vmem_limit)


def _rms(x, g):
    ms = jnp.mean(x * x, axis=-1, keepdims=True)
    return x * lax.rsqrt(ms + EPS) * g


def _gelu(x):
    c = math.sqrt(2.0 / math.pi)
    return x * (0.5 + 0.5 * jnp.tanh(x * (c + (c * 0.044715) * (x * x))))


def _dot(a, b):
    return jnp.dot(a, b, preferred_element_type=F32)


def _full(shape):
    n = len(shape)
    return pl.BlockSpec(shape, lambda *_: (0,) * n)


def _norm_proj_kernel(x_ref, g_ref, *refs, out_kinds):
    n_w = len(out_kinds)
    w_refs, out_refs = refs[:n_w], refs[n_w:]
    h = _rms(x_ref[...], g_ref[...]).astype(BF16)
    oi = 0
    for w_ref, (want_f32, want_bf16, scale) in zip(w_refs, out_kinds):
        r = _dot(h, w_ref[...])
        if scale != 1.0:
            r = r * scale
        if want_f32:
            out_refs[oi][...] = r
            oi += 1
        if want_bf16:
            out_refs[oi][...] = r.astype(BF16)
            oi += 1


def _norm_proj(x, g, ws, out_kinds, tm):
    m, d = x.shape
    out_shapes, out_specs = [], []
    for w, (want_f32, want_bf16, _) in zip(ws, out_kinds):
        n = w.shape[1]
        for want, dt in ((want_f32, F32), (want_bf16, BF16)):
            if want:
                out_shapes.append(jax.ShapeDtypeStruct((m, n), dt))
                out_specs.append(pl.BlockSpec((tm, n), lambda i: (i, 0)))
    return pl.pallas_call(
        functools.partial(_norm_proj_kernel, out_kinds=tuple(out_kinds)),
        grid=(m // tm,),
        in_specs=[pl.BlockSpec((tm, d), lambda i: (i, 0)), _full(g.shape)]
        + [_full(w.shape) for w in ws],
        out_specs=out_specs,
        out_shape=out_shapes,
        compiler_params=_cparams(1),
        name="norm_proj",
    )(x, g, *ws)


def _in_proj_prompt_kernel(x_ref, g_ref, wq_ref, wkt_ref, wv_ref, wvt_ref, wu_ref,
                           q_ref, kt_ref, ktb_ref, vb_ref, vt_ref, u_ref, *, q_scale):
    h = _rms(x_ref[0], g_ref[...]).astype(BF16)
    q_ref[0] = (_dot(h, wq_ref[...]) * q_scale).astype(BF16)
    kt = lax.dot_general(wkt_ref[...], h, _NT, preferred_element_type=F32)
    kt_ref[0] = kt
    ktb_ref[0] = kt.astype(BF16)
    vb_ref[0] = _dot(h, wv_ref[...]).astype(BF16)
    vt_ref[0] = lax.dot_general(wvt_ref[...], h, _NT, preferred_element_type=F32)
    u_ref[0] = _dot(h, wu_ref[...])


def _in_proj_prompt(x, g, wq, wk, wv, wu, q_scale, tm):
    b, t, d = x.shape
    w = wq.shape[1]
    tok = pl.BlockSpec((1, tm, w), lambda bi, i: (bi, i, 0))
    tr = pl.BlockSpec((1, w, tm), lambda bi, i: (bi, 0, i))
    ws = (wq, wk.T, wv, wv.T, wu)
    return pl.pallas_call(
        functools.partial(_in_proj_prompt_kernel, q_scale=q_scale),
        grid=(b, t // tm),
        in_specs=[pl.BlockSpec((1, tm, d), lambda bi, i: (bi, i, 0)), _full(g.shape)]
        + [_full(a.shape) for a in ws],
        out_specs=[tok, tr, tr, tok, tr, tok],
        out_shape=[
            jax.ShapeDtypeStruct((b, t, w), BF16),
            jax.ShapeDtypeStruct((b, w, t), F32),
            jax.ShapeDtypeStruct((b, w, t), BF16),
            jax.ShapeDtypeStruct((b, t, w), BF16),
            jax.ShapeDtypeStruct((b, w, t), F32),
            jax.ShapeDtypeStruct((b, t, w), F32),
        ],
        compiler_params=_cparams(2),
        name="in_proj_prompt",
    )(x, g, *ws)


SB_SUB = 256
LOG2E = 1.4426950408889634


def _softplus_and_logsig(z):
    e = jnp.exp2(jnp.abs(z) * -LOG2E)
    sp = jnp.maximum(z, 0.0) + jnp.log(1.0 + e)
    return sp, z - sp


def _bf16_pieces(x, n):
    out = []
    for _ in range(n):
        p = x.astype(BF16).astype(F32)
        out.append(p)
        x = x - p
    return out


def _sb_prompt_kernel(bias_ref, q_ref, k_ref, v_ref, o_ref, acc_ref, r_ref, *, tq, groups):
    i = pl.program_id(2)
    lane = lax.broadcasted_iota(jnp.int32, (tq, LANES), 1)
    left = lane < SB_HEAD_DIM
    nsub = tq // SB_SUB
    qqs = []
    for g in range(groups):
        q = q_ref[0, :, g * LANES:(g + 1) * LANES].astype(F32)
        bias_rows = jnp.concatenate([jnp.broadcast_to(bias_ref[g, 0:1, :], (tq, LANES)),
                                     jnp.broadcast_to(bias_ref[g, 1:2, :], (tq, LANES))], axis=0)
        qq = jnp.concatenate([jnp.where(left, q, 0.0), jnp.where(left, 0.0, q)], axis=0)
        qqs.append(jnp.concatenate([qq, bias_rows], axis=1).astype(BF16))
    k_ones = jnp.ones((LANES, tq), BF16)
    rr = lax.broadcasted_iota(jnp.int32, (SB_SUB, SB_SUB), 0)
    cc = lax.broadcasted_iota(jnp.int32, (SB_SUB, SB_SUB), 1)
    later = jnp.where(rr > cc, 1.0, 0.0).astype(BF16)
    qrow = lax.broadcasted_iota(jnp.int32, (tq, SB_SUB), 0)
    kcol = lax.broadcasted_iota(jnp.int32, (tq, SB_SUB), 1)
    acc_ref[...] = jnp.zeros_like(acc_ref)
    r_ref[...] = jnp.zeros_like(r_ref)

    def both(fn, x):
        return jnp.concatenate([fn(x[:tq]), fn(x[tq:])], axis=0)

    def run(j, masked):
        off = pl.multiple_of(j * tq, tq)
        for g in range(groups):
            gl = slice(g * LANES, (g + 1) * LANES)
            kb = jnp.concatenate([k_ref[0, gl, pl.ds(off, tq)], k_ones], axis=0)
            vb = v_ref[0, pl.ds(off, tq), gl]
            z = _dot(qqs[g], kb)
            sp, lsig = _softplus_and_logsig(z)
            r = r_ref[g]
            parts = [None] * nsub
            for n in reversed(range(nsub)):
                sl = slice(n * SB_SUB, (n + 1) * SB_SUB)
                spn = sp[:, sl]
                if masked:
                    causal = (kcol + n * SB_SUB) < qrow
                    spn = both(lambda x: jnp.where(causal, x, 0.0), spn)
                cs = _dot(spn.astype(BF16), later)
                a = jnp.exp((lsig[:, sl] - jnp.concatenate([r] * (SB_SUB // LANES), axis=1)) - cs)
                if masked:
                    a = both(lambda x: jnp.where(causal, x, 0.0), a)
                parts[n] = a.astype(BF16)
                r = r + (cs[:, 0:1] + spn[:, 0:1])
            r_ref[g] = r
            pv = _dot(jnp.concatenate(parts, axis=1), vb)
            acc_ref[:, gl] += jnp.where(left, pv[:tq], pv[tq:])

    run(i, True)

    def body(jj, carry):
        run(i - jj, False)
        return carry

    lax.fori_loop(1, i + 1, body, 0)
    o_ref[0] = acc_ref[...]


def _sb_prompt(q, k, v, bias, tq=2 * SB_SUB, groups=4):
    b, t, w = q.shape
    gw = groups * LANES
    pieces = jnp.stack(_bf16_pieces(bias, 3), axis=-1)
    bias = jnp.pad(pieces, ((0, 0), (0, LANES - 3))).reshape(w // LANES, 2, LANES)
    return pl.pallas_call(
        functools.partial(_sb_prompt_kernel, tq=tq, groups=groups),
        grid=(b, w // gw, t // tq),
        in_specs=[
            pl.BlockSpec((groups, 2, LANES), lambda bi, h, i: (h, 0, 0)),
            pl.BlockSpec((1, tq, gw), lambda bi, h, i: (bi, i, h)),
            pl.BlockSpec((1, gw, t), lambda bi, h, i: (bi, h, 0)),
            pl.BlockSpec((1, t, gw), lambda bi, h, i: (bi, 0, h)),
        ],
        out_specs=pl.BlockSpec((1, tq, gw), lambda bi, h, i: (bi, i, h)),
        out_shape=jax.ShapeDtypeStruct((b, t, w), F32),
        scratch_shapes=[pltpu.VMEM((tq, gw), F32), pltpu.VMEM((groups, 2 * tq, LANES), F32)],
        compiler_params=_cparams(3),
        name="sb_prompt",
    )(bias, q, k, v)


def _sb_decode_one(qt, bias, k_pages, v_pages, n_pages, page):
    nh, dh = SB_HEADS, SB_HEAD_DIM
    past = n_pages * page
    qb = [jnp.broadcast_to(qt[:, h:h + 1], (dh, page)) for h in range(nh)]
    zpages = []
    for p in range(n_pages):
        rows = [jnp.sum(k_pages[p, h] * qb[h], axis=0, keepdims=True) for h in range(nh)]
        zpages.append(jnp.concatenate(rows, axis=0))
    z = jnp.concatenate(zpages, axis=1) + jnp.concatenate([bias] * n_pages, axis=1)
    sp, lsig = _softplus_and_logsig(z)

    lane = lax.broadcasted_iota(jnp.int32, (nh, past), 1)
    incl = sp
    step = 1
    while step < past:
        incl = incl + jnp.where(lane < past - step, pltpu.roll(incl, past - step, 1), 0.0)
        step *= 2
    a = jnp.exp(lsig - (incl - sp))

    accs = [jnp.zeros((dh, page), F32) for _ in range(nh)]
    for p in range(n_pages):
        for h in range(nh):
            arow = jnp.broadcast_to(a[h:h + 1, p * page:(p + 1) * page], (dh, page))
            accs[h] = accs[h] + arow * v_pages[p, h]
    return jnp.concatenate([jnp.sum(acc, axis=1, keepdims=True) for acc in accs], axis=1)


def _s5_prep_kernel(lre_ref, lim_ref, ldt_ref, btr_ref, bti_ref, pre_ref, pim_ref, bbr_ref, bbi_ref):
    lre, lim = lre_ref[...], lim_ref[...]
    dt = jnp.exp(ldt_ref[...])
    mag = jnp.exp(lre * dt)
    br = mag * jnp.cos(lim * dt)
    bi = mag * jnp.sin(lim * dt)
    den = lre * lre + lim * lim
    nr, ni = br - 1.0, bi
    cr = (nr * lre + ni * lim) / den
    ci = (ni * lre - nr * lim) / den
    btr, bti = btr_ref[...], bti_ref[...]
    for k in range(S5_WINDOW):
        bbr_ref[k * SSM_GROUP:(k + 1) * SSM_GROUP, :] = cr * btr - ci * bti
        bbi_ref[k * SSM_GROUP:(k + 1) * SSM_GROUP, :] = cr * bti + ci * btr
        cr, ci = cr * br - ci * bi, cr * bi + ci * br
    pr, pi_ = br, bi
    for k in range(S5_WINDOW):
        pre_ref[k:k + 1, :] = pr
        pim_ref[k:k + 1, :] = pi_
        if k + 1 < S5_WINDOW:
            pr, pi_ = pr * br - pi_ * bi, pr * bi + pi_ * br
    for k in range(S5_WINDOW, S5_POW_ROWS):
        pr, pi_ = pr * pr - pi_ * pi_, 2.0 * pr * pi_
        pre_ref[k:k + 1, :] = pr
        pim_ref[k:k + 1, :] = pi_


def _s5_prepare(lam_re, lam_im, log_dt, b_re, b_im, c_re, c_im):
    ch = SSM_CH
    lre = lam_re.reshape(1, ch)
    lim = lam_im.reshape(1, ch)
    ldt = jnp.repeat(log_dt, SSM_STATE).reshape(1, ch)
    btr = b_re.reshape(ch, SSM_GROUP).T
    bti = b_im.reshape(ch, SSM_GROUP).T
    pre, pim, bbr, bbi = pl.pallas_call(
        _s5_prep_kernel,
        out_shape=[jax.ShapeDtypeStruct((S5_POW_ROWS, ch), F32)] * 2
        + [jax.ShapeDtypeStruct((S5_WINDOW * SSM_GROUP, ch), F32)] * 2,
        name="s5_prep",
    )(lre, lim, ldt, btr, bti)

    s_idx = np.arange(N_STRIPS)[:, None, None]
    j_idx = np.arange(LANES)[None, :, None]
    c_idx = np.arange(LANES)[None, None, :]
    grp_of_ch = (LANES * (s_idx // 4) + j_idx) // SSM_GROUP
    grp_of_state = (LANES * s_idx + c_idx) // SSM_STATE
    mask = jnp.asarray(grp_of_ch == grp_of_state, F32)

    def b_strips(bb):
        t = bb.reshape(SSM_GROUP, N_STRIPS, LANES).transpose(1, 0, 2)
        return jnp.tile(t, (1, LANES // SSM_GROUP, 1)) * mask

    def b_windows(bb):
        per_k = [b_strips(bb[k * SSM_GROUP:(k + 1) * SSM_GROUP]) for k in range(S5_WINDOW)]
        return jnp.concatenate(per_k, axis=1)

    wb = jnp.concatenate([b_windows(bbr), b_windows(bbi)], axis=2).astype(BF16)

    def c_strips(c):
        t = c.transpose(0, 2, 1).reshape(N_STRIPS, LANES, SSM_GROUP)
        return jnp.tile(t, (1, 1, LANES // SSM_GROUP)) * mask.transpose(0, 2, 1)

    wc = jnp.concatenate([c_strips(c_re), -c_strips(c_im)], axis=1).astype(BF16)

    def pw(p):
        return p.reshape(S5_POW_ROWS, N_STRIPS, LANES).transpose(1, 0, 2)

    lamp = jnp.concatenate([pw(pre), pw(pim)], axis=2)
    nb = SSM_WIDTH // LANES
    wb = wb.reshape(nb, 4, S5_WINDOW * LANES, 2 * LANES).transpose(0, 2, 1, 3)
    wb = wb.reshape(nb, S5_WINDOW * LANES, 8 * LANES)
    wc = wc.reshape(nb, 8 * LANES, LANES)
    return wb, wc, lamp


def _scan_rows(xr, xi, lam):
    s, k = S5_WINDOW, S5_WINDOW - 1
    while s < SCAN_ROWS:
        ar, ai = lam[k:k + 1, :LANES], lam[k:k + 1, LANES:]
        pr, pi_ = xr[:-s], xi[:-s]
        nr = xr[s:] + ar * pr - ai * pi_
        ni = xi[s:] + ar * pi_ + ai * pr
        xr = jnp.concatenate([xr[:s], nr], axis=0)
        xi = jnp.concatenate([xi[:s], ni], axis=0)
        s, k = 2 * s, k + 1
    return xr, xi


def _glu_out(y, wglu_ref, bglu_ref):
    y = _gelu(y)
    return y * jax.nn.sigmoid(_dot(y.astype(BF16), wglu_ref[...]) + bglu_ref[...])


def _s5_prompt_kernel(u_ref, wb_ref, wc_ref, lamp_ref, dskip_ref, wglu_ref, bglu_ref,
                      o_ref, hre_ref, him_ref, carry_ref, *, chunk):
    c = pl.program_id(1)

    @pl.when(c == 0)
    def _():
        carry_ref[...] = jnp.zeros_like(carry_ref)

    u = u_ref[0]
    seg_pos = lax.broadcasted_iota(jnp.int32, (chunk, LANES), 0) % SCAN_ROWS
    y_blocks = []
    for kb in range(SSM_WIDTH // LANES):
        ukb = u[:, kb * LANES:(kb + 1) * LANES]
        lagged = [ukb] + [jnp.where(seg_pos >= k, pltpu.roll(ukb, k, 0), 0.0)
                          for k in range(1, S5_WINDOW)]
        uwin = jnp.concatenate(lagged, axis=1).astype(BF16)
        res4 = _dot(uwin, wb_ref[kb])
        states = []
        for j in range(4):
            s = 4 * kb + j
            res = res4[:, 2 * j * LANES:2 * (j + 1) * LANES]
            lam = lamp_ref[s]
            prev = carry_ref[s]
            cr, ci = prev[7:8, :LANES], prev[7:8, LANES:]
            lr, li = lam[:S5_WINDOW, :LANES], lam[:S5_WINDOW, LANES:]
            parts = []
            for h in range(chunk // SCAN_ROWS):
                rs = slice(h * SCAN_ROWS, (h + 1) * SCAN_ROWS)
                xr, xi = res[rs, :LANES], res[rs, LANES:]
                xr = jnp.concatenate([xr[:S5_WINDOW] + (lr * cr - li * ci), xr[S5_WINDOW:]], axis=0)
                xi = jnp.concatenate([xi[:S5_WINDOW] + (lr * ci + li * cr), xi[S5_WINDOW:]], axis=0)
                xr, xi = _scan_rows(xr, xi, lam)
                cr, ci = xr[SCAN_ROWS - 1:], xi[SCAN_ROWS - 1:]
                parts.append(jnp.concatenate([xr, xi], axis=1).astype(BF16))
            carry_ref[s] = jnp.concatenate([xr[SCAN_ROWS - 8:], xi[SCAN_ROWS - 8:]], axis=1)
            states.append(jnp.concatenate(parts, axis=0))
        y_blocks.append(_dot(jnp.concatenate(states, axis=1), wc_ref[kb]))
    y = jnp.concatenate(y_blocks, axis=1) + dskip_ref[...] * u
    o_ref[0] = _glu_out(y, wglu_ref, bglu_ref)

    @pl.when(c == pl.num_programs(1) - 1)
    def _():
        for s in range(N_STRIPS):
            last = carry_ref[s]
            hre_ref[0, :, s * LANES:(s + 1) * LANES] = last[7:8, :LANES]
            him_ref[0, :, s * LANES:(s + 1) * LANES] = last[7:8, LANES:]


def _s5_prompt(u, wb, wc, lamp, dskip, wglu, bglu, chunk=1024):
    b, t, w = u.shape
    ch = SSM_CH
    return pl.pallas_call(
        functools.partial(_s5_prompt_kernel, chunk=chunk),
        grid=(b, t // chunk),
        in_specs=[pl.BlockSpec((1, chunk, w), lambda bi, c: (bi, c, 0))]
        + [_full(a.shape) for a in (wb, wc, lamp, dskip, wglu, bglu)],
        out_specs=[
            pl.BlockSpec((1, chunk, w), lambda bi, c: (bi, c, 0)),
            pl.BlockSpec((1, 1, ch), lambda bi, c: (bi, 0, 0)),
            pl.BlockSpec((1, 1, ch), lambda bi, c: (bi, 0, 0)),
        ],
        out_shape=[
            jax.ShapeDtypeStruct((b, t, w), F32),
            jax.ShapeDtypeStruct((b, 1, ch), F32),
            jax.ShapeDtypeStruct((b, 1, ch), F32),
        ],
        scratch_shapes=[pltpu.VMEM((N_STRIPS, 8, 2 * LANES), F32)],
        compiler_params=_cparams(2),
        name="s5_prompt",
    )(u, wb, wc, lamp, dskip, wglu, bglu)


def _s5_step_kernel(u_ref, h0r_ref, h0i_ref, wb_ref, wc_ref, lamp_ref, dskip_ref, wglu_ref,
                    bglu_ref, o_ref, hre_ref, him_ref):
    u = u_ref[...]
    ub = u.astype(BF16)
    y_blocks = []
    for kb in range(SSM_WIDTH // LANES):
        res4 = _dot(ub[:, kb * LANES:(kb + 1) * LANES], wb_ref[kb, 0:LANES, :])
        states = []
        for j in range(4):
            s = 4 * kb + j
            sl = slice(s * LANES, (s + 1) * LANES)
            res = res4[:, 2 * j * LANES:2 * (j + 1) * LANES]
            lam = lamp_ref[s]
            lr, li = lam[0:1, :LANES], lam[0:1, LANES:]
            h0r, h0i = h0r_ref[:, sl], h0i_ref[:, sl]
            xr = res[:, :LANES] + lr * h0r - li * h0i
            xi = res[:, LANES:] + lr * h0i + li * h0r
            hre_ref[:, sl] = xr
            him_ref[:, sl] = xi
            states.append(jnp.concatenate([xr, xi], axis=1).astype(BF16))
        y_blocks.append(_dot(jnp.concatenate(states, axis=1), wc_ref[kb]))
    y = jnp.concatenate(y_blocks, axis=1) + dskip_ref[...] * u
    o_ref[...] = _glu_out(y, wglu_ref, bglu_ref)


def _s5_step(u, h0r, h0i, wb, wc, lamp, dskip, wglu, bglu):
    n, w = u.shape
    return pl.pallas_call(
        _s5_step_kernel,
        out_shape=[
            jax.ShapeDtypeStruct((n, w), F32),
            jax.ShapeDtypeStruct((n, SSM_CH), F32),
            jax.ShapeDtypeStruct((n, SSM_CH), F32),
        ],
        compiler_params=pltpu.CompilerParams(vmem_limit_bytes=VMEM_LIMIT),
        name="s5_step",
    )(u, h0r, h0i, wb, wc, lamp, dskip, wglu, bglu)


def _mix_out_kernel(x_ref, a_ref, b_ref, ga_ref, gb_ref, wa_ref, wb_ref, o_ref):
    ha = _rms(a_ref[...], ga_ref[...]).astype(BF16)
    hb = _rms(b_ref[...], gb_ref[...]).astype(BF16)
    o_ref[...] = x_ref[...] + _dot(ha, wa_ref[...]) + _dot(hb, wb_ref[...])


def _mix_out(x, a, b, ga, gb, wa, wb, tm):
    m, d = x.shape
    w = a.shape[1]
    return pl.pallas_call(
        _mix_out_kernel,
        grid=(m // tm,),
        in_specs=[
            pl.BlockSpec((tm, d), lambda i: (i, 0)),
            pl.BlockSpec((tm, w), lambda i: (i, 0)),
            pl.BlockSpec((tm, w), lambda i: (i, 0)),
        ] + [_full(t.shape) for t in (ga, gb, wa, wb)],
        out_specs=pl.BlockSpec((tm, d), lambda i: (i, 0)),
        out_shape=jax.ShapeDtypeStruct((m, d), F32),
        compiler_params=_cparams(1),
        name="mix_out",
    )(x, a, b, ga, gb, wa, wb)


def _mem_prompt_kernel(x0_ref, a_ref, b_ref, ga_ref, gb_ref, wa_ref, wb_ref,
                       g_ref, wq_ref, mk_ref, mv_ref, wo_ref, o_ref, ob_ref):
    ha = _rms(a_ref[0], ga_ref[...]).astype(BF16)
    hb = _rms(b_ref[0], gb_ref[...]).astype(BF16)
    x = x0_ref[0] + _dot(ha, wa_ref[...]) + _dot(hb, wb_ref[...])
    q = _dot(_rms(x, g_ref[...]).astype(BF16), wq_ref[...])
    qb = (q * (1.0 / math.sqrt(MEM_HEAD_DIM))).astype(BF16)
    for h in range(MEM_HEADS):
        sl = slice(h * MEM_HEAD_DIM, (h + 1) * MEM_HEAD_DIM)
        s = lax.dot_general(qb[:, sl], mk_ref[0, :, sl], _NT, preferred_element_type=F32)
        e = jnp.exp(s - jnp.max(s, axis=-1, keepdims=True))
        o = _dot(e.astype(BF16), mv_ref[0, :, sl]) / jnp.sum(e, axis=-1, keepdims=True)
        ob_ref[:, sl] = o.astype(BF16)
    o_ref[0] = x + _dot(ob_ref[...], wo_ref[...])


def _mem_prompt(x, a, b_, ga, gb, wa, wb, g, wq, mk, mv, wo, tm):
    b, t, d = x.shape
    w = a.shape[2]
    nk = mk.shape[1]
    tok = lambda width: pl.BlockSpec((1, tm, width), lambda bi, i: (bi, i, 0))
    mem = pl.BlockSpec((1, nk, d), lambda bi, i: (bi, 0, 0))
    return pl.pallas_call(
        _mem_prompt_kernel,
        grid=(b, t // tm),
        in_specs=[tok(d), tok(w), tok(w)] + [_full(p.shape) for p in (ga, gb, wa, wb, g, wq)]
        + [mem, mem, _full(wo.shape)],
        out_specs=tok(d),
        out_shape=jax.ShapeDtypeStruct((b, t, d), F32),
        scratch_shapes=[pltpu.VMEM((tm, d), BF16)],
        compiler_params=_cparams(2),
        name="mem_prompt",
    )(x, a, b_, ga, gb, wa, wb, g, wq, mk, mv, wo)


MEM_SEQS_PER_STEP = 8


def _mem_decode_kernel(q_ref, k_ref, v_ref, o_ref, *, n_mem):
    def tree(fn, xs):
        while len(xs) > 1:
            xs = [fn(xs[j], xs[j + 1]) for j in range(0, len(xs) - 1, 2)] + xs[len(xs) & ~1:]
        return xs[0]

    nh, dh = MEM_HEADS, MEM_HEAD_DIM
    for i in range(MEM_SEQS_PER_STEP):
        q = q_ref[i]
        q2 = jnp.concatenate([q, q], axis=0)
        pair = lambda ref, t: ref[0, i, pl.ds(2 * t, 2)].reshape(2 * nh, dh)
        s = [jnp.sum(pair(k_ref, t) * q2, axis=-1, keepdims=True) for t in range(n_mem // 2)]
        m = tree(jnp.maximum, s)
        m = jnp.maximum(m[:nh], m[nh:])
        m = jnp.concatenate([m, m], axis=0)
        e = [jnp.exp(st - m) for st in s]
        denom = tree(jnp.add, e)
        acc = tree(jnp.add, [e[t] * pair(v_ref, t) for t in range(n_mem // 2)])
        o_ref[i] = (acc[:nh] + acc[nh:]) / (denom[:nh] + denom[nh:])


def _mem_decode(q, mem_k, mem_v, layer):
    _, n, nk, nh, dh = mem_k.shape
    ns = MEM_SEQS_PER_STEP
    kv_spec = pl.BlockSpec((1, ns, nk, nh, dh), lambda b: (layer, b, 0, 0, 0))
    return pl.pallas_call(
        functools.partial(_mem_decode_kernel, n_mem=nk),
        grid=(n // ns,),
        in_specs=[pl.BlockSpec((ns, nh, dh), lambda b: (b, 0, 0)), kv_spec, kv_spec],
        out_specs=pl.BlockSpec((ns, nh, dh), lambda b: (b, 0, 0)),
        out_shape=jax.ShapeDtypeStruct((n, nh, dh), F32),
        compiler_params=_cparams(1),
        name="mem_decode",
    )(q, mem_k, mem_v)


def _proj_residual_kernel(x_ref, a_ref, w_ref, o_ref):
    o_ref[...] = x_ref[...] + _dot(a_ref[...].astype(BF16), w_ref[...])


def _proj_residual(x, a, w):
    return pl.pallas_call(
        _proj_residual_kernel,
        out_shape=jax.ShapeDtypeStruct(x.shape, F32),
        compiler_params=pltpu.CompilerParams(vmem_limit_bytes=VMEM_LIMIT),
        name="proj_residual",
    )(x, a, w)


FF_CHUNK = 256


DEC_PER_STEP = 4


def _ffn_prompt_kernel(pt_ref, x_ref, g_ref, wg_ref, wu_ref, cw_ref, cb_ref, wd_ref, gf_ref,
                       qd_ref, bd_ref, kc_ref, vc_ref,
                       y_ref, cs_ref, od_ref, act_ref, carry_ref, kbuf, vbuf, sem, *,
                       tm, layer, n_pages, page):
    step = pl.program_id(0) * pl.num_programs(1) + pl.program_id(1)
    n_steps = pl.num_programs(0) * pl.num_programs(1)

    def start_fetch(seq, slot):
        for p in range(n_pages):
            pg = pt_ref[seq * n_pages + p]
            pltpu.make_async_copy(kc_ref.at[layer, pg], kbuf.at[slot, p], sem.at[0, slot]).start()
            pltpu.make_async_copy(vc_ref.at[layer, pg], vbuf.at[slot, p], sem.at[1, slot]).start()

    def wait_fetch(slot):
        for p in range(n_pages):
            pltpu.make_async_copy(kc_ref.at[layer, 0], kbuf.at[slot, p], sem.at[0, slot]).wait()
            pltpu.make_async_copy(vc_ref.at[layer, 0], vbuf.at[slot, p], sem.at[1, slot]).wait()

    @pl.when(step == 0)
    def _():
        start_fetch(0, 0)
        start_fetch(1, 1)

    @pl.when(pl.program_id(1) == 0)
    def _():
        carry_ref[...] = jnp.zeros_like(carry_ref)

    x = x_ref[0]
    h = _rms(x, g_ref[...]).astype(BF16)
    row = lax.broadcasted_iota(jnp.int32, (8, FF_CHUNK), 0)
    d_ff = wg_ref.shape[1]
    n_chunks = d_ff // FF_CHUNK

    def ff_chunk(c):
        sl = slice(c * FF_CHUNK, (c + 1) * FF_CHUNK)
        g = _dot(h, wg_ref[:, sl])
        up = _dot(h, wu_ref[:, sl])
        prev = carry_ref[:, sl]
        p1, p2 = prev[7:8], prev[6:7]
        r1, r2 = pltpu.roll(g, 1, 0), pltpu.roll(g, 2, 0)
        g1 = jnp.concatenate([jnp.where(row == 0, p1, r1[:8]), r1[8:]], axis=0)
        g2 = jnp.concatenate(
            [jnp.where(row == 0, p2, jnp.where(row == 1, p1, r2[:8])), r2[8:]], axis=0)
        conv = cb_ref[:, sl] + cw_ref[0:1, sl] * g2 + cw_ref[1:2, sl] * g1 + cw_ref[2:3, sl] * g
        act_ref[:, sl] = (_gelu(conv) * up).astype(BF16)
        carry_ref[:, sl] = g[tm - 8:]
        cs_ref[0, :, sl] = g[tm - 2:]

    bounds = [(n_chunks * u + DEC_PER_STEP // 2) // DEC_PER_STEP for u in range(DEC_PER_STEP + 1)]
    for u in range(DEC_PER_STEP):
        slot = u % 2
        wait_fetch(slot)
        od_ref[u] = _sb_decode_one(qd_ref[u], bd_ref[...], kbuf.at[slot], vbuf.at[slot],
                                   n_pages, page)
        for c in range(bounds[u], bounds[u + 1]):
            ff_chunk(c)
        if u == DEC_PER_STEP - 1:
            x3 = x + _dot(act_ref[...], wd_ref[...])
            y_ref[0] = _rms(x3, gf_ref[...])
        if u + 2 < DEC_PER_STEP:
            start_fetch(step * DEC_PER_STEP + u + 2, slot)
        else:
            @pl.when(step + 1 < n_steps)
            def _():
                start_fetch((step + 1) * DEC_PER_STEP + u + 2 - DEC_PER_STEP, slot)


def _ffn_prompt(x, g, wg, wu, cw, cb, wd, gf, q_dec, bias_dec, cache_k, cache_v, page_table,
                layer, tm):
    b, t, d = x.shape
    d_ff = wg.shape[1]
    n_seq, n_pages = page_table.shape
    page = cache_k.shape[2]
    nh, dh = SB_HEADS, SB_HEAD_DIM
    n_tiles = t // tm
    assert n_seq == b * n_tiles * DEC_PER_STEP and DEC_PER_STEP % 2 == 0
    kc = jnp.transpose(cache_k, (0, 1, 3, 4, 2))
    vc = jnp.transpose(cache_v, (0, 1, 3, 4, 2))
    qt = jnp.transpose(q_dec.reshape(n_seq, nh, dh), (0, 2, 1))
    bias_lanes = jnp.broadcast_to(bias_dec[:, None], (nh, page))
    dec = pl.BlockSpec((DEC_PER_STEP, dh, nh), lambda bi, i, pt: (bi * n_tiles + i, 0, 0))
    grid_spec = pltpu.PrefetchScalarGridSpec(
        num_scalar_prefetch=1,
        grid=(b, n_tiles),
        in_specs=[pl.BlockSpec((1, tm, d), lambda bi, i, pt: (bi, i, 0))]
        + [_full(a.shape) for a in (g, wg, wu, cw, cb, wd, gf)]
        + [dec, _full(bias_lanes.shape),
           pl.BlockSpec(memory_space=pl.ANY), pl.BlockSpec(memory_space=pl.ANY)],
        out_specs=[
            pl.BlockSpec((1, tm, d), lambda bi, i, pt: (bi, i, 0)),
            pl.BlockSpec((1, 2, d_ff), lambda bi, i, pt: (bi, 0, 0)),
            dec,
        ],
        scratch_shapes=[
            pltpu.VMEM((tm, d_ff), BF16),
            pltpu.VMEM((8, d_ff), F32),
            pltpu.VMEM((2, n_pages, nh, dh, page), F32),
            pltpu.VMEM((2, n_pages, nh, dh, page), F32),
            pltpu.SemaphoreType.DMA((2, 2)),
        ],
    )
    y, cs, od = pl.pallas_call(
        functools.partial(_ffn_prompt_kernel, tm=tm, layer=layer, n_pages=n_pages, page=page),
        grid_spec=grid_spec,
        out_shape=[
            jax.ShapeDtypeStruct((b, t, d), F32),
            jax.ShapeDtypeStruct((b, 2, d_ff), F32),
            jax.ShapeDtypeStruct((n_seq, dh, nh), F32),
        ],
        compiler_params=_cparams(2, VMEM_LIMIT_FFN),
        name="ffn_prompt",
    )(page_table.reshape(-1), x, g, wg, wu, cw, cb, wd, gf, qt, bias_lanes, kc, vc)
    return y, cs, jnp.transpose(od, (0, 2, 1)).reshape(n_seq, nh * dh)


def _ffn_step_kernel(x_ref, g_ref, wg_ref, wu_ref, cw_ref, cb_ref, wd_ref, gf_ref, p0_ref, p1_ref,
                     y_ref, gate_ref, act_ref):
    x = x_ref[...]
    h = _rms(x, g_ref[...]).astype(BF16)
    d_ff = wg_ref.shape[1]
    for c in range(d_ff // FF_CHUNK):
        sl = slice(c * FF_CHUNK, (c + 1) * FF_CHUNK)
        g = _dot(h, wg_ref[:, sl])
        up = _dot(h, wu_ref[:, sl])
        conv = (cb_ref[:, sl] + cw_ref[0:1, sl] * p0_ref[:, sl] + cw_ref[1:2, sl] * p1_ref[:, sl]
                + cw_ref[2:3, sl] * g)
        act_ref[:, sl] = (_gelu(conv) * up).astype(BF16)
        gate_ref[:, sl] = g
    x3 = x + _dot(act_ref[...], wd_ref[...])
    y_ref[...] = _rms(x3, gf_ref[...])


def _ffn_step(x, g, wg, wu, cw, cb, wd, gf, p0, p1):
    n, d = x.shape
    d_ff = wg.shape[1]
    return pl.pallas_call(
        _ffn_step_kernel,
        out_shape=[jax.ShapeDtypeStruct((n, d), F32), jax.ShapeDtypeStruct((n, d_ff), F32)],
        scratch_shapes=[pltpu.VMEM((n, d_ff), BF16)],
        compiler_params=pltpu.CompilerParams(vmem_limit_bytes=VMEM_LIMIT),
        name="ffn_step",
    )(x, g, wg, wu, cw, cb, wd, gf, p0, p1)


def kernel(x_prompt, x_sample, cache_sb_k, cache_sb_v, page_table, state_ssm_re, state_ssm_im, state_conv, cache_mem_k, cache_mem_v, mem_prompt, g_mix, w_in, sb_bias, lam_re, lam_im, log_dt, b_re, b_im, c_re, c_im, d_skip, w_glu, b_glu, g_sb_out, g_ssm_out, w_out, g_mem_q, g_mem_kv, w_mq, w_mk, w_mv, w_mo, g_ffn, w_gate, w_up, conv_w, conv_b, w_down, g_final):
    depth = w_in.shape[0]
    n_p, t_p, d = x_prompt.shape
    n_s = x_sample.shape[0]
    assert x_sample.shape[1] == 1
    tm = 512
    q_scale = 1.0 / math.sqrt(SB_HEAD_DIM)
    row = lambda a: a.reshape(1, -1)
    gf = row(g_final)

    yp = x_prompt.reshape(n_p * t_p, d)
    ys = x_sample.reshape(n_s, d)
    outs = {k: [] for k in ("pk", "pv", "pre", "pim", "pconv", "pmk", "pmv",
                            "sk", "sv", "sre", "sim", "sconv")}
    y_prompt = y_sample = None
    for l in range(depth):
        w_in_b = w_in[l].astype(BF16)
        w_q, w_k, w_v, w_u = (w_in_b[:, j * SB_WIDTH:(j + 1) * SB_WIDTH] for j in range(4))
        w_out_b = w_out[l].astype(BF16)
        wo_a, wo_b = w_out_b[:SB_WIDTH], w_out_b[SB_WIDTH:]
        wglu_b = w_glu[l].astype(BF16)
        wmq, wmk, wmv, wmo = (w[l].astype(BF16) for w in (w_mq, w_mk, w_mv, w_mo))
        wg, wu, wd = (w[l].astype(BF16) for w in (w_gate, w_up, w_down))
        wb, wc, lamp = _s5_prepare(lam_re[l], lam_im[l], log_dt[l], b_re[l], b_im[l], c_re[l], c_im[l])
        s5_w = (wb, wc, lamp, row(d_skip[l]), wglu_b, row(b_glu[l]))
        ffn_w = (row(g_ffn[l]), wg, wu, conv_w[l], row(conv_b[l]), wd, gf)

        q_b, kt_f, kt_b, v_b, vt_f, u_f = _in_proj_prompt(
            yp.reshape(n_p, t_p, d), row(g_mix[l]), w_q, w_k, w_v, w_u, q_scale, tm)
        o_sb = _sb_prompt(q_b, kt_b, v_b, sb_bias[l])
        o_ssm, hre, him = _s5_prompt(u_f, *s5_w)
        n_mem = mem_prompt.shape[1]
        mk_f, mk_b, mv_f, mv_b = _norm_proj(
            mem_prompt.reshape(n_p * n_mem, d), row(g_mem_kv[l]), [wmk, wmv],
            [(True, True, 1.0), (True, True, 1.0)], n_mem)
        x2 = _mem_prompt(yp.reshape(n_p, t_p, d), o_sb, o_ssm, row(g_sb_out[l]), row(g_ssm_out[l]),
                         wo_a, wo_b, row(g_mem_q[l]), wmq,
                         mk_b.reshape(n_p, n_mem, d), mv_b.reshape(n_p, n_mem, d), wmo, tm)
        qs_f, ks_f, vs_f, us_f = _norm_proj(
            ys, row(g_mix[l]), [w_q, w_k, w_v, w_u],
            [(True, False, q_scale), (True, False, 1.0), (True, False, 1.0), (True, False, 1.0)], n_s)
        y3, cs_p, os_sb = _ffn_prompt(x2, *ffn_w, qs_f, sb_bias[l], cache_sb_k, cache_sb_v,
                                      page_table, l, tm)
        if l + 1 < depth:
            raise NotImplementedError("final norm is fused into the last layer's FFN")
        y_prompt = y3
        to_cache = lambda a: jnp.transpose(a.reshape(n_p, SB_HEADS, SB_HEAD_DIM, t_p), (0, 3, 1, 2))
        outs["pk"].append(to_cache(kt_f))
        outs["pv"].append(to_cache(vt_f))
        outs["pre"].append(hre.reshape(n_p, SSM_GROUPS, SSM_STATE))
        outs["pim"].append(him.reshape(n_p, SSM_GROUPS, SSM_STATE))
        outs["pconv"].append(cs_p)
        outs["pmk"].append(mk_f.reshape(n_p, n_mem, MEM_HEADS, MEM_HEAD_DIM))
        outs["pmv"].append(mv_f.reshape(n_p, n_mem, MEM_HEADS, MEM_HEAD_DIM))

        os_ssm, hsr, hsi = _s5_step(us_f, state_ssm_re[l].reshape(n_s, SSM_CH),
                                    state_ssm_im[l].reshape(n_s, SSM_CH), *s5_w)
        x1s = _mix_out(ys, os_sb, os_ssm, row(g_sb_out[l]), row(g_ssm_out[l]), wo_a, wo_b, n_s)
        (qm_f,) = _norm_proj(x1s, row(g_mem_q[l]), [wmq],
                             [(True, False, 1.0 / math.sqrt(MEM_HEAD_DIM))], n_s)
        om = _mem_decode(qm_f.reshape(n_s, MEM_HEADS, MEM_HEAD_DIM), cache_mem_k, cache_mem_v,
                         l).reshape(n_s, d)
        x2s = _proj_residual(x1s, om, wmo)
        y3s, gate_s = _ffn_step(x2s, *ffn_w, state_conv[l][:, 0], state_conv[l][:, 1])
        y_sample = y3s
        outs["sk"].append(ks_f.reshape(n_s, 1, SB_HEADS, SB_HEAD_DIM))
        outs["sv"].append(vs_f.reshape(n_s, 1, SB_HEADS, SB_HEAD_DIM))
        outs["sre"].append(hsr.reshape(n_s, SSM_GROUPS, SSM_STATE))
        outs["sim"].append(hsi.reshape(n_s, SSM_GROUPS, SSM_STATE))
        outs["sconv"].append(jnp.stack([state_conv[l][:, 1], gate_s], axis=1))

    st = lambda k: jnp.stack(outs[k])
    return (y_prompt, y_sample.reshape(n_s, 1, d),
            st("pk"), st("pv"), st("pre"), st("pim"), st("pconv"), st("pmk"), st("pmv"),
            st("sk"), st("sv"), st("sre"), st("sim"), st("sconv"))
```
